```python
import math
import jax
import jax.numpy as jnp
from jax import lax
import numpy as np

D_MODEL = 4096
BATCH = 2
SEQ = 4096
DEPTH = 2
DEC_BATCH = 32
DEC_SEQ = 64
PAST_LEN = 2048

CHUNK = 64
HEAD_DIM = 128
N_HEADS_A = D_MODEL // HEAD_DIM
N_HEADS_B = D_MODEL // HEAD_DIM
INNER_A = N_HEADS_A * HEAD_DIM
INNER_B = N_HEADS_B * HEAD_DIM
LEFT_CHUNKS = 8
BAND_CHUNKS = LEFT_CHUNKS + 1
REL_CLIP = 128
N_REL = 2 * REL_CLIP + 1
Q_BLOCK = 128
N_A_LAYERS = DEPTH // 2
N_B_LAYERS = DEPTH - N_A_LAYERS
EPS = 1e-6
NEG_INF = -1e30

kernel_name = "yoco_stickbreak_chunkband_stream_step"


def rmsnorm(x, g):
    xf = x.astype(jnp.float32)
    y = xf * lax.rsqrt(jnp.mean(xf * xf, axis=-1, keepdims=True) + EPS)
    return (y * g.astype(jnp.float32)).astype(x.dtype)


def adaln(c, w_mod, b_mod):
    mod = jax.nn.silu(c) @ w_mod + b_mod
    shift, scale, gate = jnp.split(mod[:, None, :], 3, axis=-1)
    return shift, scale, gate


def stick_breaking(q, k, v, q_pos, k_pos):
    scale = 1.0 / math.sqrt(q.shape[-1])
    z = jnp.einsum("bqhd,bkhd->bhqk", q.astype(jnp.float32), k.astype(jnp.float32)) * scale
    causal = k_pos[None, :] < q_pos[:, None]
    log_keep = jnp.where(causal, jax.nn.log_sigmoid(-z), 0.0)
    log_rest = lax.cumsum(log_keep, axis=3, reverse=True) - log_keep
    w = jnp.where(causal, jnp.exp(jax.nn.log_sigmoid(z) + log_rest), 0.0)
    o = jnp.einsum("bhqk,bkhd->bqhd", w, v.astype(jnp.float32))
    return o.astype(q.dtype)


def stick_breaking_prompt(q, k, v):
    b, s, h, d = q.shape
    nb = s // Q_BLOCK
    q_blocks = jnp.moveaxis(q.reshape(b, nb, Q_BLOCK, h, d), 1, 0)
    k_pos = jnp.arange(s)

    def one_block(args):
        i, q_i = args
        q_pos = i * Q_BLOCK + jnp.arange(Q_BLOCK)
        return stick_breaking(q_i, k, v, q_pos, k_pos)

    o = lax.map(one_block, (jnp.arange(nb), q_blocks))
    return jnp.moveaxis(o, 0, 1).reshape(b, s, h, d)


def chunk_band_attn(q, k, v, q_pos, k_pos, rel_bias):
    scale = 1.0 / math.sqrt(q.shape[-1])
    s = jnp.einsum("bqhd,bkhd->bhqk", q.astype(jnp.float32), k.astype(jnp.float32)) * scale
    q_chunk = q_pos // CHUNK
    k_chunk = k_pos // CHUNK
    visible = ((k_pos[None, :] >= 0)
               & (k_chunk[None, :] <= q_chunk[:, None])
               & (k_chunk[None, :] >= q_chunk[:, None] - LEFT_CHUNKS))
    rel = jnp.clip(q_pos[:, None] - k_pos[None, :], -REL_CLIP, REL_CLIP) + REL_CLIP
    s = s + rel_bias.astype(jnp.float32)[:, rel][None]
    p = jax.nn.softmax(jnp.where(visible, s, NEG_INF), axis=-1)
    o = jnp.einsum("bhqk,bkhd->bqhd", p, v.astype(jnp.float32))
    return o.astype(q.dtype)


def chunk_band_prompt(q, k, v, rel_bias):
    b, s, h, d = q.shape
    nc = s // CHUNK
    pad = LEFT_CHUNKS * CHUNK
    band = BAND_CHUNKS * CHUNK
    k_pad = jnp.pad(k, ((0, 0), (pad, 0), (0, 0), (0, 0)))
    v_pad = jnp.pad(v, ((0, 0), (pad, 0), (0, 0), (0, 0)))
    q_chunks = jnp.moveaxis(q.reshape(b, nc, CHUNK, h, d), 1, 0)

    def one_chunk(args):
        ci, q_i = args
        start = ci * CHUNK
        k_band = lax.dynamic_slice_in_dim(k_pad, start, band, axis=1)
        v_band = lax.dynamic_slice_in_dim(v_pad, start, band, axis=1)
        q_pos = start + jnp.arange(CHUNK)
        k_pos = start - pad + jnp.arange(band)
        return chunk_band_attn(q_i, k_band, v_band, q_pos, k_pos, rel_bias)

    o = lax.map(one_chunk, (jnp.arange(nc), q_chunks))
    return jnp.moveaxis(o, 0, 1).reshape(b, s, h, d)


def mixer_a_in(x, c, w_mod, b_mod, g_norm, w_in):
    shift, scale, gate = adaln(c, w_mod, b_mod)
    h = rmsnorm(x, g_norm) * (1.0 + scale) + shift
    q, k, v, g = jnp.split(h @ w_in, 4, axis=-1)
    bsz, t = x.shape[0], x.shape[1]
    q = q.reshape(bsz, t, N_HEADS_A, HEAD_DIM)
    k = k.reshape(bsz, t, N_HEADS_A, HEAD_DIM)
    v = v.reshape(bsz, t, N_HEADS_A, HEAD_DIM)
    return q, k, v, g, gate


def mixer_b_in(x, c, w_mod, b_mod, g_norm, w_in):
    shift, scale, gate = adaln(c, w_mod, b_mod)
    h = rmsnorm(x, g_norm) * (1.0 + scale) + shift
    q, g = jnp.split(h @ w_in, 2, axis=-1)
    bsz, t = x.shape[0], x.shape[1]
    return q.reshape(bsz, t, N_HEADS_B, HEAD_DIM), g, gate


def mixer_out(x, o, g, gate, w_out):
    bsz, t = x.shape[0], x.shape[1]
    y = (o.reshape(bsz, t, -1) * jax.nn.silu(g)) @ w_out
    return x + gate * y


def shared_kv(x, g_kv, w_kv):
    bsz, t = x.shape[0], x.shape[1]
    k, v = jnp.split(rmsnorm(x, g_kv) @ w_kv, 2, axis=-1)
    return (k.reshape(bsz, t, N_HEADS_B, HEAD_DIM), v.reshape(bsz, t, N_HEADS_B, HEAD_DIM))


def setup_inputs(seed: int = 0) -> dict:
    key = jax.random.key(seed)
    ks = jax.random.split(key, 24)
    f32 = jnp.float32

    def nrm(k, shape, s):
        return jax.random.normal(k, shape, f32) * s

    past_b = min(LEFT_CHUNKS * CHUNK, PAST_LEN)
    ws = D_MODEL ** -0.5
    return {
        "x_prompt": nrm(ks[0], (BATCH, SEQ, D_MODEL), 1.0),
        "x_sample": nrm(ks[1], (DEC_BATCH, DEC_SEQ, D_MODEL), 1.0),
        "c_prompt": nrm(ks[2], (BATCH, D_MODEL), 1.0),
        "c_sample": nrm(ks[3], (DEC_BATCH, D_MODEL), 1.0),
        "cache_a_k": nrm(ks[4], (N_A_LAYERS, DEC_BATCH, PAST_LEN, N_HEADS_A, HEAD_DIM), 1.0),
        "cache_a_v": nrm(ks[5], (N_A_LAYERS, DEC_BATCH, PAST_LEN, N_HEADS_A, HEAD_DIM), 1.0),
        "cache_b_k": nrm(ks[6], (DEC_BATCH, past_b, N_HEADS_B, HEAD_DIM), 1.0),
        "cache_b_v": nrm(ks[7], (DEC_BATCH, past_b, N_HEADS_B, HEAD_DIM), 1.0),
        "w_mod_a": nrm(ks[8], (N_A_LAYERS, D_MODEL, 3 * D_MODEL), 0.5 * ws),
        "b_mod_a": nrm(ks[9], (N_A_LAYERS, 3 * D_MODEL), 0.02),
        "g_norm_a": 1.0 + nrm(ks[10], (N_A_LAYERS, D_MODEL), 0.02),
        "w_in_a": nrm(ks[11], (N_A_LAYERS, D_MODEL, 4 * INNER_A), ws),
        "w_out_a": nrm(ks[12], (N_A_LAYERS, INNER_A, D_MODEL), INNER_A ** -0.5),
        "g_kv": 1.0 + nrm(ks[13], (D_MODEL,), 0.02),
        "w_kv": nrm(ks[14], (D_MODEL, 2 * INNER_B), ws),
        "w_mod_b": nrm(ks[15], (N_B_LAYERS, D_MODEL, 3 * D_MODEL), 0.5 * ws),
        "b_mod_b": nrm(ks[16], (N_B_LAYERS, 3 * D_MODEL), 0.02),
        "g_norm_b": 1.0 + nrm(ks[17], (N_B_LAYERS, D_MODEL), 0.02),
        "w_in_b": nrm(ks[18], (N_B_LAYERS, D_MODEL, 2 * INNER_B), ws),
        "rel_bias_b": nrm(ks[19], (N_B_LAYERS, N_HEADS_B, N_REL), 0.5),
        "w_out_b": nrm(ks[20], (N_B_LAYERS, INNER_B, D_MODEL), INNER_B ** -0.5),
        "g_final": 1.0 + nrm(ks[21], (D_MODEL,), 0.02),
    }


def reference(x_prompt, x_sample, c_prompt, c_sample, cache_a_k, cache_a_v, cache_b_k, cache_b_v,
              w_mod_a, b_mod_a, g_norm_a, w_in_a, w_out_a, g_kv, w_kv,
              w_mod_b, b_mod_b, g_norm_b, w_in_b, rel_bias_b, w_out_b, g_final):
    xp, xs = x_prompt, x_sample
    s_len = xp.shape[1]
    t_new = xs.shape[1]
    past = cache_a_k.shape[2]
    past_b = cache_b_k.shape[1]
    q_pos_s = past + jnp.arange(t_new)
    k_pos_a_s = jnp.arange(past + t_new)
    k_pos_b_s = jnp.concatenate([past - past_b + jnp.arange(past_b), q_pos_s])
    a_k_p, a_v_p, a_k_s, a_v_s = [], [], [], []
    for layer in range(DEPTH):
        if layer < N_A_LAYERS:
            i = layer
            q, k, v, g, gate = mixer_a_in(xp, c_prompt, w_mod_a[i], b_mod_a[i], g_norm_a[i], w_in_a[i])
            xp = mixer_out(xp, stick_breaking_prompt(q, k, v), g, gate, w_out_a[i])
            a_k_p.append(k)
            a_v_p.append(v)
            q, k, v, g, gate = mixer_a_in(xs, c_sample, w_mod_a[i], b_mod_a[i], g_norm_a[i], w_in_a[i])
            k_all = jnp.concatenate([cache_a_k[i], k], axis=1)
            v_all = jnp.concatenate([cache_a_v[i], v], axis=1)
            xs = mixer_out(xs, stick_breaking(q, k_all, v_all, q_pos_s, k_pos_a_s), g, gate, w_out_a[i])
            a_k_s.append(k)
            a_v_s.append(v)
            if layer == N_A_LAYERS - 1:
                kb_p, vb_p = shared_kv(xp, g_kv, w_kv)
                kb_s, vb_s = shared_kv(xs, g_kv, w_kv)
                kb_s_all = jnp.concatenate([cache_b_k, kb_s], axis=1)
                vb_s_all = jnp.concatenate([cache_b_v, vb_s], axis=1)
        else:
            j = layer - N_A_LAYERS
            q, g, gate = mixer_b_in(xp, c_prompt, w_mod_b[j], b_mod_b[j], g_norm_b[j], w_in_b[j])
            xp = mixer_out(xp, chunk_band_prompt(q, kb_p, vb_p, rel_bias_b[j]), g, gate, w_out_b[j])
            q, g, gate = mixer_b_in(xs, c_sample, w_mod_b[j], b_mod_b[j], g_norm_b[j], w_in_b[j])
            o = chunk_band_attn(q, kb_s_all, vb_s_all, q_pos_s, k_pos_b_s, rel_bias_b[j])
            xs = mixer_out(xs, o, g, gate, w_out_b[j])
    y_prompt = rmsnorm(xp, g_final)
    y_sample = rmsnorm(xs, g_final)
    keep_b = min(LEFT_CHUNKS * CHUNK, s_len)
    return (y_prompt, y_sample,
            jnp.stack(a_k_p), jnp.stack(a_v_p), jnp.stack(a_k_s), jnp.stack(a_v_s),
            kb_p[:, s_len - keep_b:], vb_p[:, s_len - keep_b:], kb_s, vb_s)
```

```python
import functools
import math

import jax
import jax.numpy as jnp
from jax import lax
from jax.experimental import pallas as pl
from jax.experimental.pallas import tpu as pltpu

F32 = jnp.float32
BF16 = jnp.bfloat16

HEAD_DIM = 128
CHUNK = 64
LEFT_CHUNKS = 8
REL_CLIP = 128
EPS = 1e-6
NEG_INF = -1e30

V7X_VMEM_LIMIT_BYTES = 52 * 1024 * 1024
LANES = 128
BF16_SUBLANES = 16

MM_TM = 1024
MM_TN = 512
NORM_TM = 256
MOD_TN = 512
ATTN_T = 256
SAMPLE_HEADS = 8


def _params(n_grid):
    return pltpu.CompilerParams(
        dimension_semantics=("arbitrary",) * n_grid,
        vmem_limit_bytes=V7X_VMEM_LIMIT_BYTES)


def _silu(x):
    return x / (1.0 + jnp.exp(-x))


def _dot(a, b):
    return jnp.dot(a, b, preferred_element_type=F32)


def _dot_nt(a, b):
    return lax.dot_general(a, b, (((1,), (1,)), ((), ())), preferred_element_type=F32)


def _mod_kernel(c_ref, wa_ref, ba_ref, wb_ref, bb_ref, oa_ref, ob_ref):
    a = _silu(c_ref[...]).astype(BF16)
    oa_ref[...] = _dot(a, wa_ref[...].astype(BF16)) + ba_ref[...]
    ob_ref[...] = _dot(a, wb_ref[...].astype(BF16)) + bb_ref[...]


def _adaln_tables(c_all, w_a, b_a, w_b, b_b):
    rows, d = c_all.shape
    n = w_a.shape[1]
    w_spec = pl.BlockSpec((d, MOD_TN), lambda j: (0, j))
    b_spec = pl.BlockSpec((1, MOD_TN), lambda j: (0, j))
    o_spec = pl.BlockSpec((rows, MOD_TN), lambda j: (0, j))
    return pl.pallas_call(
        _mod_kernel,
        out_shape=(jax.ShapeDtypeStruct((rows, n), F32),) * 2,
        grid=(n // MOD_TN,),
        in_specs=[pl.BlockSpec((rows, d), lambda j: (0, 0)), w_spec, b_spec, w_spec, b_spec],
        out_specs=(o_spec, o_spec),
        compiler_params=_params(1),
        name="adaln_mod",
    )(c_all, w_a, b_a.reshape(1, n), w_b, b_b.reshape(1, n))


def _split_mod(mod, d):
    rows = mod.shape[0]
    return tuple(mod[:, k * d:(k + 1) * d].reshape(rows, 1, d) for k in range(3))


def _rms_scale(x):
    return lax.rsqrt(jnp.mean(x * x, axis=-1, keepdims=True) + EPS)


def _modulate(y, shift_ref, scale_ref):
    nb = shift_ref.shape[0]
    tm, d = y.shape
    y3 = y.reshape(nb, tm // nb, d)
    return (y3 * (1.0 + scale_ref[...]) + shift_ref[...]).reshape(tm, d)


def _prenorm_kernel(xp_ref, xs_ref, g_ref, shp_ref, scp_ref, shs_ref, scs_ref, o_ref, *, n_prompt_tiles):
    i = pl.program_id(0)

    def run(x_ref, shift_ref, scale_ref):
        x = x_ref[...]
        y = x * _rms_scale(x) * g_ref[...]
        o_ref[...] = _modulate(y, shift_ref, scale_ref).astype(o_ref.dtype)

    @pl.when(i < n_prompt_tiles)
    def _():
        run(xp_ref, shp_ref, scp_ref)

    @pl.when(i >= n_prompt_tiles)
    def _():
        run(xs_ref, shs_ref, scs_ref)


def _group_specs(tm, d, n_prompt_tiles, prompt_rpb, sample_rpb, n_sample_batches):
    tiles_per_prompt_batch = prompt_rpb // tm
    nb_s = tm // sample_rpb
    rows_p = pl.BlockSpec((tm, d), lambda i: (jnp.minimum(i, n_prompt_tiles - 1), 0))
    rows_s = pl.BlockSpec((tm, d), lambda i: (jnp.maximum(i - n_prompt_tiles, 0), 0))
    tab_p = pl.BlockSpec(
        (1, 1, d),
        lambda i: (n_sample_batches + jnp.minimum(i, n_prompt_tiles - 1) // tiles_per_prompt_batch, 0, 0))
    tab_s = pl.BlockSpec((nb_s, 1, d), lambda i: (jnp.maximum(i - n_prompt_tiles, 0), 0, 0))
    return rows_p, rows_s, tab_p, tab_s


def _prenorm(x_p, x_s, g, shift, scale, prompt_rpb, sample_rpb):
    mp, d = x_p.shape
    ms = x_s.shape[0]
    tm = NORM_TM
    npt = mp // tm
    rows_p, rows_s, tab_p, tab_s = _group_specs(tm, d, npt, prompt_rpb, sample_rpb, ms // sample_rpb)
    return pl.pallas_call(
        functools.partial(_prenorm_kernel, n_prompt_tiles=npt),
        out_shape=jax.ShapeDtypeStruct((mp + ms, d), BF16),
        grid=((mp + ms) // tm,),
        in_specs=[rows_p, rows_s, pl.BlockSpec((1, d), lambda i: (0, 0)), tab_p, tab_p, tab_s, tab_s],
        out_specs=pl.BlockSpec((tm, d), lambda i: (i, 0)),
        compiler_params=_params(1),
        name="prenorm",
    )(x_p, x_s, g.reshape(1, d), shift, scale, shift, scale)


def _dualnorm_kernel(xp_ref, xs_ref, gkv_ref, gb_ref, shp_ref, scp_ref, shs_ref, scs_ref,
                     okv_ref, ob_ref, *, n_prompt_tiles):
    i = pl.program_id(0)

    def run(x_ref, shift_ref, scale_ref):
        x = x_ref[...]
        y = x * _rms_scale(x)
        okv_ref[...] = (y * gkv_ref[...]).astype(okv_ref.dtype)
        ob_ref[...] = _modulate(y * gb_ref[...], shift_ref, scale_ref).astype(ob_ref.dtype)

    @pl.when(i < n_prompt_tiles)
    def _():
        run(xp_ref, shp_ref, scp_ref)

    @pl.when(i >= n_prompt_tiles)
    def _():
        run(xs_ref, shs_ref, scs_ref)


def _dualnorm(x_p, x_s, g_kv, g_b, shift, scale, prompt_rpb, sample_rpb):
    mp, d = x_p.shape
    ms = x_s.shape[0]
    tm = NORM_TM
    npt = mp // tm
    rows_p, rows_s, tab_p, tab_s = _group_specs(tm, d, npt, prompt_rpb, sample_rpb, ms // sample_rpb)
    g_spec = pl.BlockSpec((1, d), lambda i: (0, 0))
    o_spec = pl.BlockSpec((tm, d), lambda i: (i, 0))
    return pl.pallas_call(
        functools.partial(_dualnorm_kernel, n_prompt_tiles=npt),
        out_shape=(jax.ShapeDtypeStruct((mp + ms, d), BF16),) * 2,
        grid=((mp + ms) // tm,),
        in_specs=[rows_p, rows_s, g_spec, g_spec, tab_p, tab_p, tab_s, tab_s],
        out_specs=(o_spec, o_spec),
        compiler_params=_params(1),
        name="dualnorm",
    )(x_p, x_s, g_kv.reshape(1, d), g_b.reshape(1, d), shift, scale, shift, scale)


def _finalnorm_kernel(x_ref, g_ref, o_ref):
    x = x_ref[...]
    o_ref[...] = x * _rms_scale(x) * g_ref[...]


def _finalnorm(x, g):
    m, d = x.shape
    tm = NORM_TM
    return pl.pallas_call(
        _finalnorm_kernel,
        out_shape=jax.ShapeDtypeStruct((m, d), F32),
        grid=(m // tm,),
        in_specs=[pl.BlockSpec((tm, d), lambda i: (i, 0)), pl.BlockSpec((1, d), lambda i: (0, 0))],
        out_specs=pl.BlockSpec((tm, d), lambda i: (i, 0)),
        compiler_params=_params(1),
        name="finalnorm",
    )(x, g.reshape(1, d))


def _mm_kernel(h_ref, w_ref, o_ref, wbf_ref, *, silu):
    @pl.when(pl.program_id(1) == 0)
    def _():
        wbf_ref[...] = w_ref[...].astype(BF16)

    acc = _dot(h_ref[...], wbf_ref[...])
    if silu:
        acc = _silu(acc)
    o_ref[...] = acc.astype(o_ref.dtype)


def _mm(h, w, col0, n_out, out_dtype, *, silu=False, row0=0, rows=None):
    k = h.shape[1]
    rows = h.shape[0] if rows is None else rows
    tm, tn = MM_TM, MM_TN
    i0, j0 = row0 // tm, col0 // tn
    return pl.pallas_call(
        functools.partial(_mm_kernel, silu=silu),
        out_shape=jax.ShapeDtypeStruct((rows, n_out), out_dtype),
        grid=(n_out // tn, rows // tm),
        in_specs=[pl.BlockSpec((tm, k), lambda j, i: (i0 + i, 0)),
                  pl.BlockSpec((k, tn), lambda j, i: (0, j0 + j))],
        out_specs=pl.BlockSpec((tm, tn), lambda j, i: (i, j)),
        scratch_shapes=[pltpu.VMEM((k, tn), BF16)],
        compiler_params=_params(2),
        name="proj",
    )(h, w)


def _mm_resid_kernel(u_ref, w_ref, x_ref, gate_ref, o_ref, wbf_ref):
    @pl.when(pl.program_id(1) == 0)
    def _():
        wbf_ref[...] = w_ref[...].astype(BF16)

    acc = _dot(u_ref[...], wbf_ref[...])
    nb = gate_ref.shape[0]
    tm, tn = acc.shape
    y = acc.reshape(nb, tm // nb, tn) * gate_ref[...]
    o_ref[...] = x_ref[...] + y.reshape(tm, tn)


def _mm_resid(u, w, x, gate, rows_per_batch, table_row0):
    m, k = u.shape
    n = w.shape[1]
    tm, tn = MM_TM, MM_TN
    if rows_per_batch >= tm:
        nb = 1
        tiles_per_batch = rows_per_batch // tm
        gate_map = lambda j, i: (table_row0 + i // tiles_per_batch, 0, j)
    else:
        nb = tm // rows_per_batch
        blk0 = table_row0 // nb
        gate_map = lambda j, i: (blk0 + i, 0, j)
    return pl.pallas_call(
        _mm_resid_kernel,
        out_shape=jax.ShapeDtypeStruct((m, n), F32),
        grid=(n // tn, m // tm),
        in_specs=[pl.BlockSpec((tm, k), lambda j, i: (i, 0)),
                  pl.BlockSpec((k, tn), lambda j, i: (0, j)),
                  pl.BlockSpec((tm, tn), lambda j, i: (i, j)),
                  pl.BlockSpec((nb, 1, tn), gate_map)],
        out_specs=pl.BlockSpec((tm, tn), lambda j, i: (i, j)),
        scratch_shapes=[pltpu.VMEM((k, tn), BF16)],
        compiler_params=_params(2),
        name="out_proj",
    )(u, w, x, gate)


def _softplus(z):
    return jnp.maximum(z, 0.0) + jnp.log(1.0 + jnp.exp(-jnp.abs(z)))


def _suffix_sum(lk, tri):
    hi = lk.astype(BF16)
    lo = (lk - hi.astype(F32)).astype(BF16)
    return _dot(hi, tri) + _dot(lo, tri)


def _stick_block(q, kblk, vblk, tri, carry, acc, causal):
    z = _dot_nt(q, kblk) * (1.0 / math.sqrt(HEAD_DIM))
    lk = -_softplus(z)
    if causal is not None:
        lk = jnp.where(causal, lk, 0.0)
    incl = _suffix_sum(lk, tri)
    w = jnp.exp(z + incl + carry)
    if causal is not None:
        w = jnp.where(causal, w, 0.0)
    acc = acc + _dot(w.astype(BF16), vblk)
    carry = carry + incl[:, :1]
    return carry, acc


def _strict_lower(tq, tk):
    return lax.broadcasted_iota(jnp.int32, (tq, tk), 1) < lax.broadcasted_iota(jnp.int32, (tq, tk), 0)


def _stick_prompt_kernel(q_ref, k_ref, v_ref, sg_ref, tri_ref, o_ref, kbf_ref, vbf_ref):
    qi = pl.program_id(2)
    t = q_ref.shape[1]

    @pl.when(qi == 0)
    def _():
        kbf_ref[...] = k_ref[0].astype(BF16)
        vbf_ref[...] = v_ref[0].astype(BF16)

    q = q_ref[0]
    tri = tri_ref[...]

    def blocks(kb):
        k0 = pl.multiple_of(kb * t, t)
        return kbf_ref[pl.ds(k0, t), :], vbf_ref[pl.ds(k0, t), :]

    kblk, vblk = blocks(qi)
    carry, acc = _stick_block(q, kblk, vblk, tri, jnp.zeros((t, 1), F32),
                              jnp.zeros((t, HEAD_DIM), F32), _strict_lower(t, t))

    def body(it, state):
        kblk, vblk = blocks(qi - 1 - it)
        return _stick_block(q, kblk, vblk, tri, state[0], state[1], None)

    carry, acc = lax.fori_loop(0, qi, body, (carry, acc))
    o_ref[0] = (acc * sg_ref[0].astype(F32)).astype(o_ref.dtype)


def _tri_incl(t):
    return (jnp.arange(t)[:, None] >= jnp.arange(t)[None, :]).astype(BF16)


def _stick_prompt(q, k, v, sg, batch, seq):
    d = k.shape[1]
    n_heads = d // HEAD_DIM
    t = ATTN_T
    tiles = seq // t
    flat = lambda a: a.reshape(1, a.shape[0], d)
    blk_in = pl.BlockSpec((1, t, HEAD_DIM), lambda b, h, i: (0, b * tiles + i, h))
    kv = pl.BlockSpec((1, seq, HEAD_DIM), lambda b, h, i: (0, b, h))
    out = pl.pallas_call(
        _stick_prompt_kernel,
        out_shape=jax.ShapeDtypeStruct((batch, seq, d), BF16),
        grid=(batch, n_heads, tiles),
        in_specs=[blk_in, kv, kv, blk_in, pl.BlockSpec((t, t), lambda b, h, i: (0, 0))],
        out_specs=pl.BlockSpec((1, t, HEAD_DIM), lambda b, h, i: (b, i, h)),
        scratch_shapes=[pltpu.VMEM((seq, HEAD_DIM), BF16), pltpu.VMEM((seq, HEAD_DIM), BF16)],
        compiler_params=_params(3),
        name="stick_prompt",
    )(flat(q), flat(k), flat(v), flat(sg), _tri_incl(t))
    return out.reshape(batch * seq, d)


def _stick_sample_kernel(q_ref, ck_ref, cv_ref, k_ref, v_ref, sg_ref, tri_ref, o_ref, *, kblock):
    t_new = q_ref.shape[1]
    past = ck_ref.shape[1]
    tri = tri_ref[...]
    tri_new = tri_ref[:t_new, :t_new]
    causal = _strict_lower(t_new, t_new)
    for hh in range(ck_ref.shape[2]):
        lanes = slice(hh * HEAD_DIM, (hh + 1) * HEAD_DIM)
        q = q_ref[0, :, lanes]
        carry, acc = _stick_block(
            q, k_ref[0, :, lanes].astype(BF16), v_ref[0, :, lanes].astype(BF16), tri_new,
            jnp.zeros((t_new, 1), F32), jnp.zeros((t_new, HEAD_DIM), F32), causal)

        def body(it, state):
            k0 = pl.multiple_of(past - (it + 1) * kblock, kblock)
            kblk = ck_ref[0, pl.ds(k0, kblock), hh, :].astype(BF16)
            vblk = cv_ref[0, pl.ds(k0, kblock), hh, :].astype(BF16)
            return _stick_block(q, kblk, vblk, tri, state[0], state[1], None)

        carry, acc = lax.fori_loop(0, past // kblock, body, (carry, acc))
        o_ref[0, :, lanes] = (acc * sg_ref[0, :, lanes].astype(F32)).astype(o_ref.dtype)


def _stick_sample(q, cache_k, cache_v, k, v, sg, row0):
    b, past, n_heads, _ = cache_k.shape
    d = n_heads * HEAD_DIM
    t_new = k.shape[0] // b
    hw = SAMPLE_HEADS * HEAD_DIM
    t = ATTN_T
    blk0 = row0 // t_new
    q3 = q.reshape(q.shape[0] // t_new, t_new, d)
    sg3 = sg.reshape(q3.shape)
    merged = pl.BlockSpec((1, t_new, hw), lambda bi, g: (blk0 + bi, 0, g))
    new = pl.BlockSpec((1, t_new, hw), lambda bi, g: (bi, 0, g))
    cache = pl.BlockSpec((1, past, SAMPLE_HEADS, HEAD_DIM), lambda bi, g: (bi, 0, g, 0))
    out = pl.pallas_call(
        functools.partial(_stick_sample_kernel, kblock=t),
        out_shape=jax.ShapeDtypeStruct((b, t_new, d), BF16),
        grid=(b, d // hw),
        in_specs=[merged, cache, cache, new, new, merged, pl.BlockSpec((t, t), lambda bi, g: (0, 0))],
        out_specs=new,
        compiler_params=_params(2),
        name="stick_sample",
    )(q3, cache_k, cache_v, k.reshape(b, t_new, d), v.reshape(b, t_new, d), sg3, _tri_incl(t))
    return out.reshape(b * t_new, d)


def _band_prompt_kernel(q_ref, k_ref, v_ref, sg_ref, bias_ref, o_ref, kbf_ref, vbf_ref):
    qi = pl.program_id(2)
    t = q_ref.shape[1]
    n_kb = bias_ref.shape[1]

    @pl.when(qi == 0)
    def _():
        kbf_ref[...] = k_ref[0].astype(BF16)
        vbf_ref[...] = v_ref[0].astype(BF16)

    q = q_ref[0]
    scores, vblks = [], []
    for dd in range(n_kb):
        kb = qi - (n_kb - 1) + dd
        k0 = pl.multiple_of(jnp.maximum(kb, 0) * t, t)
        s = _dot_nt(q, kbf_ref[pl.ds(k0, t), :]) * (1.0 / math.sqrt(HEAD_DIM)) + bias_ref[0, dd]
        scores.append(jnp.where(kb >= 0, s, NEG_INF))
        vblks.append(vbf_ref[pl.ds(k0, t), :])
    m = scores[0].max(axis=-1, keepdims=True)
    for s in scores[1:]:
        m = jnp.maximum(m, s.max(axis=-1, keepdims=True))
    acc = jnp.zeros((t, HEAD_DIM), F32)
    denom = jnp.zeros((t, 1), F32)
    for s, vblk in zip(scores, vblks):
        p = jnp.exp(s - m)
        denom = denom + p.sum(axis=-1, keepdims=True)
        acc = acc + _dot(p.astype(BF16), vblk)
    o_ref[0] = (acc / denom * sg_ref[0].astype(F32)).astype(o_ref.dtype)


def _band_bias_prompt(rel_bias, t):
    n_kb = (LEFT_CHUNKS * CHUNK) // t + 1
    r = jnp.arange(t)[None, :, None]
    c = jnp.arange(t)[None, None, :]
    back = (n_kb - 1 - jnp.arange(n_kb))[:, None, None]
    rel = back * t + r - c
    chunk_diff = back * (t // CHUNK) + r // CHUNK - c // CHUNK
    visible = (chunk_diff >= 0) & (chunk_diff <= LEFT_CHUNKS)
    idx = jnp.clip(rel, -REL_CLIP, REL_CLIP) + REL_CLIP
    return jnp.where(visible[None], rel_bias.astype(F32)[:, idx], NEG_INF)


def _band_prompt(q, k, v, sg, rel_bias, batch, seq):
    d = k.shape[1]
    n_heads = d // HEAD_DIM
    t = ATTN_T
    tiles = seq // t
    bias = _band_bias_prompt(rel_bias, t)
    blk_in = pl.BlockSpec((1, t, HEAD_DIM), lambda b, h, i: (0, b * tiles + i, h))
    kv = pl.BlockSpec((1, seq, HEAD_DIM), lambda b, h, i: (0, b, h))
    flat = lambda a: a.reshape(1, a.shape[0], d)
    out = pl.pallas_call(
        _band_prompt_kernel,
        out_shape=jax.ShapeDtypeStruct((batch, seq, d), BF16),
        grid=(batch, n_heads, tiles),
        in_specs=[blk_in, kv, kv, blk_in,
                  pl.BlockSpec((1,) + bias.shape[1:], lambda b, h, i: (h, 0, 0, 0))],
        out_specs=pl.BlockSpec((1, t, HEAD_DIM), lambda b, h, i: (b, i, h)),
        scratch_shapes=[pltpu.VMEM((seq, HEAD_DIM), BF16), pltpu.VMEM((seq, HEAD_DIM), BF16)],
        compiler_params=_params(3),
        name="band_prompt",
    )(flat(q), flat(k), flat(v), flat(sg), bias)
    return out.reshape(batch * seq, d)


def _band_sample_kernel(q_ref, ck_ref, cv_ref, k_ref, v_ref, sg_ref, bias_ref, o_ref):
    past = ck_ref.shape[1]
    for hh in range(ck_ref.shape[2]):
        lanes = slice(hh * HEAD_DIM, (hh + 1) * HEAD_DIM)
        q = q_ref[0, :, lanes]
        scale = 1.0 / math.sqrt(HEAD_DIM)
        s_old = _dot_nt(q, ck_ref[0, :, hh, :].astype(BF16)) * scale + bias_ref[hh, :, :past]
        s_new = _dot_nt(q, k_ref[0, :, lanes].astype(BF16)) * scale + bias_ref[hh, :, past:]
        m = jnp.maximum(s_old.max(axis=-1, keepdims=True), s_new.max(axis=-1, keepdims=True))
        p_old = jnp.exp(s_old - m)
        p_new = jnp.exp(s_new - m)
        denom = p_old.sum(axis=-1, keepdims=True) + p_new.sum(axis=-1, keepdims=True)
        acc = (_dot(p_old.astype(BF16), cv_ref[0, :, hh, :].astype(BF16))
               + _dot(p_new.astype(BF16), v_ref[0, :, lanes].astype(BF16)))
        o_ref[0, :, lanes] = (acc / denom * sg_ref[0, :, lanes].astype(F32)).astype(o_ref.dtype)


def _band_bias_sample(rel_bias, past_total, past_b, t_new):
    q_pos = past_total + jnp.arange(t_new)
    k_pos = jnp.concatenate([past_total - past_b + jnp.arange(past_b), q_pos])
    q_chunk = q_pos // CHUNK
    k_chunk = k_pos // CHUNK
    visible = ((k_pos[None, :] >= 0) & (k_chunk[None, :] <= q_chunk[:, None])
               & (k_chunk[None, :] >= q_chunk[:, None] - LEFT_CHUNKS))
    idx = jnp.clip(q_pos[:, None] - k_pos[None, :], -REL_CLIP, REL_CLIP) + REL_CLIP
    return jnp.where(visible[None], rel_bias.astype(F32)[:, idx], NEG_INF)


def _band_sample(q, cache_k, cache_v, k, v, sg, rel_bias, row0, past_total):
    b, past_b, n_heads, _ = cache_k.shape
    d = n_heads * HEAD_DIM
    t_new = (q.shape[0] - row0) // b
    hw = SAMPLE_HEADS * HEAD_DIM
    blk0 = row0 // t_new
    bias = _band_bias_sample(rel_bias, past_total, past_b, t_new)
    r3 = lambda a: a.reshape(a.shape[0] // t_new, t_new, d)
    merged = pl.BlockSpec((1, t_new, hw), lambda bi, g: (blk0 + bi, 0, g))
    cache = pl.BlockSpec((1, past_b, SAMPLE_HEADS, HEAD_DIM), lambda bi, g: (bi, 0, g, 0))
    out = pl.pallas_call(
        _band_sample_kernel,
        out_shape=jax.ShapeDtypeStruct((b, t_new, d), BF16),
        grid=(b, d // hw),
        in_specs=[merged, cache, cache, merged, merged, merged,
                  pl.BlockSpec((SAMPLE_HEADS,) + bias.shape[1:], lambda bi, g: (g, 0, 0))],
        out_specs=pl.BlockSpec((1, t_new, hw), lambda bi, g: (bi, 0, g)),
        compiler_params=_params(2),
        name="band_sample",
    )(r3(q), cache_k, cache_v, r3(k), r3(v), r3(sg), bias)
    return out.reshape(b * t_new, d)


def kernel(x_prompt, x_sample, c_prompt, c_sample, cache_a_k, cache_a_v, cache_b_k, cache_b_v, w_mod_a, b_mod_a, g_norm_a, w_in_a, w_out_a, g_kv, w_kv, w_mod_b, b_mod_b, g_norm_b, w_in_b, rel_bias_b, w_out_b, g_final):
    bp, seq, d = x_prompt.shape
    bs, t_new, _ = x_sample.shape
    n_heads = d // HEAD_DIM
    past = cache_a_k.shape[2]
    past_b = cache_b_k.shape[1]
    mp, ms = bp * seq, bs * t_new
    assert w_mod_a.shape[0] == 1 and w_mod_b.shape[0] == 1, "one layer of each mixer"
    assert seq % MM_TM == 0 and mp % MM_TM == 0 and ms % MM_TM == 0 and MM_TM % t_new == 0
    assert bs % (MM_TM // t_new) == 0 and bs % (NORM_TM // t_new) == 0
    assert seq % ATTN_T == 0 and past % ATTN_T == 0 and (LEFT_CHUNKS * CHUNK) % ATTN_T == 0

    xp = x_prompt.reshape(mp, d)
    xs = x_sample.reshape(ms, d)

    n_c = bs + bp
    pad = -n_c % BF16_SUBLANES
    c_all = jnp.concatenate([c_sample, c_prompt, jnp.zeros((pad, d), F32)], axis=0)
    mod_a, mod_b = _adaln_tables(c_all, w_mod_a[0], b_mod_a[0], w_mod_b[0], b_mod_b[0])
    shift_a, scale_a, gate_a = _split_mod(mod_a, d)
    shift_b, scale_b, gate_b = _split_mod(mod_b, d)

    h_a = _prenorm(xp, xs, g_norm_a[0], shift_a, scale_a, seq, t_new)
    w_in = w_in_a[0]
    q_a = _mm(h_a, w_in, 0, d, BF16)
    k_p = _mm(h_a, w_in, d, d, F32, row0=0, rows=mp)
    k_s = _mm(h_a, w_in, d, d, F32, row0=mp, rows=ms)
    v_p = _mm(h_a, w_in, 2 * d, d, F32, row0=0, rows=mp)
    v_s = _mm(h_a, w_in, 2 * d, d, F32, row0=mp, rows=ms)
    sg_a = _mm(h_a, w_in, 3 * d, d, BF16, silu=True)

    u_p = _stick_prompt(q_a, k_p, v_p, sg_a, bp, seq)
    u_s = _stick_sample(q_a, cache_a_k[0], cache_a_v[0], k_s, v_s, sg_a, mp)
    x1_p = _mm_resid(u_p, w_out_a[0], xp, gate_a, seq, bs)
    x1_s = _mm_resid(u_s, w_out_a[0], xs, gate_a, t_new, 0)

    h_kv, h_b = _dualnorm(x1_p, x1_s, g_kv, g_norm_b[0], shift_b, scale_b, seq, t_new)
    kb = _mm(h_kv, w_kv, 0, d, F32)
    vb = _mm(h_kv, w_kv, d, d, F32)
    q_b = _mm(h_b, w_in_b[0], 0, d, BF16)
    sg_b = _mm(h_b, w_in_b[0], d, d, BF16, silu=True)

    ub_p = _band_prompt(q_b, kb, vb, sg_b, rel_bias_b[0], bp, seq)
    ub_s = _band_sample(q_b, cache_b_k, cache_b_v, kb, vb, sg_b, rel_bias_b[0], mp, past)
    x2_p = _mm_resid(ub_p, w_out_b[0], x1_p, gate_b, seq, bs)
    x2_s = _mm_resid(ub_s, w_out_b[0], x1_s, gate_b, t_new, 0)

    y_p = _finalnorm(x2_p, g_final).reshape(bp, seq, d)
    y_s = _finalnorm(x2_s, g_final).reshape(bs, t_new, d)

    keep_b = min(LEFT_CHUNKS * CHUNK, seq)
    heads = lambda a, b, t: a.reshape(b, t, n_heads, HEAD_DIM)
    kb_p, vb_p = heads(kb[:mp], bp, seq), heads(vb[:mp], bp, seq)
    return (y_p, y_s,
            heads(k_p, bp, seq)[None], heads(v_p, bp, seq)[None],
            heads(k_s, bs, t_new)[None], heads(v_s, bs, t_new)[None],
            kb_p[:, seq - keep_b:], vb_p[:, seq - keep_b:],
            heads(kb[mp:], bs, t_new), heads(vb[mp:], bs, t_new))
```

```python
import functools
import math

import jax
import jax.numpy as jnp
from jax import lax
from jax.experimental import pallas as pl
from jax.experimental.pallas import tpu as pltpu

F32 = jnp.float32
BF16 = jnp.bfloat16

HEAD_DIM = 128
CHUNK = 64
LEFT_CHUNKS = 8
REL_CLIP = 128
EPS = 1e-6
NEG_INF = -1e30

V7X_VMEM_LIMIT_BYTES = 52 * 1024 * 1024
LANES = 128
BF16_SUBLANES = 16

MM_TM = 1024
MM_TN = 512
NORM_TM = 256
MOD_TN = 512
ATTN_T = 256
STICK_TQ = 512
STICK_HEADS = 1
BAND_HEADS = 2

F32_EXP_UNDERFLOW_LOG = -104.0
SAMPLE_HEADS = 8


def _params(n_grid):
    return pltpu.CompilerParams(
        dimension_semantics=("arbitrary",) * n_grid,
        vmem_limit_bytes=V7X_VMEM_LIMIT_BYTES)


def _silu(x):
    return x / (1.0 + jnp.exp(-x))


def _dot(a, b):
    return jnp.dot(a, b, preferred_element_type=F32)


def _dot_nt(a, b):
    return lax.dot_general(a, b, (((1,), (1,)), ((), ())), preferred_element_type=F32)


def _mod_kernel(c_ref, wa_ref, ba_ref, wb_ref, bb_ref, oa_ref, ob_ref):
    a = _silu(c_ref[...]).astype(BF16)
    oa_ref[...] = _dot(a, wa_ref[...].astype(BF16)) + ba_ref[...]
    ob_ref[...] = _dot(a, wb_ref[...].astype(BF16)) + bb_ref[...]


def _adaln_tables(c_all, w_a, b_a, w_b, b_b):
    rows, d = c_all.shape
    n = w_a.shape[1]
    w_spec = pl.BlockSpec((d, MOD_TN), lambda j: (0, j))
    b_spec = pl.BlockSpec((1, MOD_TN), lambda j: (0, j))
    o_spec = pl.BlockSpec((rows, MOD_TN), lambda j: (0, j))
    return pl.pallas_call(
        _mod_kernel,
        out_shape=(jax.ShapeDtypeStruct((rows, n), F32),) * 2,
        grid=(n // MOD_TN,),
        in_specs=[pl.BlockSpec((rows, d), lambda j: (0, 0)), w_spec, b_spec, w_spec, b_spec],
        out_specs=(o_spec, o_spec),
        compiler_params=_params(1),
        name="adaln_mod",
    )(c_all, w_a, b_a.reshape(1, n), w_b, b_b.reshape(1, n))


def _split_mod(mod, d):
    rows = mod.shape[0]
    return tuple(mod[:, k * d:(k + 1) * d].reshape(rows, 1, d) for k in range(3))


def _rms_scale(x):
    return lax.rsqrt(jnp.mean(x * x, axis=-1, keepdims=True) + EPS)


def _modulate(y, shift_ref, scale_ref):
    nb = shift_ref.shape[0]
    tm, d = y.shape
    y3 = y.reshape(nb, tm // nb, d)
    return (y3 * (1.0 + scale_ref[...]) + shift_ref[...]).reshape(tm, d)


def _prenorm_kernel(xp_ref, xs_ref, g_ref, shp_ref, scp_ref, shs_ref, scs_ref, o_ref, *, n_prompt_tiles):
    i = pl.program_id(0)

    def run(x_ref, shift_ref, scale_ref):
        x = x_ref[...]
        y = x * _rms_scale(x) * g_ref[...]
        o_ref[...] = _modulate(y, shift_ref, scale_ref).astype(o_ref.dtype)

    @pl.when(i < n_prompt_tiles)
    def _():
        run(xp_ref, shp_ref, scp_ref)

    @pl.when(i >= n_prompt_tiles)
    def _():
        run(xs_ref, shs_ref, scs_ref)


def _group_specs(tm, d, n_prompt_tiles, prompt_rpb, sample_rpb, n_sample_batches):
    tiles_per_prompt_batch = prompt_rpb // tm
    nb_s = tm // sample_rpb
    rows_p = pl.BlockSpec((tm, d), lambda i: (jnp.minimum(i, n_prompt_tiles - 1), 0))
    rows_s = pl.BlockSpec((tm, d), lambda i: (jnp.maximum(i - n_prompt_tiles, 0), 0))
    tab_p = pl.BlockSpec(
        (1, 1, d),
        lambda i: (n_sample_batches + jnp.minimum(i, n_prompt_tiles - 1) // tiles_per_prompt_batch, 0, 0))
    tab_s = pl.BlockSpec((nb_s, 1, d), lambda i: (jnp.maximum(i - n_prompt_tiles, 0), 0, 0))
    return rows_p, rows_s, tab_p, tab_s


def _prenorm(x_p, x_s, g, shift, scale, prompt_rpb, sample_rpb):
    mp, d = x_p.shape
    ms = x_s.shape[0]
    tm = NORM_TM
    npt = mp // tm
    rows_p, rows_s, tab_p, tab_s = _group_specs(tm, d, npt, prompt_rpb, sample_rpb, ms // sample_rpb)
    return pl.pallas_call(
        functools.partial(_prenorm_kernel, n_prompt_tiles=npt),
        out_shape=jax.ShapeDtypeStruct((mp + ms, d), BF16),
        grid=((mp + ms) // tm,),
        in_specs=[rows_p, rows_s, pl.BlockSpec((1, d), lambda i: (0, 0)), tab_p, tab_p, tab_s, tab_s],
        out_specs=pl.BlockSpec((tm, d), lambda i: (i, 0)),
        compiler_params=_params(1),
        name="prenorm",
    )(x_p, x_s, g.reshape(1, d), shift, scale, shift, scale)


def _dualnorm_kernel(xp_ref, xs_ref, gkv_ref, gb_ref, shp_ref, scp_ref, shs_ref, scs_ref,
                     okv_ref, ob_ref, *, n_prompt_tiles):
    i = pl.program_id(0)

    def run(x_ref, shift_ref, scale_ref):
        x = x_ref[...]
        y = x * _rms_scale(x)
        okv_ref[...] = (y * gkv_ref[...]).astype(okv_ref.dtype)
        ob_ref[...] = _modulate(y * gb_ref[...], shift_ref, scale_ref).astype(ob_ref.dtype)

    @pl.when(i < n_prompt_tiles)
    def _():
        run(xp_ref, shp_ref, scp_ref)

    @pl.when(i >= n_prompt_tiles)
    def _():
        run(xs_ref, shs_ref, scs_ref)


def _dualnorm(x_p, x_s, g_kv, g_b, shift, scale, prompt_rpb, sample_rpb):
    mp, d = x_p.shape
    ms = x_s.shape[0]
    tm = NORM_TM
    npt = mp // tm
    rows_p, rows_s, tab_p, tab_s = _group_specs(tm, d, npt, prompt_rpb, sample_rpb, ms // sample_rpb)
    g_spec = pl.BlockSpec((1, d), lambda i: (0, 0))
    o_spec = pl.BlockSpec((tm, d), lambda i: (i, 0))
    return pl.pallas_call(
        functools.partial(_dualnorm_kernel, n_prompt_tiles=npt),
        out_shape=(jax.ShapeDtypeStruct((mp + ms, d), BF16),) * 2,
        grid=((mp + ms) // tm,),
        in_specs=[rows_p, rows_s, g_spec, g_spec, tab_p, tab_p, tab_s, tab_s],
        out_specs=(o_spec, o_spec),
        compiler_params=_params(1),
        name="dualnorm",
    )(x_p, x_s, g_kv.reshape(1, d), g_b.reshape(1, d), shift, scale, shift, scale)


def _finalnorm_kernel(x_ref, g_ref, o_ref):
    x = x_ref[...]
    o_ref[...] = x * _rms_scale(x) * g_ref[...]


def _finalnorm(x, g):
    m, d = x.shape
    tm = NORM_TM
    return pl.pallas_call(
        _finalnorm_kernel,
        out_shape=jax.ShapeDtypeStruct((m, d), F32),
        grid=(m // tm,),
        in_specs=[pl.BlockSpec((tm, d), lambda i: (i, 0)), pl.BlockSpec((1, d), lambda i: (0, 0))],
        out_specs=pl.BlockSpec((tm, d), lambda i: (i, 0)),
        compiler_params=_params(1),
        name="finalnorm",
    )(x, g.reshape(1, d))


def _mm_kernel(h_ref, w_ref, o_ref, wbf_ref, *, silu, scale):
    @pl.when(pl.program_id(1) == 0)
    def _():
        wbf_ref[...] = w_ref[...].astype(BF16)

    acc = _dot(h_ref[...], wbf_ref[...])
    if silu:
        acc = _silu(acc)
    if scale is not None:
        acc = acc * scale
    o_ref[...] = acc.astype(o_ref.dtype)


def _mm(h, w, col0, n_out, out_dtype, *, silu=False, scale=None, row0=0, rows=None):
    k = h.shape[1]
    rows = h.shape[0] if rows is None else rows
    tm, tn = MM_TM, MM_TN
    i0, j0 = row0 // tm, col0 // tn
    return pl.pallas_call(
        functools.partial(_mm_kernel, silu=silu, scale=scale),
        out_shape=jax.ShapeDtypeStruct((rows, n_out), out_dtype),
        grid=(n_out // tn, rows // tm),
        in_specs=[pl.BlockSpec((tm, k), lambda j, i: (i0 + i, 0)),
                  pl.BlockSpec((k, tn), lambda j, i: (0, j0 + j))],
        out_specs=pl.BlockSpec((tm, tn), lambda j, i: (i, j)),
        scratch_shapes=[pltpu.VMEM((k, tn), BF16)],
        compiler_params=_params(2),
        name="proj",
    )(h, w)


def _mm_resid_kernel(u_ref, w_ref, x_ref, gate_ref, o_ref, wbf_ref):
    @pl.when(pl.program_id(1) == 0)
    def _():
        wbf_ref[...] = w_ref[...].astype(BF16)

    acc = _dot(u_ref[...], wbf_ref[...])
    nb = gate_ref.shape[0]
    tm, tn = acc.shape
    y = acc.reshape(nb, tm // nb, tn) * gate_ref[...]
    o_ref[...] = x_ref[...] + y.reshape(tm, tn)


def _mm_resid(u, w, x, gate, rows_per_batch, table_row0):
    m, k = u.shape
    n = w.shape[1]
    tm, tn = MM_TM, MM_TN
    if rows_per_batch >= tm:
        nb = 1
        tiles_per_batch = rows_per_batch // tm
        gate_map = lambda j, i: (table_row0 + i // tiles_per_batch, 0, j)
    else:
        nb = tm // rows_per_batch
        blk0 = table_row0 // nb
        gate_map = lambda j, i: (blk0 + i, 0, j)
    return pl.pallas_call(
        _mm_resid_kernel,
        out_shape=jax.ShapeDtypeStruct((m, n), F32),
        grid=(n // tn, m // tm),
        in_specs=[pl.BlockSpec((tm, k), lambda j, i: (i, 0)),
                  pl.BlockSpec((k, tn), lambda j, i: (0, j)),
                  pl.BlockSpec((tm, tn), lambda j, i: (i, j)),
                  pl.BlockSpec((nb, 1, tn), gate_map)],
        out_specs=pl.BlockSpec((tm, tn), lambda j, i: (i, j)),
        scratch_shapes=[pltpu.VMEM((k, tn), BF16)],
        compiler_params=_params(2),
        name="out_proj",
    )(u, w, x, gate)


def _softplus(z):
    return jnp.maximum(z, 0.0) + jnp.log(1.0 + jnp.exp(-jnp.abs(z)))


def _suffix_sum(x, tri):
    hi = x.astype(BF16)
    lo = (x - hi.astype(F32)).astype(BF16)
    return _dot(hi, tri) + _dot(lo, tri)


def _stick_block(q, kblk, vblk, tri, carry, acc, causal):
    z = _dot_nt(q, kblk)
    sp = _softplus(z)
    if causal is not None:
        sp = jnp.where(causal, sp, 0.0)
    incl = _suffix_sum(sp, tri)
    w = jnp.exp(z - incl + carry)
    if causal is not None:
        w = jnp.where(causal, w, 0.0)
    acc = acc + _dot(w.astype(BF16), vblk)
    carry = carry - incl[:, :1]
    return carry, acc


def _strict_lower(tq, tk):
    return lax.broadcasted_iota(jnp.int32, (tq, tk), 1) < lax.broadcasted_iota(jnp.int32, (tq, tk), 0)


def _all_weights_vanish(carry):
    return jnp.max(carry) <= F32_EXP_UNDERFLOW_LOG


def _stick_prompt_kernel(q_ref, k_ref, v_ref, sg_ref, tri_ref, o_ref, kbf_ref, vbf_ref, *, tk):
    qi = pl.program_id(2)
    tq = q_ref.shape[1]
    n_sub = tq // tk

    @pl.when(qi == 0)
    def _():
        kbf_ref[...] = k_ref[0].astype(BF16)
        vbf_ref[...] = v_ref[0].astype(BF16)

    heads = range(q_ref.shape[2] // HEAD_DIM)
    lanes = [slice(hh * HEAD_DIM, (hh + 1) * HEAD_DIM) for hh in heads]
    qs = [q_ref[0, :, lanes[hh]] for hh in heads]
    tri = tri_ref[...]

    def kv(k0, hh):
        k0 = pl.multiple_of(k0, tk)
        return kbf_ref[pl.ds(k0, tk), lanes[hh]], vbf_ref[pl.ds(k0, tk), lanes[hh]]

    carries = [jnp.zeros((tq, 1), F32) for _ in heads]
    accs = [jnp.zeros((tq, HEAD_DIM), F32) for _ in heads]
    for sb in reversed(range(n_sub)):
        r0 = sb * tk
        causal = _strict_lower(tq - r0, tk)
        for hh in heads:
            kblk, vblk = kv(qi * tq + r0, hh)
            c_sub, a_sub = _stick_block(qs[hh][r0:], kblk, vblk, tri, carries[hh][r0:], accs[hh][r0:], causal)
            carries[hh] = jnp.concatenate([carries[hh][:r0], c_sub], axis=0) if r0 else c_sub
            accs[hh] = jnp.concatenate([accs[hh][:r0], a_sub], axis=0) if r0 else a_sub

    def cond(state):
        it, carries, _ = state
        return jnp.logical_and(
            it < qi * n_sub, jnp.logical_not(_all_weights_vanish(functools.reduce(jnp.maximum, carries))))

    def body(state):
        it, carries, accs = state
        new = [_stick_block(qs[hh], *kv((qi * n_sub - 1 - it) * tk, hh), tri, carries[hh], accs[hh], None)
               for hh in heads]
        return it + 1, tuple(s[0] for s in new), tuple(s[1] for s in new)

    _, carries, accs = lax.while_loop(cond, body, (jnp.int32(0), tuple(carries), tuple(accs)))
    for hh in heads:
        o_ref[0, :, lanes[hh]] = (accs[hh] * sg_ref[0, :, lanes[hh]].astype(F32)).astype(o_ref.dtype)


def _tri_incl(t):
    return (jnp.arange(t)[:, None] >= jnp.arange(t)[None, :]).astype(BF16)


def _stick_prompt(q, k, v, sg, batch, seq):
    d = k.shape[1]
    n_heads = d // HEAD_DIM
    tq, tk = STICK_TQ, ATTN_T
    tiles = seq // tq
    hw = STICK_HEADS * HEAD_DIM
    flat = lambda a: a.reshape(1, a.shape[0], d)
    blk_in = pl.BlockSpec((1, tq, hw), lambda b, h, i: (0, b * tiles + i, h))
    kv = pl.BlockSpec((1, seq, hw), lambda b, h, i: (0, b, h))
    out = pl.pallas_call(
        functools.partial(_stick_prompt_kernel, tk=tk),
        out_shape=jax.ShapeDtypeStruct((batch, seq, d), BF16),
        grid=(batch, d // hw, tiles),
        in_specs=[blk_in, kv, kv, blk_in, pl.BlockSpec((tk, tk), lambda b, h, i: (0, 0))],
        out_specs=pl.BlockSpec((1, tq, hw), lambda b, h, i: (b, i, h)),
        scratch_shapes=[pltpu.VMEM((seq, hw), BF16), pltpu.VMEM((seq, hw), BF16)],
        compiler_params=_params(3),
        name="stick_prompt",
    )(flat(q), flat(k), flat(v), flat(sg), _tri_incl(tk))
    return out.reshape(batch * seq, d)


def _stick_sample_kernel(q_ref, ck_ref, cv_ref, k_ref, v_ref, sg_ref, tri_ref, o_ref, kst_ref, vst_ref, *, kblock):
    t_new = q_ref.shape[1]
    past = ck_ref.shape[1]
    heads = range(ck_ref.shape[2])
    lanes = [slice(hh * HEAD_DIM, (hh + 1) * HEAD_DIM) for hh in heads]
    tri = tri_ref[...]
    tri_new = tri_ref[:t_new, :t_new]
    causal = _strict_lower(t_new, t_new)
    qs = [q_ref[0, :, lanes[hh]] for hh in heads]

    state = [_stick_block(qs[hh], k_ref[0, :, lanes[hh]].astype(BF16), v_ref[0, :, lanes[hh]].astype(BF16),
                          tri_new, jnp.zeros((t_new, 1), F32), jnp.zeros((t_new, HEAD_DIM), F32), causal)
             for hh in heads]
    carries = tuple(s[0] for s in state)
    accs = tuple(s[1] for s in state)

    def cond(state):
        it, carries, _ = state
        return jnp.logical_and(it < past // kblock,
                               jnp.logical_not(_all_weights_vanish(functools.reduce(jnp.maximum, carries))))

    def body(state):
        it, carries, accs = state
        k0 = pl.multiple_of(past - (it + 1) * kblock, kblock)
        for hh in heads:
            kst_ref[hh] = ck_ref[0, pl.ds(k0, kblock), hh, :]
            vst_ref[hh] = cv_ref[0, pl.ds(k0, kblock), hh, :]
        new = [_stick_block(qs[hh], kst_ref[hh].astype(BF16), vst_ref[hh].astype(BF16), tri,
                            carries[hh], accs[hh], None) for hh in heads]
        return it + 1, tuple(s[0] for s in new), tuple(s[1] for s in new)

    _, carries, accs = lax.while_loop(cond, body, (jnp.int32(0), carries, accs))
    for hh in heads:
        o_ref[0, :, lanes[hh]] = (accs[hh] * sg_ref[0, :, lanes[hh]].astype(F32)).astype(o_ref.dtype)


def _stick_sample(q, cache_k, cache_v, k, v, sg, row0):
    b, past, n_heads, _ = cache_k.shape
    d = n_heads * HEAD_DIM
    t_new = k.shape[0] // b
    hw = SAMPLE_HEADS * HEAD_DIM
    t = ATTN_T
    blk0 = row0 // t_new
    q3 = q.reshape(q.shape[0] // t_new, t_new, d)
    sg3 = sg.reshape(q3.shape)
    merged = pl.BlockSpec((1, t_new, hw), lambda bi, g: (blk0 + bi, 0, g))
    new = pl.BlockSpec((1, t_new, hw), lambda bi, g: (bi, 0, g))
    cache = pl.BlockSpec((1, past, SAMPLE_HEADS, HEAD_DIM), lambda bi, g: (bi, 0, g, 0))
    out = pl.pallas_call(
        functools.partial(_stick_sample_kernel, kblock=t),
        out_shape=jax.ShapeDtypeStruct((b, t_new, d), BF16),
        grid=(b, d // hw),
        in_specs=[merged, cache, cache, new, new, merged, pl.BlockSpec((t, t), lambda bi, g: (0, 0))],
        out_specs=new,
        scratch_shapes=[pltpu.VMEM((SAMPLE_HEADS, t, HEAD_DIM), F32)] * 2,
        compiler_params=_params(2),
        name="stick_sample",
    )(q3, cache_k, cache_v, k.reshape(b, t_new, d), v.reshape(b, t_new, d), sg3, _tri_incl(t))
    return out.reshape(b * t_new, d)


def _band_prompt_kernel(q_ref, k_ref, v_ref, sg_ref, bias_ref, o_ref, kbf_ref, vbf_ref):
    qi = pl.program_id(2)
    t = q_ref.shape[1]
    n_kb = bias_ref.shape[2] // t

    @pl.when(qi == 0)
    def _():
        kbf_ref[...] = k_ref[0].astype(BF16)
        vbf_ref[...] = v_ref[0].astype(BF16)

    for hh in range(q_ref.shape[2] // HEAD_DIM):
        lanes = slice(hh * HEAD_DIM, (hh + 1) * HEAD_DIM)
        q = q_ref[0, :, lanes]
        scores, vblks = [], []
        for dd in range(n_kb):
            kb = qi - (n_kb - 1) + dd
            k0 = pl.multiple_of(jnp.maximum(kb, 0) * t, t)
            s = _dot_nt(q, kbf_ref[pl.ds(k0, t), lanes]) + bias_ref[hh, :, dd * t:(dd + 1) * t]
            scores.append(jnp.where(kb >= 0, s, NEG_INF))
            vblks.append(vbf_ref[pl.ds(k0, t), lanes])
        m = scores[0].max(axis=-1, keepdims=True)
        for s in scores[1:]:
            m = jnp.maximum(m, s.max(axis=-1, keepdims=True))
        acc = jnp.zeros((t, HEAD_DIM), F32)
        denom = jnp.zeros((t, 1), F32)
        for s, vblk in zip(scores, vblks):
            p = jnp.exp(s - m)
            denom = denom + p.sum(axis=-1, keepdims=True)
            acc = acc + _dot(p.astype(BF16), vblk)
        o_ref[0, :, lanes] = (acc / denom * sg_ref[0, :, lanes].astype(F32)).astype(o_ref.dtype)


def _toeplitz(values_at, rows, cols):
    period = rows + cols
    j = jnp.arange(period)
    v = values_at(jnp.where(j < cols, j, j - period))
    flat = jnp.tile(v, (1,) * (v.ndim - 1) + (rows,))[..., :rows * (period - 1)]
    return flat.reshape(v.shape[:-1] + (rows, period - 1))[..., :cols]


def _band_bias_prompt(rel_bias, t):
    n_kb = (LEFT_CHUNKS * CHUNK) // t + 1
    back = (n_kb - 1) * t
    rb = rel_bias.astype(F32)
    bias = _toeplitz(lambda m: rb[:, jnp.clip(back - m, -REL_CLIP, REL_CLIP) + REL_CLIP], t, n_kb * t)
    r = jnp.arange(t)[:, None]
    c = jnp.arange(n_kb * t)[None, :]
    chunk_diff = (back + r) // CHUNK - c // CHUNK
    visible = (chunk_diff >= 0) & (chunk_diff <= LEFT_CHUNKS)
    return jnp.where(visible[None], bias, NEG_INF)


def _band_prompt(q, k, v, sg, rel_bias, batch, seq):
    d = k.shape[1]
    t = ATTN_T
    tiles = seq // t
    hw = BAND_HEADS * HEAD_DIM
    bias = _band_bias_prompt(rel_bias, t)
    blk_in = pl.BlockSpec((1, t, hw), lambda b, h, i: (0, b * tiles + i, h))
    kv = pl.BlockSpec((1, seq, hw), lambda b, h, i: (0, b, h))
    flat = lambda a: a.reshape(1, a.shape[0], d)
    out = pl.pallas_call(
        _band_prompt_kernel,
        out_shape=jax.ShapeDtypeStruct((batch, seq, d), BF16),
        grid=(batch, d // hw, tiles),
        in_specs=[blk_in, kv, kv, blk_in,
                  pl.BlockSpec((BAND_HEADS,) + bias.shape[1:], lambda b, h, i: (h, 0, 0))],
        out_specs=pl.BlockSpec((1, t, hw), lambda b, h, i: (b, i, h)),
        scratch_shapes=[pltpu.VMEM((seq, hw), BF16), pltpu.VMEM((seq, hw), BF16)],
        compiler_params=_params(3),
        name="band_prompt",
    )(flat(q), flat(k), flat(v), flat(sg), bias)
    return out.reshape(batch * seq, d)


def _band_sample_kernel(q_ref, ck_ref, cv_ref, k_ref, v_ref, sg_ref, bias_ref, o_ref, kst_ref, vst_ref):
    past = ck_ref.shape[1]
    for hh in range(ck_ref.shape[2]):
        kst_ref[hh] = ck_ref[0, :, hh, :]
        vst_ref[hh] = cv_ref[0, :, hh, :]
    for hh in range(ck_ref.shape[2]):
        lanes = slice(hh * HEAD_DIM, (hh + 1) * HEAD_DIM)
        q = q_ref[0, :, lanes]
        s_old = _dot_nt(q, kst_ref[hh].astype(BF16)) + bias_ref[hh, :, :past]
        s_new = _dot_nt(q, k_ref[0, :, lanes].astype(BF16)) + bias_ref[hh, :, past:]
        m = jnp.maximum(s_old.max(axis=-1, keepdims=True), s_new.max(axis=-1, keepdims=True))
        p_old = jnp.exp(s_old - m)
        p_new = jnp.exp(s_new - m)
        denom = p_old.sum(axis=-1, keepdims=True) + p_new.sum(axis=-1, keepdims=True)
        acc = (_dot(p_old.astype(BF16), vst_ref[hh].astype(BF16))
               + _dot(p_new.astype(BF16), v_ref[0, :, lanes].astype(BF16)))
        o_ref[0, :, lanes] = (acc / denom * sg_ref[0, :, lanes].astype(F32)).astype(o_ref.dtype)


def _band_bias_sample(rel_bias, past_total, past_b, t_new):
    rb = rel_bias.astype(F32)
    bias = _toeplitz(lambda m: rb[:, jnp.clip(past_b - m, -REL_CLIP, REL_CLIP) + REL_CLIP],
                     t_new, past_b + t_new)
    q_pos = past_total + jnp.arange(t_new)
    k_pos = past_total - past_b + jnp.arange(past_b + t_new)
    q_chunk = q_pos // CHUNK
    k_chunk = k_pos // CHUNK
    visible = ((k_pos[None, :] >= 0) & (k_chunk[None, :] <= q_chunk[:, None])
               & (k_chunk[None, :] >= q_chunk[:, None] - LEFT_CHUNKS))
    return jnp.where(visible[None], bias, NEG_INF)


def _band_sample(q, cache_k, cache_v, k, v, sg, rel_bias, row0, past_total):
    b, past_b, n_heads, _ = cache_k.shape
    d = n_heads * HEAD_DIM
    t_new = (q.shape[0] - row0) // b
    hw = SAMPLE_HEADS * HEAD_DIM
    blk0 = row0 // t_new
    bias = _band_bias_sample(rel_bias, past_total, past_b, t_new)
    r3 = lambda a: a.reshape(a.shape[0] // t_new, t_new, d)
    merged = pl.BlockSpec((1, t_new, hw), lambda bi, g: (blk0 + bi, 0, g))
    cache = pl.BlockSpec((1, past_b, SAMPLE_HEADS, HEAD_DIM), lambda bi, g: (bi, 0, g, 0))
    out = pl.pallas_call(
        _band_sample_kernel,
        out_shape=jax.ShapeDtypeStruct((b, t_new, d), BF16),
        grid=(b, d // hw),
        in_specs=[merged, cache, cache, merged, merged, merged,
                  pl.BlockSpec((SAMPLE_HEADS,) + bias.shape[1:], lambda bi, g: (g, 0, 0))],
        out_specs=pl.BlockSpec((1, t_new, hw), lambda bi, g: (bi, 0, g)),
        scratch_shapes=[pltpu.VMEM((SAMPLE_HEADS, past_b, HEAD_DIM), F32)] * 2,
        compiler_params=_params(2),
        name="band_sample",
    )(r3(q), cache_k, cache_v, r3(k), r3(v), r3(sg), bias)
    return out.reshape(b * t_new, d)


def kernel(x_prompt, x_sample, c_prompt, c_sample, cache_a_k, cache_a_v, cache_b_k, cache_b_v, w_mod_a, b_mod_a, g_norm_a, w_in_a, w_out_a, g_kv, w_kv, w_mod_b, b_mod_b, g_norm_b, w_in_b, rel_bias_b, w_out_b, g_final):
    bp, seq, d = x_prompt.shape
    bs, t_new, _ = x_sample.shape
    n_heads = d // HEAD_DIM
    past = cache_a_k.shape[2]
    past_b = cache_b_k.shape[1]
    mp, ms = bp * seq, bs * t_new
    assert w_mod_a.shape[0] == 1 and w_mod_b.shape[0] == 1, "one layer of each mixer"
    assert seq % MM_TM == 0 and mp % MM_TM == 0 and ms % MM_TM == 0 and MM_TM % t_new == 0
    assert bs % (MM_TM // t_new) == 0 and bs % (NORM_TM // t_new) == 0
    assert seq % STICK_TQ == 0 and STICK_TQ % ATTN_T == 0
    assert seq % ATTN_T == 0 and past % ATTN_T == 0 and (LEFT_CHUNKS * CHUNK) % ATTN_T == 0

    xp = x_prompt.reshape(mp, d)
    xs = x_sample.reshape(ms, d)

    n_c = bs + bp
    pad = -n_c % BF16_SUBLANES
    c_all = jnp.concatenate([c_sample, c_prompt, jnp.zeros((pad, d), F32)], axis=0)
    mod_a, mod_b = _adaln_tables(c_all, w_mod_a[0], b_mod_a[0], w_mod_b[0], b_mod_b[0])
    shift_a, scale_a, gate_a = _split_mod(mod_a, d)
    shift_b, scale_b, gate_b = _split_mod(mod_b, d)

    h_a = _prenorm(xp, xs, g_norm_a[0], shift_a, scale_a, seq, t_new)
    w_in = w_in_a[0]
    q_scale = 1.0 / math.sqrt(HEAD_DIM)
    q_a = _mm(h_a, w_in, 0, d, BF16, scale=q_scale)
    k_p = _mm(h_a, w_in, d, d, F32, row0=0, rows=mp)
    k_s = _mm(h_a, w_in, d, d, F32, row0=mp, rows=ms)
    v_p = _mm(h_a, w_in, 2 * d, d, F32, row0=0, rows=mp)
    v_s = _mm(h_a, w_in, 2 * d, d, F32, row0=mp, rows=ms)
    sg_a = _mm(h_a, w_in, 3 * d, d, BF16, silu=True)

    u_p = _stick_prompt(q_a, k_p, v_p, sg_a, bp, seq)
    u_s = _stick_sample(q_a, cache_a_k[0], cache_a_v[0], k_s, v_s, sg_a, mp)
    x1_p = _mm_resid(u_p, w_out_a[0], xp, gate_a, seq, bs)
    x1_s = _mm_resid(u_s, w_out_a[0], xs, gate_a, t_new, 0)

    h_kv, h_b = _dualnorm(x1_p, x1_s, g_kv, g_norm_b[0], shift_b, scale_b, seq, t_new)
    kb = _mm(h_kv, w_kv, 0, d, F32)
    vb = _mm(h_kv, w_kv, d, d, F32)
    q_b = _mm(h_b, w_in_b[0], 0, d, BF16, scale=q_scale)
    sg_b = _mm(h_b, w_in_b[0], d, d, BF16, silu=True)

    ub_p = _band_prompt(q_b, kb, vb, sg_b, rel_bias_b[0], bp, seq)
    ub_s = _band_sample(q_b, cache_b_k, cache_b_v, kb, vb, sg_b, rel_bias_b[0], mp, past)
    x2_p = _mm_resid(ub_p, w_out_b[0], x1_p, gate_b, seq, bs)
    x2_s = _mm_resid(ub_s, w_out_b[0], x1_s, gate_b, t_new, 0)

    y_p = _finalnorm(x2_p, g_final).reshape(bp, seq, d)
    y_s = _finalnorm(x2_s, g_final).reshape(bs, t_new, d)

    keep_b = min(LEFT_CHUNKS * CHUNK, seq)
    heads = lambda a, b, t: a.reshape(b, t, n_heads, HEAD_DIM)
    kb_p, vb_p = heads(kb[:mp], bp, seq), heads(vb[:mp], bp, seq)
    return (y_p, y_s,
            heads(k_p, bp, seq)[None], heads(v_p, bp, seq)[None],
            heads(k_s, bs, t_new)[None], heads(v_s, bs, t_new)[None],
            kb_p[:, seq - keep_b:], vb_p[:, seq - keep_b:],
            heads(kb[mp:], bs, t_new), heads(vb[mp:], bs, t_new))
```

```python
import functools
import math

import jax
import jax.numpy as jnp
from jax import lax
from jax.experimental import pallas as pl
from jax.experimental.pallas import tpu as pltpu

F32 = jnp.float32
BF16 = jnp.bfloat16

HEAD_DIM = 128
CHUNK = 64
LEFT_CHUNKS = 8
REL_CLIP = 128
EPS = 1e-6
NEG_INF = -1e30

V7X_VMEM_LIMIT_BYTES = 52 * 1024 * 1024
LANES = 128
BF16_SUBLANES = 16

MM_TM = 512
RESID_TM = 512
MM_TN = 1024
NORM_TM = 256
MOD_TN = 512
ATTN_T = 256
STICK_TQ = 512
STICK_HEADS = 2
BAND_HEADS = 2

F32_EXP_UNDERFLOW_LOG = -104.0
SAMPLE_HEADS = 8


def _params(n_grid):
    return pltpu.CompilerParams(
        dimension_semantics=("arbitrary",) * n_grid,
        vmem_limit_bytes=V7X_VMEM_LIMIT_BYTES)


def _silu(x):
    return x / (1.0 + jnp.exp(-x))


def _dot(a, b):
    return jnp.dot(a, b, preferred_element_type=F32)


def _dot_nt(a, b):
    return lax.dot_general(a, b, (((1,), (1,)), ((), ())), preferred_element_type=F32)


def _mod_kernel(c_ref, wa_ref, ba_ref, wb_ref, bb_ref, oa_ref, ob_ref):
    a = _silu(c_ref[...]).astype(BF16)
    oa_ref[...] = _dot(a, wa_ref[...].astype(BF16)) + ba_ref[...]
    ob_ref[...] = _dot(a, wb_ref[...].astype(BF16)) + bb_ref[...]


def _adaln_tables(c_all, w_a, b_a, w_b, b_b):
    rows, d = c_all.shape
    n = w_a.shape[1]
    w_spec = pl.BlockSpec((d, MOD_TN), lambda j: (0, j))
    b_spec = pl.BlockSpec((1, MOD_TN), lambda j: (0, j))
    o_spec = pl.BlockSpec((rows, MOD_TN), lambda j: (0, j))
    return pl.pallas_call(
        _mod_kernel,
        out_shape=(jax.ShapeDtypeStruct((rows, n), F32),) * 2,
        grid=(n // MOD_TN,),
        in_specs=[pl.BlockSpec((rows, d), lambda j: (0, 0)), w_spec, b_spec, w_spec, b_spec],
        out_specs=(o_spec, o_spec),
        compiler_params=_params(1),
        name="adaln_mod",
    )(c_all, w_a, b_a.reshape(1, n), w_b, b_b.reshape(1, n))


def _split_mod(mod, d):
    rows = mod.shape[0]
    return tuple(mod[:, k * d:(k + 1) * d].reshape(rows, 1, d) for k in range(3))


def _rms_scale(x):
    return lax.rsqrt(jnp.mean(x * x, axis=-1, keepdims=True) + EPS)


def _modulate(y, shift_ref, scale_ref):
    nb = shift_ref.shape[0]
    tm, d = y.shape
    y3 = y.reshape(nb, tm // nb, d)
    return (y3 * (1.0 + scale_ref[...]) + shift_ref[...]).reshape(tm, d)


def _prenorm_kernel(xp_ref, xs_ref, g_ref, shp_ref, scp_ref, shs_ref, scs_ref, o_ref, *, n_prompt_tiles):
    i = pl.program_id(0)

    def run(x_ref, shift_ref, scale_ref):
        x = x_ref[...]
        y = x * _rms_scale(x) * g_ref[...]
        o_ref[...] = _modulate(y, shift_ref, scale_ref).astype(o_ref.dtype)

    @pl.when(i < n_prompt_tiles)
    def _():
        run(xp_ref, shp_ref, scp_ref)

    @pl.when(i >= n_prompt_tiles)
    def _():
        run(xs_ref, shs_ref, scs_ref)


def _group_specs(tm, d, n_prompt_tiles, prompt_rpb, sample_rpb, n_sample_batches):
    tiles_per_prompt_batch = prompt_rpb // tm
    nb_s = tm // sample_rpb
    rows_p = pl.BlockSpec((tm, d), lambda i: (jnp.minimum(i, n_prompt_tiles - 1), 0))
    rows_s = pl.BlockSpec((tm, d), lambda i: (jnp.maximum(i - n_prompt_tiles, 0), 0))
    tab_p = pl.BlockSpec(
        (1, 1, d),
        lambda i: (n_sample_batches + jnp.minimum(i, n_prompt_tiles - 1) // tiles_per_prompt_batch, 0, 0))
    tab_s = pl.BlockSpec((nb_s, 1, d), lambda i: (jnp.maximum(i - n_prompt_tiles, 0), 0, 0))
    return rows_p, rows_s, tab_p, tab_s


def _prenorm(x_p, x_s, g, shift, scale, prompt_rpb, sample_rpb):
    mp, d = x_p.shape
    ms = x_s.shape[0]
    tm = NORM_TM
    npt = mp // tm
    rows_p, rows_s, tab_p, tab_s = _group_specs(tm, d, npt, prompt_rpb, sample_rpb, ms // sample_rpb)
    return pl.pallas_call(
        functools.partial(_prenorm_kernel, n_prompt_tiles=npt),
        out_shape=jax.ShapeDtypeStruct((mp + ms, d), BF16),
        grid=((mp + ms) // tm,),
        in_specs=[rows_p, rows_s, pl.BlockSpec((1, d), lambda i: (0, 0)), tab_p, tab_p, tab_s, tab_s],
        out_specs=pl.BlockSpec((tm, d), lambda i: (i, 0)),
        compiler_params=_params(1),
        name="prenorm",
    )(x_p, x_s, g.reshape(1, d), shift, scale, shift, scale)


def _dualnorm_kernel(xp_ref, xs_ref, gkv_ref, gb_ref, shp_ref, scp_ref, shs_ref, scs_ref,
                     okv_ref, ob_ref, *, n_prompt_tiles):
    i = pl.program_id(0)

    def run(x_ref, shift_ref, scale_ref):
        x = x_ref[...]
        y = x * _rms_scale(x)
        okv_ref[...] = (y * gkv_ref[...]).astype(okv_ref.dtype)
        ob_ref[...] = _modulate(y * gb_ref[...], shift_ref, scale_ref).astype(ob_ref.dtype)

    @pl.when(i < n_prompt_tiles)
    def _():
        run(xp_ref, shp_ref, scp_ref)

    @pl.when(i >= n_prompt_tiles)
    def _():
        run(xs_ref, shs_ref, scs_ref)


def _dualnorm(x_p, x_s, g_kv, g_b, shift, scale, prompt_rpb, sample_rpb):
    mp, d = x_p.shape
    ms = x_s.shape[0]
    tm = NORM_TM
    npt = mp // tm
    rows_p, rows_s, tab_p, tab_s = _group_specs(tm, d, npt, prompt_rpb, sample_rpb, ms // sample_rpb)
    g_spec = pl.BlockSpec((1, d), lambda i: (0, 0))
    o_spec = pl.BlockSpec((tm, d), lambda i: (i, 0))
    return pl.pallas_call(
        functools.partial(_dualnorm_kernel, n_prompt_tiles=npt),
        out_shape=(jax.ShapeDtypeStruct((mp + ms, d), BF16),) * 2,
        grid=((mp + ms) // tm,),
        in_specs=[rows_p, rows_s, g_spec, g_spec, tab_p, tab_p, tab_s, tab_s],
        out_specs=(o_spec, o_spec),
        compiler_params=_params(1),
        name="dualnorm",
    )(x_p, x_s, g_kv.reshape(1, d), g_b.reshape(1, d), shift, scale, shift, scale)


def _finalnorm_kernel(x_ref, g_ref, o_ref):
    x = x_ref[...]
    o_ref[...] = x * _rms_scale(x) * g_ref[...]


def _finalnorm(x, g):
    m, d = x.shape
    tm = NORM_TM
    return pl.pallas_call(
        _finalnorm_kernel,
        out_shape=jax.ShapeDtypeStruct((m, d), F32),
        grid=(m // tm,),
        in_specs=[pl.BlockSpec((tm, d), lambda i: (i, 0)), pl.BlockSpec((1, d), lambda i: (0, 0))],
        out_specs=pl.BlockSpec((tm, d), lambda i: (i, 0)),
        compiler_params=_params(1),
        name="finalnorm",
    )(x, g.reshape(1, d))


def _stage_weight_tile(w_hbm, stage_ref, wbf_ref, sem, col0):
    j, nj = pl.program_id(0), pl.num_programs(0)
    tn = wbf_ref.shape[1]

    def tile_copy(jj):
        return pltpu.make_async_copy(w_hbm.at[:, pl.ds(col0 + jj * tn, tn)], stage_ref, sem)

    @pl.when(pl.program_id(1) == 0)
    def _():
        @pl.when(j == 0)
        def _():
            tile_copy(0).start()

        tile_copy(j).wait()
        wbf_ref[...] = stage_ref[...].astype(BF16)

        @pl.when(j + 1 < nj)
        def _():
            tile_copy(j + 1).start()


def _weight_scratch(k, tn):
    return [pltpu.VMEM((k, tn), F32), pltpu.VMEM((k, tn), BF16), pltpu.SemaphoreType.DMA(())]


def _mm_kernel(h_ref, w_hbm, o_ref, stage_ref, wbf_ref, sem, *, silu, scale, col0):
    _stage_weight_tile(w_hbm, stage_ref, wbf_ref, sem, col0)
    acc = _dot(h_ref[...], wbf_ref[...])
    if silu:
        acc = _silu(acc)
    if scale is not None:
        acc = acc * scale
    o_ref[...] = acc.astype(o_ref.dtype)


def _mm(h, w, col0, n_out, out_dtype, *, silu=False, scale=None, row0=0, rows=None):
    k = h.shape[1]
    rows = h.shape[0] if rows is None else rows
    tm, tn = MM_TM, MM_TN
    i0 = row0 // tm
    return pl.pallas_call(
        functools.partial(_mm_kernel, silu=silu, scale=scale, col0=col0),
        out_shape=jax.ShapeDtypeStruct((rows, n_out), out_dtype),
        grid=(n_out // tn, rows // tm),
        in_specs=[pl.BlockSpec((tm, k), lambda j, i: (i0 + i, 0)),
                  pl.BlockSpec(memory_space=pl.ANY)],
        out_specs=pl.BlockSpec((tm, tn), lambda j, i: (i, j)),
        scratch_shapes=_weight_scratch(k, tn),
        compiler_params=_params(2),
        name="proj",
    )(h, w)


def _mm_resid_kernel(u_ref, w_hbm, x_ref, gate_ref, o_ref, stage_ref, wbf_ref, sem):
    _stage_weight_tile(w_hbm, stage_ref, wbf_ref, sem, 0)
    acc = _dot(u_ref[...], wbf_ref[...])
    nb = gate_ref.shape[0]
    tm, tn = acc.shape
    y = acc.reshape(nb, tm // nb, tn) * gate_ref[...]
    o_ref[...] = x_ref[...] + y.reshape(tm, tn)


def _mm_resid(u, w, x, gate, rows_per_batch, table_row0):
    m, k = u.shape
    n = w.shape[1]
    tm, tn = RESID_TM, MM_TN
    if rows_per_batch >= tm:
        nb = 1
        tiles_per_batch = rows_per_batch // tm
        gate_map = lambda j, i: (table_row0 + i // tiles_per_batch, 0, j)
    else:
        nb = tm // rows_per_batch
        blk0 = table_row0 // nb
        gate_map = lambda j, i: (blk0 + i, 0, j)
    return pl.pallas_call(
        _mm_resid_kernel,
        out_shape=jax.ShapeDtypeStruct((m, n), F32),
        grid=(n // tn, m // tm),
        in_specs=[pl.BlockSpec((tm, k), lambda j, i: (i, 0)),
                  pl.BlockSpec(memory_space=pl.ANY),
                  pl.BlockSpec((tm, tn), lambda j, i: (i, j)),
                  pl.BlockSpec((nb, 1, tn), gate_map)],
        out_specs=pl.BlockSpec((tm, tn), lambda j, i: (i, j)),
        scratch_shapes=_weight_scratch(k, tn),
        compiler_params=_params(2),
        name="out_proj",
    )(u, w, x, gate)


def _softplus(z):
    return jnp.maximum(z, 0.0) + jnp.log(1.0 + jnp.exp(-jnp.abs(z)))


def _suffix_sum(x, tri):
    hi = x.astype(BF16)
    lo = (x - hi.astype(F32)).astype(BF16)
    return _dot(hi, tri) + _dot(lo, tri)


def _stick_blocks(qs, kblks, vblks, tri, carries, accs, causal):
    zs = [_dot_nt(q, kblk) for q, kblk in zip(qs, kblks)]
    sps = [_softplus(z) if causal is None else jnp.where(causal, _softplus(z), 0.0) for z in zs]
    incls = [_suffix_sum(sp, tri) for sp in sps]
    ws = [jnp.exp(z - incl + carry) for z, incl, carry in zip(zs, incls, carries)]
    if causal is not None:
        ws = [jnp.where(causal, w, 0.0) for w in ws]
    accs = [acc + _dot(w.astype(BF16), vblk) for acc, w, vblk in zip(accs, ws, vblks)]
    carries = [carry - incl[:, :1] for carry, incl in zip(carries, incls)]
    return carries, accs


def _strict_lower(tq, tk):
    return lax.broadcasted_iota(jnp.int32, (tq, tk), 1) < lax.broadcasted_iota(jnp.int32, (tq, tk), 0)


def _all_weights_vanish(carry):
    return jnp.max(carry) <= F32_EXP_UNDERFLOW_LOG


def _stick_prompt_kernel(q_ref, k_ref, v_ref, sg_ref, tri_ref, o_ref, kbf_ref, vbf_ref, *, tk):
    qi = pl.program_id(2)
    tq = q_ref.shape[1]
    n_sub = tq // tk

    @pl.when(qi == 0)
    def _():
        kbf_ref[...] = k_ref[0].astype(BF16)
        vbf_ref[...] = v_ref[0].astype(BF16)

    heads = range(q_ref.shape[2] // HEAD_DIM)
    lanes = [slice(hh * HEAD_DIM, (hh + 1) * HEAD_DIM) for hh in heads]
    qs = [q_ref[0, :, lanes[hh]] for hh in heads]
    tri = tri_ref[...]

    def kv(k0):
        k0 = pl.multiple_of(k0, tk)
        return ([kbf_ref[pl.ds(k0, tk), lanes[hh]] for hh in heads],
                [vbf_ref[pl.ds(k0, tk), lanes[hh]] for hh in heads])

    carries = [jnp.zeros((tq, 1), F32) for _ in heads]
    accs = [jnp.zeros((tq, HEAD_DIM), F32) for _ in heads]
    for sb in reversed(range(n_sub)):
        r0 = sb * tk
        kblks, vblks = kv(qi * tq + r0)
        c_sub, a_sub = _stick_blocks([q[r0:] for q in qs], kblks, vblks, tri, [c[r0:] for c in carries],
                                     [a[r0:] for a in accs], _strict_lower(tq - r0, tk))
        if r0:
            c_sub = [jnp.concatenate([c[:r0], cs], axis=0) for c, cs in zip(carries, c_sub)]
            a_sub = [jnp.concatenate([a[:r0], as_], axis=0) for a, as_ in zip(accs, a_sub)]
        carries, accs = c_sub, a_sub

    def cond(state):
        it, carries, _ = state
        return jnp.logical_and(
            it < qi * n_sub, jnp.logical_not(_all_weights_vanish(functools.reduce(jnp.maximum, carries))))

    def body(state):
        it, carries, accs = state
        kblks, vblks = kv((qi * n_sub - 1 - it) * tk)
        carries, accs = _stick_blocks(qs, kblks, vblks, tri, carries, accs, None)
        return it + 1, tuple(carries), tuple(accs)

    _, carries, accs = lax.while_loop(cond, body, (jnp.int32(0), tuple(carries), tuple(accs)))
    for hh in heads:
        o_ref[0, :, lanes[hh]] = (accs[hh] * sg_ref[0, :, lanes[hh]].astype(F32)).astype(o_ref.dtype)


def _tri_incl(t):
    return (jnp.arange(t)[:, None] >= jnp.arange(t)[None, :]).astype(BF16)


def _stick_prompt(q, k, v, sg, batch, seq):
    d = k.shape[1]
    n_heads = d // HEAD_DIM
    tq, tk = STICK_TQ, ATTN_T
    tiles = seq // tq
    hw = STICK_HEADS * HEAD_DIM
    flat = lambda a: a.reshape(1, a.shape[0], d)
    blk_in = pl.BlockSpec((1, tq, hw), lambda b, h, i: (0, b * tiles + i, h))
    kv = pl.BlockSpec((1, seq, hw), lambda b, h, i: (0, b, h))
    out = pl.pallas_call(
        functools.partial(_stick_prompt_kernel, tk=tk),
        out_shape=jax.ShapeDtypeStruct((batch, seq, d), BF16),
        grid=(batch, d // hw, tiles),
        in_specs=[blk_in, kv, kv, blk_in, pl.BlockSpec((tk, tk), lambda b, h, i: (0, 0))],
        out_specs=pl.BlockSpec((1, tq, hw), lambda b, h, i: (b, i, h)),
        scratch_shapes=[pltpu.VMEM((seq, hw), BF16), pltpu.VMEM((seq, hw), BF16)],
        compiler_params=_params(3),
        name="stick_prompt",
    )(flat(q), flat(k), flat(v), flat(sg), _tri_incl(tk))
    return out.reshape(batch * seq, d)


def _head_rows(cache_ref, hh, t0, n):
    nh, hd = cache_ref.shape[-2:]
    flat = cache_ref.reshape(math.prod(cache_ref.shape[:-1]), hd)
    return flat[pl.ds(t0 * nh + hh, n, stride=nh), :]


def _stick_sample_kernel(q_ref, ck_hbm, cv_hbm, k_ref, v_ref, sg_ref, tri_ref, o_ref, kbuf, vbuf, sem):
    bi, g = pl.program_id(0), pl.program_id(1)
    t_new = q_ref.shape[1]
    past = ck_hbm.shape[1]
    _, kblock, nh, _ = kbuf.shape
    n_blk = past // kblock
    heads = range(nh)
    lanes = [slice(hh * HEAD_DIM, (hh + 1) * HEAD_DIM) for hh in heads]
    tri = tri_ref[...]
    qs = [q_ref[0, :, lanes[hh]] for hh in heads]

    def block_copies(blk, slot):
        rows = pl.ds(past - (blk + 1) * kblock, kblock)
        group = pl.ds(g * nh, nh)
        return (pltpu.make_async_copy(ck_hbm.at[bi, rows, group, :], kbuf.at[slot], sem.at[0, slot]),
                pltpu.make_async_copy(cv_hbm.at[bi, rows, group, :], vbuf.at[slot], sem.at[1, slot]))

    for c in block_copies(0, 0):
        c.start()

    carries, accs = _stick_blocks(
        qs, [k_ref[0, :, lanes[hh]].astype(BF16) for hh in heads],
        [v_ref[0, :, lanes[hh]].astype(BF16) for hh in heads], tri_ref[:t_new, :t_new],
        [jnp.zeros((t_new, 1), F32) for _ in heads], [jnp.zeros((t_new, HEAD_DIM), F32) for _ in heads],
        _strict_lower(t_new, t_new))

    def cond(state):
        it, carries, _ = state
        return jnp.logical_and(it < n_blk,
                               jnp.logical_not(_all_weights_vanish(functools.reduce(jnp.maximum, carries))))

    def body(state):
        it, carries, accs = state
        slot = it % 2
        for c in block_copies(it, slot):
            c.wait()

        @pl.when(it + 1 < n_blk)
        def _():
            for c in block_copies(it + 1, 1 - slot):
                c.start()

        t0 = slot * kblock
        carries, accs = _stick_blocks(
            qs, [_head_rows(kbuf, hh, t0, kblock).astype(BF16) for hh in heads],
            [_head_rows(vbuf, hh, t0, kblock).astype(BF16) for hh in heads], tri, carries, accs, None)
        return it + 1, tuple(carries), tuple(accs)

    done, carries, accs = lax.while_loop(cond, body, (jnp.int32(0), tuple(carries), tuple(accs)))

    @pl.when(done < n_blk)
    def _():
        for c in block_copies(done, done % 2):
            c.wait()

    for hh in heads:
        o_ref[0, :, lanes[hh]] = (accs[hh] * sg_ref[0, :, lanes[hh]].astype(F32)).astype(o_ref.dtype)


def _stick_sample(q, cache_k, cache_v, k, v, sg, row0):
    b, past, n_heads, _ = cache_k.shape
    d = n_heads * HEAD_DIM
    t_new = k.shape[0] // b
    hw = SAMPLE_HEADS * HEAD_DIM
    t = ATTN_T
    blk0 = row0 // t_new
    q3 = q.reshape(q.shape[0] // t_new, t_new, d)
    sg3 = sg.reshape(q3.shape)
    merged = pl.BlockSpec((1, t_new, hw), lambda bi, g: (blk0 + bi, 0, g))
    new = pl.BlockSpec((1, t_new, hw), lambda bi, g: (bi, 0, g))
    cache = pl.BlockSpec(memory_space=pl.ANY)
    block_buf = pltpu.VMEM((2, t, SAMPLE_HEADS, HEAD_DIM), F32)
    out = pl.pallas_call(
        _stick_sample_kernel,
        out_shape=jax.ShapeDtypeStruct((b, t_new, d), BF16),
        grid=(b, d // hw),
        in_specs=[merged, cache, cache, new, new, merged, pl.BlockSpec((t, t), lambda bi, g: (0, 0))],
        out_specs=new,
        scratch_shapes=[block_buf, block_buf, pltpu.SemaphoreType.DMA((2, 2))],
        compiler_params=_params(2),
        name="stick_sample",
    )(q3, cache_k, cache_v, k.reshape(b, t_new, d), v.reshape(b, t_new, d), sg3, _tri_incl(t))
    return out.reshape(b * t_new, d)


def _band_prompt_kernel(q_ref, k_ref, v_ref, sg_ref, bias_ref, o_ref, kbf_ref, vbf_ref):
    qi = pl.program_id(2)
    t = q_ref.shape[1]
    n_kb = bias_ref.shape[2] // t

    @pl.when(qi == 0)
    def _():
        kbf_ref[...] = k_ref[0].astype(BF16)
        vbf_ref[...] = v_ref[0].astype(BF16)

    heads = range(q_ref.shape[2] // HEAD_DIM)
    lanes = [slice(hh * HEAD_DIM, (hh + 1) * HEAD_DIM) for hh in heads]
    blocks = []
    for dd in range(n_kb):
        kb = qi - (n_kb - 1) + dd
        blocks.append((pl.multiple_of(jnp.maximum(kb, 0) * t, t), kb >= 0))
    scores = [[jnp.where(ok, _dot_nt(q_ref[0, :, lanes[hh]], kbf_ref[pl.ds(k0, t), lanes[hh]])
                         + bias_ref[hh, :, dd * t:(dd + 1) * t], NEG_INF)
               for dd, (k0, ok) in enumerate(blocks)] for hh in heads]
    maxes = [functools.reduce(jnp.maximum, [s.max(axis=-1, keepdims=True) for s in scores[hh]]) for hh in heads]
    probs = [[jnp.exp(s - maxes[hh]) for s in scores[hh]] for hh in heads]
    accs = [sum(_dot(p.astype(BF16), vbf_ref[pl.ds(k0, t), lanes[hh]]) for p, (k0, _) in zip(probs[hh], blocks))
            for hh in heads]
    for hh in heads:
        denom = sum(p.sum(axis=-1, keepdims=True) for p in probs[hh])
        o_ref[0, :, lanes[hh]] = (accs[hh] / denom * sg_ref[0, :, lanes[hh]].astype(F32)).astype(o_ref.dtype)


def _toeplitz(values_at, rows, cols):
    period = rows + cols
    j = jnp.arange(period)
    v = values_at(jnp.where(j < cols, j, j - period))
    flat = jnp.tile(v, (1,) * (v.ndim - 1) + (rows,))[..., :rows * (period - 1)]
    return flat.reshape(v.shape[:-1] + (rows, period - 1))[..., :cols]


def _band_bias_prompt(rel_bias, t):
    n_kb = (LEFT_CHUNKS * CHUNK) // t + 1
    back = (n_kb - 1) * t
    rb = rel_bias.astype(F32)
    bias = _toeplitz(lambda m: rb[:, jnp.clip(back - m, -REL_CLIP, REL_CLIP) + REL_CLIP], t, n_kb * t)
    r = jnp.arange(t)[:, None]
    c = jnp.arange(n_kb * t)[None, :]
    chunk_diff = (back + r) // CHUNK - c // CHUNK
    visible = (chunk_diff >= 0) & (chunk_diff <= LEFT_CHUNKS)
    return jnp.where(visible[None], bias, NEG_INF)


def _band_prompt(q, k, v, sg, rel_bias, batch, seq):
    d = k.shape[1]
    t = ATTN_T
    tiles = seq // t
    hw = BAND_HEADS * HEAD_DIM
    bias = _band_bias_prompt(rel_bias, t)
    blk_in = pl.BlockSpec((1, t, hw), lambda b, h, i: (0, b * tiles + i, h))
    kv = pl.BlockSpec((1, seq, hw), lambda b, h, i: (0, b, h))
    flat = lambda a: a.reshape(1, a.shape[0], d)
    out = pl.pallas_call(
        _band_prompt_kernel,
        out_shape=jax.ShapeDtypeStruct((batch, seq, d), BF16),
        grid=(batch, d // hw, tiles),
        in_specs=[blk_in, kv, kv, blk_in,
                  pl.BlockSpec((BAND_HEADS,) + bias.shape[1:], lambda b, h, i: (h, 0, 0))],
        out_specs=pl.BlockSpec((1, t, hw), lambda b, h, i: (b, i, h)),
        scratch_shapes=[pltpu.VMEM((seq, hw), BF16), pltpu.VMEM((seq, hw), BF16)],
        compiler_params=_params(3),
        name="band_prompt",
    )(flat(q), flat(k), flat(v), flat(sg), bias)
    return out.reshape(batch * seq, d)


def _band_sample_kernel(q_ref, ck_ref, cv_ref, k_ref, v_ref, sg_ref, bias_ref, o_ref):
    past = ck_ref.shape[1]
    heads = range(ck_ref.shape[2])
    lanes = [slice(hh * HEAD_DIM, (hh + 1) * HEAD_DIM) for hh in heads]
    s_old = [_dot_nt(q_ref[0, :, lanes[hh]], _head_rows(ck_ref, hh, 0, past).astype(BF16)) + bias_ref[hh, :, :past]
             for hh in heads]
    s_new = [_dot_nt(q_ref[0, :, lanes[hh]], k_ref[0, :, lanes[hh]].astype(BF16)) + bias_ref[hh, :, past:]
             for hh in heads]
    m = [jnp.maximum(a.max(axis=-1, keepdims=True), b.max(axis=-1, keepdims=True)) for a, b in zip(s_old, s_new)]
    p_old = [jnp.exp(s - mm) for s, mm in zip(s_old, m)]
    p_new = [jnp.exp(s - mm) for s, mm in zip(s_new, m)]
    accs = [_dot(p_old[hh].astype(BF16), _head_rows(cv_ref, hh, 0, past).astype(BF16))
            + _dot(p_new[hh].astype(BF16), v_ref[0, :, lanes[hh]].astype(BF16)) for hh in heads]
    for hh in heads:
        denom = p_old[hh].sum(axis=-1, keepdims=True) + p_new[hh].sum(axis=-1, keepdims=True)
        o_ref[0, :, lanes[hh]] = (accs[hh] / denom * sg_ref[0, :, lanes[hh]].astype(F32)).astype(o_ref.dtype)


def _band_bias_sample(rel_bias, past_total, past_b, t_new):
    rb = rel_bias.astype(F32)
    bias = _toeplitz(lambda m: rb[:, jnp.clip(past_b - m, -REL_CLIP, REL_CLIP) + REL_CLIP],
                     t_new, past_b + t_new)
    q_pos = past_total + jnp.arange(t_new)
    k_pos = past_total - past_b + jnp.arange(past_b + t_new)
    q_chunk = q_pos // CHUNK
    k_chunk = k_pos // CHUNK
    visible = ((k_pos[None, :] >= 0) & (k_chunk[None, :] <= q_chunk[:, None])
               & (k_chunk[None, :] >= q_chunk[:, None] - LEFT_CHUNKS))
    return jnp.where(visible[None], bias, NEG_INF)


def _band_sample(q, cache_k, cache_v, k, v, sg, rel_bias, row0, past_total):
    b, past_b, n_heads, _ = cache_k.shape
    d = n_heads * HEAD_DIM
    t_new = (q.shape[0] - row0) // b
    hw = SAMPLE_HEADS * HEAD_DIM
    blk0 = row0 // t_new
    bias = _band_bias_sample(rel_bias, past_total, past_b, t_new)
    r3 = lambda a: a.reshape(a.shape[0] // t_new, t_new, d)
    merged = pl.BlockSpec((1, t_new, hw), lambda bi, g: (blk0 + bi, 0, g))
    cache = pl.BlockSpec((1, past_b, SAMPLE_HEADS, HEAD_DIM), lambda bi, g: (bi, 0, g, 0))
    out = pl.pallas_call(
        _band_sample_kernel,
        out_shape=jax.ShapeDtypeStruct((b, t_new, d), BF16),
        grid=(b, d // hw),
        in_specs=[merged, cache, cache, merged, merged, merged,
                  pl.BlockSpec((SAMPLE_HEADS,) + bias.shape[1:], lambda bi, g: (g, 0, 0))],
        out_specs=pl.BlockSpec((1, t_new, hw), lambda bi, g: (bi, 0, g)),
        compiler_params=_params(2),
        name="band_sample",
    )(r3(q), cache_k, cache_v, r3(k), r3(v), r3(sg), bias)
    return out.reshape(b * t_new, d)


def kernel(x_prompt, x_sample, c_prompt, c_sample, cache_a_k, cache_a_v, cache_b_k, cache_b_v, w_mod_a, b_mod_a, g_norm_a, w_in_a, w_out_a, g_kv, w_kv, w_mod_b, b_mod_b, g_norm_b, w_in_b, rel_bias_b, w_out_b, g_final):
    bp, seq, d = x_prompt.shape
    bs, t_new, _ = x_sample.shape
    n_heads = d // HEAD_DIM
    past = cache_a_k.shape[2]
    past_b = cache_b_k.shape[1]
    mp, ms = bp * seq, bs * t_new
    assert w_mod_a.shape[0] == 1 and w_mod_b.shape[0] == 1, "one layer of each mixer"
    assert seq % MM_TM == 0 and mp % MM_TM == 0 and ms % MM_TM == 0 and MM_TM % t_new == 0
    assert seq % RESID_TM == 0 and ms % RESID_TM == 0 and RESID_TM % t_new == 0 and d % MM_TN == 0
    assert bs % (RESID_TM // t_new) == 0 and bs % (NORM_TM // t_new) == 0
    assert seq % STICK_TQ == 0 and STICK_TQ % ATTN_T == 0
    assert seq % ATTN_T == 0 and past % ATTN_T == 0 and (LEFT_CHUNKS * CHUNK) % ATTN_T == 0

    xp = x_prompt.reshape(mp, d)
    xs = x_sample.reshape(ms, d)

    n_c = bs + bp
    pad = -n_c % BF16_SUBLANES
    c_all = jnp.concatenate([c_sample, c_prompt, jnp.zeros((pad, d), F32)], axis=0)
    mod_a, mod_b = _adaln_tables(c_all, w_mod_a[0], b_mod_a[0], w_mod_b[0], b_mod_b[0])
    shift_a, scale_a, gate_a = _split_mod(mod_a, d)
    shift_b, scale_b, gate_b = _split_mod(mod_b, d)

    h_a = _prenorm(xp, xs, g_norm_a[0], shift_a, scale_a, seq, t_new)
    w_in = w_in_a[0]
    q_scale = 1.0 / math.sqrt(HEAD_DIM)
    q_a = _mm(h_a, w_in, 0, d, BF16, scale=q_scale)
    k_p = _mm(h_a, w_in, d, d, F32, row0=0, rows=mp)
    k_s = _mm(h_a, w_in, d, d, F32, row0=mp, rows=ms)
    v_p = _mm(h_a, w_in, 2 * d, d, F32, row0=0, rows=mp)
    v_s = _mm(h_a, w_in, 2 * d, d, F32, row0=mp, rows=ms)
    sg_a = _mm(h_a, w_in, 3 * d, d, BF16, silu=True)

    u_p = _stick_prompt(q_a, k_p, v_p, sg_a, bp, seq)
    u_s = _stick_sample(q_a, cache_a_k[0], cache_a_v[0], k_s, v_s, sg_a, mp)
    x1_p = _mm_resid(u_p, w_out_a[0], xp, gate_a, seq, bs)
    x1_s = _mm_resid(u_s, w_out_a[0], xs, gate_a, t_new, 0)

    h_kv, h_b = _dualnorm(x1_p, x1_s, g_kv, g_norm_b[0], shift_b, scale_b, seq, t_new)
    kb = _mm(h_kv, w_kv, 0, d, F32)
    vb = _mm(h_kv, w_kv, d, d, F32)
    q_b = _mm(h_b, w_in_b[0], 0, d, BF16, scale=q_scale)
    sg_b = _mm(h_b, w_in_b[0], d, d, BF16, silu=True)

    ub_p = _band_prompt(q_b, kb, vb, sg_b, rel_bias_b[0], bp, seq)
    ub_s = _band_sample(q_b, cache_b_k, cache_b_v, kb, vb, sg_b, rel_bias_b[0], mp, past)
    x2_p = _mm_resid(ub_p, w_out_b[0], x1_p, gate_b, seq, bs)
    x2_s = _mm_resid(ub_s, w_out_b[0], x1_s, gate_b, t_new, 0)

    y_p = _finalnorm(x2_p, g_final).reshape(bp, seq, d)
    y_s = _finalnorm(x2_s, g_final).reshape(bs, t_new, d)

    keep_b = min(LEFT_CHUNKS * CHUNK, seq)
    heads = lambda a, b, t: a.reshape(b, t, n_heads, HEAD_DIM)
    kb_p, vb_p = heads(kb[:mp], bp, seq), heads(vb[:mp], bp, seq)
    return (y_p, y_s,
            heads(k_p, bp, seq)[None], heads(v_p, bp, seq)[None],
            heads(k_s, bs, t_new)[None], heads(v_s, bs, t_new)[None],
            kb_p[:, seq - keep_b:], vb_p[:, seq - keep_b:],
            heads(kb[mp:], bs, t_new), heads(vb[mp:], bs, t_new))
```

```python
import functools
import math

import jax
import jax.numpy as jnp
from jax import lax
from jax.experimental import pallas as pl
from jax.experimental.pallas import tpu as pltpu

F32 = jnp.float32
BF16 = jnp.bfloat16

HEAD_DIM = 128
CHUNK = 64
LEFT_CHUNKS = 8
REL_CLIP = 128
EPS = 1e-6
NEG_INF = -1e30

V7X_VMEM_LIMIT_BYTES = 52 * 1024 * 1024
LANES = 128
BF16_SUBLANES = 16

MM_TM = 512
RESID_TM = 512
MM_TN = 1024
NORM_TM = 256
MOD_TN = 512
ATTN_T = 256
STICK_TQ = 512
STICK_HEADS = 4
BAND_HEADS = 4

F32_EXP_UNDERFLOW_LOG = -104.0
SAMPLE_HEADS = 8


def _params(n_grid):
    return pltpu.CompilerParams(
        dimension_semantics=("arbitrary",) * n_grid,
        vmem_limit_bytes=V7X_VMEM_LIMIT_BYTES)


def _silu(x):
    return x / (1.0 + jnp.exp(-x))


def _dot(a, b):
    return jnp.dot(a, b, preferred_element_type=F32)


def _dot_nt(a, b):
    return lax.dot_general(a, b, (((1,), (1,)), ((), ())), preferred_element_type=F32)


def _mod_kernel(c_ref, wa_ref, ba_ref, wb_ref, bb_ref, oa_ref, ob_ref):
    a = _silu(c_ref[...]).astype(BF16)
    oa_ref[...] = _dot(a, wa_ref[...].astype(BF16)) + ba_ref[...]
    ob_ref[...] = _dot(a, wb_ref[...].astype(BF16)) + bb_ref[...]


def _adaln_tables(c_all, w_a, b_a, w_b, b_b):
    rows, d = c_all.shape
    n = w_a.shape[1]
    w_spec = pl.BlockSpec((d, MOD_TN), lambda j: (0, j))
    b_spec = pl.BlockSpec((1, MOD_TN), lambda j: (0, j))
    o_spec = pl.BlockSpec((rows, MOD_TN), lambda j: (0, j))
    return pl.pallas_call(
        _mod_kernel,
        out_shape=(jax.ShapeDtypeStruct((rows, n), F32),) * 2,
        grid=(n // MOD_TN,),
        in_specs=[pl.BlockSpec((rows, d), lambda j: (0, 0)), w_spec, b_spec, w_spec, b_spec],
        out_specs=(o_spec, o_spec),
        compiler_params=_params(1),
        name="adaln_mod",
    )(c_all, w_a, b_a.reshape(1, n), w_b, b_b.reshape(1, n))


def _split_mod(mod, d):
    rows = mod.shape[0]
    return tuple(mod[:, k * d:(k + 1) * d].reshape(rows, 1, d) for k in range(3))


def _rms_scale(x):
    return lax.rsqrt(jnp.mean(x * x, axis=-1, keepdims=True) + EPS)


def _modulate(y, shift_ref, scale_ref):
    nb = shift_ref.shape[0]
    tm, d = y.shape
    y3 = y.reshape(nb, tm // nb, d)
    return (y3 * (1.0 + scale_ref[...]) + shift_ref[...]).reshape(tm, d)


def _prenorm_kernel(xp_ref, xs_ref, g_ref, shp_ref, scp_ref, shs_ref, scs_ref, o_ref, *, n_prompt_tiles):
    i = pl.program_id(0)

    def run(x_ref, shift_ref, scale_ref):
        x = x_ref[...]
        y = x * _rms_scale(x) * g_ref[...]
        o_ref[...] = _modulate(y, shift_ref, scale_ref).astype(o_ref.dtype)

    @pl.when(i < n_prompt_tiles)
    def _():
        run(xp_ref, shp_ref, scp_ref)

    @pl.when(i >= n_prompt_tiles)
    def _():
        run(xs_ref, shs_ref, scs_ref)


def _group_specs(tm, d, n_prompt_tiles, prompt_rpb, sample_rpb, n_sample_batches):
    tiles_per_prompt_batch = prompt_rpb // tm
    nb_s = tm // sample_rpb
    rows_p = pl.BlockSpec((tm, d), lambda i: (jnp.minimum(i, n_prompt_tiles - 1), 0))
    rows_s = pl.BlockSpec((tm, d), lambda i: (jnp.maximum(i - n_prompt_tiles, 0), 0))
    tab_p = pl.BlockSpec(
        (1, 1, d),
        lambda i: (n_sample_batches + jnp.minimum(i, n_prompt_tiles - 1) // tiles_per_prompt_batch, 0, 0))
    tab_s = pl.BlockSpec((nb_s, 1, d), lambda i: (jnp.maximum(i - n_prompt_tiles, 0), 0, 0))
    return rows_p, rows_s, tab_p, tab_s


def _prenorm(x_p, x_s, g, shift, scale, prompt_rpb, sample_rpb):
    mp, d = x_p.shape
    ms = x_s.shape[0]
    tm = NORM_TM
    npt = mp // tm
    rows_p, rows_s, tab_p, tab_s = _group_specs(tm, d, npt, prompt_rpb, sample_rpb, ms // sample_rpb)
    return pl.pallas_call(
        functools.partial(_prenorm_kernel, n_prompt_tiles=npt),
        out_shape=jax.ShapeDtypeStruct((mp + ms, d), BF16),
        grid=((mp + ms) // tm,),
        in_specs=[rows_p, rows_s, pl.BlockSpec((1, d), lambda i: (0, 0)), tab_p, tab_p, tab_s, tab_s],
        out_specs=pl.BlockSpec((tm, d), lambda i: (i, 0)),
        compiler_params=_params(1),
        name="prenorm",
    )(x_p, x_s, g.reshape(1, d), shift, scale, shift, scale)


def _dualnorm_kernel(xp_ref, xs_ref, gkv_ref, gb_ref, shp_ref, scp_ref, shs_ref, scs_ref,
                     okv_ref, ob_ref, *, n_prompt_tiles):
    i = pl.program_id(0)

    def run(x_ref, shift_ref, scale_ref):
        x = x_ref[...]
        y = x * _rms_scale(x)
        okv_ref[...] = (y * gkv_ref[...]).astype(okv_ref.dtype)
        ob_ref[...] = _modulate(y * gb_ref[...], shift_ref, scale_ref).astype(ob_ref.dtype)

    @pl.when(i < n_prompt_tiles)
    def _():
        run(xp_ref, shp_ref, scp_ref)

    @pl.when(i >= n_prompt_tiles)
    def _():
        run(xs_ref, shs_ref, scs_ref)


def _dualnorm(x_p, x_s, g_kv, g_b, shift, scale, prompt_rpb, sample_rpb):
    mp, d = x_p.shape
    ms = x_s.shape[0]
    tm = NORM_TM
    npt = mp // tm
    rows_p, rows_s, tab_p, tab_s = _group_specs(tm, d, npt, prompt_rpb, sample_rpb, ms // sample_rpb)
    g_spec = pl.BlockSpec((1, d), lambda i: (0, 0))
    o_spec = pl.BlockSpec((tm, d), lambda i: (i, 0))
    return pl.pallas_call(
        functools.partial(_dualnorm_kernel, n_prompt_tiles=npt),
        out_shape=(jax.ShapeDtypeStruct((mp + ms, d), BF16),) * 2,
        grid=((mp + ms) // tm,),
        in_specs=[rows_p, rows_s, g_spec, g_spec, tab_p, tab_p, tab_s, tab_s],
        out_specs=(o_spec, o_spec),
        compiler_params=_params(1),
        name="dualnorm",
    )(x_p, x_s, g_kv.reshape(1, d), g_b.reshape(1, d), shift, scale, shift, scale)


def _finalnorm_kernel(x_ref, g_ref, o_ref):
    x = x_ref[...]
    o_ref[...] = x * _rms_scale(x) * g_ref[...]


def _finalnorm(x, g):
    m, d = x.shape
    tm = NORM_TM
    return pl.pallas_call(
        _finalnorm_kernel,
        out_shape=jax.ShapeDtypeStruct((m, d), F32),
        grid=(m // tm,),
        in_specs=[pl.BlockSpec((tm, d), lambda i: (i, 0)), pl.BlockSpec((1, d), lambda i: (0, 0))],
        out_specs=pl.BlockSpec((tm, d), lambda i: (i, 0)),
        compiler_params=_params(1),
        name="finalnorm",
    )(x, g.reshape(1, d))


def _stage_weight_tile(w_hbm, stage_ref, wbf_ref, sem, col0):
    j, nj = pl.program_id(0), pl.num_programs(0)
    tn = wbf_ref.shape[1]

    def tile_copy(jj):
        return pltpu.make_async_copy(w_hbm.at[:, pl.ds(col0 + jj * tn, tn)], stage_ref, sem)

    @pl.when(pl.program_id(1) == 0)
    def _():
        @pl.when(j == 0)
        def _():
            tile_copy(0).start()

        tile_copy(j).wait()
        wbf_ref[...] = stage_ref[...].astype(BF16)

        @pl.when(j + 1 < nj)
        def _():
            tile_copy(j + 1).start()


def _weight_scratch(k, tn):
    return [pltpu.VMEM((k, tn), F32), pltpu.VMEM((k, tn), BF16), pltpu.SemaphoreType.DMA(())]


def _mm_kernel(h_ref, w_hbm, *refs, silu, scale, col0):
    *o_refs, stage_ref, wbf_ref, sem = refs
    _stage_weight_tile(w_hbm, stage_ref, wbf_ref, sem, col0)
    acc = _dot(h_ref[...], wbf_ref[...])
    if silu:
        acc = _silu(acc)
    if scale is not None:
        acc = acc * scale
    for o_ref in o_refs:
        o_ref[...] = acc.astype(o_ref.dtype)


def _mm(h, w, col0, n_out, out_dtypes, *, silu=False, scale=None, row0=0, rows=None):
    k = h.shape[1]
    rows = h.shape[0] if rows is None else rows
    tm, tn = MM_TM, MM_TN
    i0 = row0 // tm
    o_spec = pl.BlockSpec((tm, tn), lambda j, i: (i, j))
    return pl.pallas_call(
        functools.partial(_mm_kernel, silu=silu, scale=scale, col0=col0),
        out_shape=tuple(jax.ShapeDtypeStruct((rows, n_out), dt) for dt in out_dtypes),
        grid=(n_out // tn, rows // tm),
        in_specs=[pl.BlockSpec((tm, k), lambda j, i: (i0 + i, 0)),
                  pl.BlockSpec(memory_space=pl.ANY)],
        out_specs=tuple(o_spec for _ in out_dtypes),
        scratch_shapes=_weight_scratch(k, tn),
        compiler_params=_params(2),
        name="proj",
    )(h, w)


def _mm_resid_kernel(u_ref, w_hbm, x_ref, gate_ref, o_ref, stage_ref, wbf_ref, sem):
    _stage_weight_tile(w_hbm, stage_ref, wbf_ref, sem, 0)
    acc = _dot(u_ref[...], wbf_ref[...])
    nb = gate_ref.shape[0]
    tm, tn = acc.shape
    y = acc.reshape(nb, tm // nb, tn) * gate_ref[...]
    o_ref[...] = x_ref[...] + y.reshape(tm, tn)


def _mm_resid(u, w, x, gate, rows_per_batch, table_row0):
    m, k = u.shape
    n = w.shape[1]
    tm, tn = RESID_TM, MM_TN
    if rows_per_batch >= tm:
        nb = 1
        tiles_per_batch = rows_per_batch // tm
        gate_map = lambda j, i: (table_row0 + i // tiles_per_batch, 0, j)
    else:
        nb = tm // rows_per_batch
        blk0 = table_row0 // nb
        gate_map = lambda j, i: (blk0 + i, 0, j)
    return pl.pallas_call(
        _mm_resid_kernel,
        out_shape=jax.ShapeDtypeStruct((m, n), F32),
        grid=(n // tn, m // tm),
        in_specs=[pl.BlockSpec((tm, k), lambda j, i: (i, 0)),
                  pl.BlockSpec(memory_space=pl.ANY),
                  pl.BlockSpec((tm, tn), lambda j, i: (i, j)),
                  pl.BlockSpec((nb, 1, tn), gate_map)],
        out_specs=pl.BlockSpec((tm, tn), lambda j, i: (i, j)),
        scratch_shapes=_weight_scratch(k, tn),
        compiler_params=_params(2),
        name="out_proj",
    )(u, w, x, gate)


def _softplus(z):
    return jnp.maximum(z, 0.0) + jnp.log(1.0 + jnp.exp(-jnp.abs(z)))


def _suffix_sum(x, tri):
    hi = x.astype(BF16)
    lo = (x - hi.astype(F32)).astype(BF16)
    return _dot(hi, tri) + _dot(lo, tri)


def _stick_blocks(qs, kblks, vblks, tri, carries, accs, causal):
    zs = [_dot_nt(q, kblk) for q, kblk in zip(qs, kblks)]
    sps = [_softplus(z) if causal is None else jnp.where(causal, _softplus(z), 0.0) for z in zs]
    incls = [_suffix_sum(sp, tri) for sp in sps]
    ws = [jnp.exp(z - incl + carry) for z, incl, carry in zip(zs, incls, carries)]
    if causal is not None:
        ws = [jnp.where(causal, w, 0.0) for w in ws]
    accs = [acc + _dot(w.astype(BF16), vblk) for acc, w, vblk in zip(accs, ws, vblks)]
    carries = [carry - incl[:, :1] for carry, incl in zip(carries, incls)]
    return carries, accs


def _strict_lower(tq, tk):
    return lax.broadcasted_iota(jnp.int32, (tq, tk), 1) < lax.broadcasted_iota(jnp.int32, (tq, tk), 0)


def _all_weights_vanish(carry):
    return jnp.max(carry) <= F32_EXP_UNDERFLOW_LOG


def _stick_prompt_kernel(q_ref, k_ref, v_ref, sg_ref, tri_ref, o_ref, *, tk):
    qi = pl.program_id(2)
    tq = q_ref.shape[1]
    n_sub = tq // tk
    heads = range(q_ref.shape[2] // HEAD_DIM)
    lanes = [slice(hh * HEAD_DIM, (hh + 1) * HEAD_DIM) for hh in heads]
    qs = [q_ref[0, :, lanes[hh]] for hh in heads]
    tri = tri_ref[...]

    def kv(k0):
        k0 = pl.multiple_of(k0, tk)
        return ([k_ref[0, pl.ds(k0, tk), lanes[hh]] for hh in heads],
                [v_ref[0, pl.ds(k0, tk), lanes[hh]] for hh in heads])

    carries = [jnp.zeros((tq, 1), F32) for _ in heads]
    accs = [jnp.zeros((tq, HEAD_DIM), F32) for _ in heads]
    for sb in reversed(range(n_sub)):
        r0 = sb * tk
        kblks, vblks = kv(qi * tq + r0)
        c_sub, a_sub = _stick_blocks([q[r0:] for q in qs], kblks, vblks, tri, [c[r0:] for c in carries],
                                     [a[r0:] for a in accs], _strict_lower(tq - r0, tk))
        if r0:
            c_sub = [jnp.concatenate([c[:r0], cs], axis=0) for c, cs in zip(carries, c_sub)]
            a_sub = [jnp.concatenate([a[:r0], as_], axis=0) for a, as_ in zip(accs, a_sub)]
        carries, accs = c_sub, a_sub

    def cond(state):
        it, carries, _ = state
        return jnp.logical_and(
            it < qi * n_sub, jnp.logical_not(_all_weights_vanish(functools.reduce(jnp.maximum, carries))))

    def body(state):
        it, carries, accs = state
        kblks, vblks = kv((qi * n_sub - 1 - it) * tk)
        carries, accs = _stick_blocks(qs, kblks, vblks, tri, carries, accs, None)
        return it + 1, tuple(carries), tuple(accs)

    _, carries, accs = lax.while_loop(cond, body, (jnp.int32(0), tuple(carries), tuple(accs)))
    for hh in heads:
        o_ref[0, :, lanes[hh]] = (accs[hh] * sg_ref[0, :, lanes[hh]].astype(F32)).astype(o_ref.dtype)


def _tri_incl(t):
    return (jnp.arange(t)[:, None] >= jnp.arange(t)[None, :]).astype(BF16)


def _stick_prompt(q, k, v, sg, batch, seq):
    d = k.shape[1]
    tq, tk = STICK_TQ, ATTN_T
    tiles = seq // tq
    hw = STICK_HEADS * HEAD_DIM
    flat = lambda a: a.reshape(1, a.shape[0], d)
    blk_in = pl.BlockSpec((1, tq, hw), lambda b, h, i: (0, b * tiles + i, h))
    kv = pl.BlockSpec((1, seq, hw), lambda b, h, i: (0, b, h))
    out = pl.pallas_call(
        functools.partial(_stick_prompt_kernel, tk=tk),
        out_shape=jax.ShapeDtypeStruct((batch, seq, d), BF16),
        grid=(batch, d // hw, tiles),
        in_specs=[blk_in, kv, kv, blk_in, pl.BlockSpec((tk, tk), lambda b, h, i: (0, 0))],
        out_specs=pl.BlockSpec((1, tq, hw), lambda b, h, i: (b, i, h)),
        compiler_params=_params(3),
        name="stick_prompt",
    )(flat(q), flat(k), flat(v), flat(sg), _tri_incl(tk))
    return out.reshape(batch * seq, d)


def _head_rows(cache_ref, hh, t0, n):
    nh, hd = cache_ref.shape[-2:]
    flat = cache_ref.reshape(math.prod(cache_ref.shape[:-1]), hd)
    return flat[pl.ds(t0 * nh + hh, n, stride=nh), :]


def _stick_sample_kernel(q_ref, ck_hbm, cv_hbm, k_ref, v_ref, sg_ref, tri_ref, o_ref, kbuf, vbuf, sem):
    bi, g = pl.program_id(0), pl.program_id(1)
    t_new = q_ref.shape[1]
    past = ck_hbm.shape[1]
    _, kblock, nh, _ = kbuf.shape
    n_blk = past // kblock
    heads = range(nh)
    lanes = [slice(hh * HEAD_DIM, (hh + 1) * HEAD_DIM) for hh in heads]
    tri = tri_ref[...]
    qs = [q_ref[0, :, lanes[hh]] for hh in heads]

    def block_copies(blk, slot):
        rows = pl.ds(past - (blk + 1) * kblock, kblock)
        group = pl.ds(g * nh, nh)
        return (pltpu.make_async_copy(ck_hbm.at[bi, rows, group, :], kbuf.at[slot], sem.at[0, slot]),
                pltpu.make_async_copy(cv_hbm.at[bi, rows, group, :], vbuf.at[slot], sem.at[1, slot]))

    for c in block_copies(0, 0):
        c.start()

    carries, accs = _stick_blocks(
        qs, [k_ref[0, :, lanes[hh]].astype(BF16) for hh in heads],
        [v_ref[0, :, lanes[hh]].astype(BF16) for hh in heads], tri_ref[:t_new, :t_new],
        [jnp.zeros((t_new, 1), F32) for _ in heads], [jnp.zeros((t_new, HEAD_DIM), F32) for _ in heads],
        _strict_lower(t_new, t_new))

    def cond(state):
        it, carries, _ = state
        return jnp.logical_and(it < n_blk,
                               jnp.logical_not(_all_weights_vanish(functools.reduce(jnp.maximum, carries))))

    def body(state):
        it, carries, accs = state
        slot = it % 2
        for c in block_copies(it, slot):
            c.wait()

        @pl.when(it + 1 < n_blk)
        def _():
            for c in block_copies(it + 1, 1 - slot):
                c.start()

        t0 = slot * kblock
        carries, accs = _stick_blocks(
            qs, [_head_rows(kbuf, hh, t0, kblock).astype(BF16) for hh in heads],
            [_head_rows(vbuf, hh, t0, kblock).astype(BF16) for hh in heads], tri, carries, accs, None)
        return it + 1, tuple(carries), tuple(accs)

    done, carries, accs = lax.while_loop(cond, body, (jnp.int32(0), tuple(carries), tuple(accs)))

    @pl.when(done < n_blk)
    def _():
        for c in block_copies(done, done % 2):
            c.wait()

    for hh in heads:
        o_ref[0, :, lanes[hh]] = (accs[hh] * sg_ref[0, :, lanes[hh]].astype(F32)).astype(o_ref.dtype)


def _stick_sample(q, cache_k, cache_v, k, v, sg, row0):
    b, past, n_heads, _ = cache_k.shape
    d = n_heads * HEAD_DIM
    t_new = k.shape[0] // b
    hw = SAMPLE_HEADS * HEAD_DIM
    t = ATTN_T
    blk0 = row0 // t_new
    q3 = q.reshape(q.shape[0] // t_new, t_new, d)
    sg3 = sg.reshape(q3.shape)
    merged = pl.BlockSpec((1, t_new, hw), lambda bi, g: (blk0 + bi, 0, g))
    new = pl.BlockSpec((1, t_new, hw), lambda bi, g: (bi, 0, g))
    cache = pl.BlockSpec(memory_space=pl.ANY)
    block_buf = pltpu.VMEM((2, t, SAMPLE_HEADS, HEAD_DIM), F32)
    out = pl.pallas_call(
        _stick_sample_kernel,
        out_shape=jax.ShapeDtypeStruct((b, t_new, d), BF16),
        grid=(b, d // hw),
        in_specs=[merged, cache, cache, new, new, merged, pl.BlockSpec((t, t), lambda bi, g: (0, 0))],
        out_specs=new,
        scratch_shapes=[block_buf, block_buf, pltpu.SemaphoreType.DMA((2, 2))],
        compiler_params=_params(2),
        name="stick_sample",
    )(q3, cache_k, cache_v, k.reshape(b, t_new, d), v.reshape(b, t_new, d), sg3, _tri_incl(t))
    return out.reshape(b * t_new, d)


def _band_prompt_kernel(q_ref, k_ref, v_ref, sg_ref, bias_ref, o_ref):
    qi = pl.program_id(2)
    t = q_ref.shape[1]
    n_kb = bias_ref.shape[2] // t
    heads = range(q_ref.shape[2] // HEAD_DIM)
    lanes = [slice(hh * HEAD_DIM, (hh + 1) * HEAD_DIM) for hh in heads]
    blocks = []
    for dd in range(n_kb):
        kb = qi - (n_kb - 1) + dd
        blocks.append((pl.multiple_of(jnp.maximum(kb, 0) * t, t), kb >= 0))
    scores = [[jnp.where(ok, _dot_nt(q_ref[0, :, lanes[hh]], k_ref[0, pl.ds(k0, t), lanes[hh]])
                         + bias_ref[hh, :, dd * t:(dd + 1) * t], NEG_INF)
               for dd, (k0, ok) in enumerate(blocks)] for hh in heads]
    maxes = [functools.reduce(jnp.maximum, [s.max(axis=-1, keepdims=True) for s in scores[hh]]) for hh in heads]
    probs = [[jnp.exp(s - maxes[hh]) for s in scores[hh]] for hh in heads]
    accs = [sum(_dot(p.astype(BF16), v_ref[0, pl.ds(k0, t), lanes[hh]]) for p, (k0, _) in zip(probs[hh], blocks))
            for hh in heads]
    for hh in heads:
        denom = sum(p.sum(axis=-1, keepdims=True) for p in probs[hh])
        o_ref[0, :, lanes[hh]] = (accs[hh] / denom * sg_ref[0, :, lanes[hh]].astype(F32)).astype(o_ref.dtype)


def _toeplitz(values_at, rows, cols):
    period = rows + cols
    j = jnp.arange(period)
    v = values_at(jnp.where(j < cols, j, j - period))
    flat = jnp.tile(v, (1,) * (v.ndim - 1) + (rows,))[..., :rows * (period - 1)]
    return flat.reshape(v.shape[:-1] + (rows, period - 1))[..., :cols]


def _band_bias_prompt(rel_bias, t):
    n_kb = (LEFT_CHUNKS * CHUNK) // t + 1
    back = (n_kb - 1) * t
    rb = rel_bias.astype(F32)
    bias = _toeplitz(lambda m: rb[:, jnp.clip(back - m, -REL_CLIP, REL_CLIP) + REL_CLIP], t, n_kb * t)
    r = jnp.arange(t)[:, None]
    c = jnp.arange(n_kb * t)[None, :]
    chunk_diff = (back + r) // CHUNK - c // CHUNK
    visible = (chunk_diff >= 0) & (chunk_diff <= LEFT_CHUNKS)
    return jnp.where(visible[None], bias, NEG_INF)


def _band_prompt(q, k, v, sg, rel_bias, batch, seq):
    d = k.shape[1]
    t = ATTN_T
    tiles = seq // t
    hw = BAND_HEADS * HEAD_DIM
    bias = _band_bias_prompt(rel_bias, t)
    blk_in = pl.BlockSpec((1, t, hw), lambda b, h, i: (0, b * tiles + i, h))
    kv = pl.BlockSpec((1, seq, hw), lambda b, h, i: (0, b, h))
    flat = lambda a: a.reshape(1, a.shape[0], d)
    out = pl.pallas_call(
        _band_prompt_kernel,
        out_shape=jax.ShapeDtypeStruct((batch, seq, d), BF16),
        grid=(batch, d // hw, tiles),
        in_specs=[blk_in, kv, kv, blk_in,
                  pl.BlockSpec((BAND_HEADS,) + bias.shape[1:], lambda b, h, i: (h, 0, 0))],
        out_specs=pl.BlockSpec((1, t, hw), lambda b, h, i: (b, i, h)),
        compiler_params=_params(3),
        name="band_prompt",
    )(flat(q), flat(k), flat(v), flat(sg), bias)
    return out.reshape(batch * seq, d)


def _band_sample_kernel(q_ref, ck_ref, cv_ref, k_ref, v_ref, sg_ref, bias_ref, o_ref):
    past = ck_ref.shape[1]
    heads = range(ck_ref.shape[2])
    lanes = [slice(hh * HEAD_DIM, (hh + 1) * HEAD_DIM) for hh in heads]
    s_old = [_dot_nt(q_ref[0, :, lanes[hh]], _head_rows(ck_ref, hh, 0, past).astype(BF16)) + bias_ref[hh, :, :past]
             for hh in heads]
    s_new = [_dot_nt(q_ref[0, :, lanes[hh]], k_ref[0, :, lanes[hh]].astype(BF16)) + bias_ref[hh, :, past:]
             for hh in heads]
    m = [jnp.maximum(a.max(axis=-1, keepdims=True), b.max(axis=-1, keepdims=True)) for a, b in zip(s_old, s_new)]
    p_old = [jnp.exp(s - mm) for s, mm in zip(s_old, m)]
    p_new = [jnp.exp(s - mm) for s, mm in zip(s_new, m)]
    accs = [_dot(p_old[hh].astype(BF16), _head_rows(cv_ref, hh, 0, past).astype(BF16))
            + _dot(p_new[hh].astype(BF16), v_ref[0, :, lanes[hh]].astype(BF16)) for hh in heads]
    for hh in heads:
        denom = p_old[hh].sum(axis=-1, keepdims=True) + p_new[hh].sum(axis=-1, keepdims=True)
        o_ref[0, :, lanes[hh]] = (accs[hh] / denom * sg_ref[0, :, lanes[hh]].astype(F32)).astype(o_ref.dtype)


def _band_bias_sample(rel_bias, past_total, past_b, t_new):
    rb = rel_bias.astype(F32)
    bias = _toeplitz(lambda m: rb[:, jnp.clip(past_b - m, -REL_CLIP, REL_CLIP) + REL_CLIP],
                     t_new, past_b + t_new)
    q_pos = past_total + jnp.arange(t_new)
    k_pos = past_total - past_b + jnp.arange(past_b + t_new)
    q_chunk = q_pos // CHUNK
    k_chunk = k_pos // CHUNK
    visible = ((k_pos[None, :] >= 0) & (k_chunk[None, :] <= q_chunk[:, None])
               & (k_chunk[None, :] >= q_chunk[:, None] - LEFT_CHUNKS))
    return jnp.where(visible[None], bias, NEG_INF)


def _band_sample(q, cache_k, cache_v, k, v, sg, rel_bias, row0, past_total):
    b, past_b, n_heads, _ = cache_k.shape
    d = n_heads * HEAD_DIM
    t_new = (q.shape[0] - row0) // b
    hw = SAMPLE_HEADS * HEAD_DIM
    blk0 = row0 // t_new
    bias = _band_bias_sample(rel_bias, past_total, past_b, t_new)
    r3 = lambda a: a.reshape(a.shape[0] // t_new, t_new, d)
    merged = pl.BlockSpec((1, t_new, hw), lambda bi, g: (blk0 + bi, 0, g))
    cache = pl.BlockSpec((1, past_b, SAMPLE_HEADS, HEAD_DIM), lambda bi, g: (bi, 0, g, 0))
    out = pl.pallas_call(
        _band_sample_kernel,
        out_shape=jax.ShapeDtypeStruct((b, t_new, d), BF16),
        grid=(b, d // hw),
        in_specs=[merged, cache, cache, merged, merged, merged,
                  pl.BlockSpec((SAMPLE_HEADS,) + bias.shape[1:], lambda bi, g: (g, 0, 0))],
        out_specs=pl.BlockSpec((1, t_new, hw), lambda bi, g: (bi, 0, g)),
        compiler_params=_params(2),
        name="band_sample",
    )(r3(q), cache_k, cache_v, r3(k), r3(v), r3(sg), bias)
    return out.reshape(b * t_new, d)


def kernel(x_prompt, x_sample, c_prompt, c_sample, cache_a_k, cache_a_v, cache_b_k, cache_b_v, w_mod_a, b_mod_a, g_norm_a, w_in_a, w_out_a, g_kv, w_kv, w_mod_b, b_mod_b, g_norm_b, w_in_b, rel_bias_b, w_out_b, g_final):
    bp, seq, d = x_prompt.shape
    bs, t_new, _ = x_sample.shape
    n_heads = d // HEAD_DIM
    past = cache_a_k.shape[2]
    past_b = cache_b_k.shape[1]
    mp, ms = bp * seq, bs * t_new
    assert w_mod_a.shape[0] == 1 and w_mod_b.shape[0] == 1, "one layer of each mixer"
    assert seq % MM_TM == 0 and mp % MM_TM == 0 and ms % MM_TM == 0 and MM_TM % t_new == 0
    assert seq % RESID_TM == 0 and ms % RESID_TM == 0 and RESID_TM % t_new == 0 and d % MM_TN == 0
    assert bs % (RESID_TM // t_new) == 0 and bs % (NORM_TM // t_new) == 0
    assert seq % STICK_TQ == 0 and STICK_TQ % ATTN_T == 0
    assert seq % ATTN_T == 0 and past % ATTN_T == 0 and (LEFT_CHUNKS * CHUNK) % ATTN_T == 0

    xp = x_prompt.reshape(mp, d)
    xs = x_sample.reshape(ms, d)

    n_c = bs + bp
    pad = -n_c % BF16_SUBLANES
    c_all = jnp.concatenate([c_sample, c_prompt, jnp.zeros((pad, d), F32)], axis=0)
    mod_a, mod_b = _adaln_tables(c_all, w_mod_a[0], b_mod_a[0], w_mod_b[0], b_mod_b[0])
    shift_a, scale_a, gate_a = _split_mod(mod_a, d)
    shift_b, scale_b, gate_b = _split_mod(mod_b, d)

    h_a = _prenorm(xp, xs, g_norm_a[0], shift_a, scale_a, seq, t_new)
    w_in = w_in_a[0]
    q_scale = 1.0 / math.sqrt(HEAD_DIM)
    q_a, = _mm(h_a, w_in, 0, d, (BF16,), scale=q_scale)
    k_p, k_p16 = _mm(h_a, w_in, d, d, (F32, BF16), row0=0, rows=mp)
    k_s, = _mm(h_a, w_in, d, d, (F32,), row0=mp, rows=ms)
    v_p, v_p16 = _mm(h_a, w_in, 2 * d, d, (F32, BF16), row0=0, rows=mp)
    v_s, = _mm(h_a, w_in, 2 * d, d, (F32,), row0=mp, rows=ms)
    sg_a, = _mm(h_a, w_in, 3 * d, d, (BF16,), silu=True)

    u_p = _stick_prompt(q_a, k_p16, v_p16, sg_a, bp, seq)
    u_s = _stick_sample(q_a, cache_a_k[0], cache_a_v[0], k_s, v_s, sg_a, mp)
    x1_p = _mm_resid(u_p, w_out_a[0], xp, gate_a, seq, bs)
    x1_s = _mm_resid(u_s, w_out_a[0], xs, gate_a, t_new, 0)

    h_kv, h_b = _dualnorm(x1_p, x1_s, g_kv, g_norm_b[0], shift_b, scale_b, seq, t_new)
    kb, kb16 = _mm(h_kv, w_kv, 0, d, (F32, BF16))
    vb, vb16 = _mm(h_kv, w_kv, d, d, (F32, BF16))
    q_b, = _mm(h_b, w_in_b[0], 0, d, (BF16,), scale=q_scale)
    sg_b, = _mm(h_b, w_in_b[0], d, d, (BF16,), silu=True)

    ub_p = _band_prompt(q_b, kb16, vb16, sg_b, rel_bias_b[0], bp, seq)
    ub_s = _band_sample(q_b, cache_b_k, cache_b_v, kb16, vb16, sg_b, rel_bias_b[0], mp, past)
    x2_p = _mm_resid(ub_p, w_out_b[0], x1_p, gate_b, seq, bs)
    x2_s = _mm_resid(ub_s, w_out_b[0], x1_s, gate_b, t_new, 0)

    y_p = _finalnorm(x2_p, g_final).reshape(bp, seq, d)
    y_s = _finalnorm(x2_s, g_final).reshape(bs, t_new, d)

    keep_b = min(LEFT_CHUNKS * CHUNK, seq)
    heads = lambda a, b, t: a.reshape(b, t, n_heads, HEAD_DIM)
    tail = lambda a: jnp.stack([a[(b + 1) * seq - keep_b:(b + 1) * seq] for b in range(bp)])
    return (y_p, y_s,
            heads(k_p, bp, seq)[None], heads(v_p, bp, seq)[None],
            heads(k_s, bs, t_new)[None], heads(v_s, bs, t_new)[None],
            heads(tail(kb), bp, keep_b), heads(tail(vb), bp, keep_b),
            heads(kb[mp:], bs, t_new), heads(vb[mp:], bs, t_new))
```

```python
import functools
import math

import jax
import jax.numpy as jnp
from jax import lax
from jax.experimental import pallas as pl
from jax.experimental.pallas import tpu as pltpu

F32 = jnp.float32
BF16 = jnp.bfloat16

HEAD_DIM = 128
CHUNK = 64
LEFT_CHUNKS = 8
REL_CLIP = 128
EPS = 1e-6
NEG_INF = -1e30

V7X_VMEM_LIMIT_BYTES = 52 * 1024 * 1024
LANES = 128
BF16_SUBLANES = 16

MM_TM = 512
RESID_TM = 512
MM_TN = 1024
NORM_TM = 256
MOD_TN = 512
ATTN_T = 256
STICK_TQ = 256
STICK_HEADS = 8
BAND_HEADS = 4

F32_EXP_UNDERFLOW_LOG = -104.0
SAMPLE_HEADS = 8
CACHE_SLOTS = 3


def _params(n_grid):
    return pltpu.CompilerParams(
        dimension_semantics=("arbitrary",) * n_grid,
        vmem_limit_bytes=V7X_VMEM_LIMIT_BYTES)


def _silu(x):
    return x / (1.0 + jnp.exp(-x))


def _dot(a, b):
    return jnp.dot(a, b, preferred_element_type=F32)


def _dot_nt(a, b):
    return lax.dot_general(a, b, (((1,), (1,)), ((), ())), preferred_element_type=F32)


def _mod_kernel(c_ref, wa_ref, ba_ref, wb_ref, bb_ref, oa_ref, ob_ref):
    a = _silu(c_ref[...]).astype(BF16)
    oa_ref[...] = _dot(a, wa_ref[...].astype(BF16)) + ba_ref[...]
    ob_ref[...] = _dot(a, wb_ref[...].astype(BF16)) + bb_ref[...]


def _adaln_tables(c_all, w_a, b_a, w_b, b_b):
    rows, d = c_all.shape
    n = w_a.shape[1]
    w_spec = pl.BlockSpec((d, MOD_TN), lambda j: (0, j))
    b_spec = pl.BlockSpec((1, MOD_TN), lambda j: (0, j))
    o_spec = pl.BlockSpec((rows, MOD_TN), lambda j: (0, j))
    return pl.pallas_call(
        _mod_kernel,
        out_shape=(jax.ShapeDtypeStruct((rows, n), F32),) * 2,
        grid=(n // MOD_TN,),
        in_specs=[pl.BlockSpec((rows, d), lambda j: (0, 0)), w_spec, b_spec, w_spec, b_spec],
        out_specs=(o_spec, o_spec),
        compiler_params=_params(1),
        name="adaln_mod",
    )(c_all, w_a, b_a.reshape(1, n), w_b, b_b.reshape(1, n))


def _split_mod(mod, d):
    rows = mod.shape[0]
    return tuple(mod[:, k * d:(k + 1) * d].reshape(rows, 1, d) for k in range(3))


def _rms_scale(x):
    return lax.rsqrt(jnp.mean(x * x, axis=-1, keepdims=True) + EPS)


def _modulate(y, shift_ref, scale_ref):
    nb = shift_ref.shape[0]
    tm, d = y.shape
    y3 = y.reshape(nb, tm // nb, d)
    return (y3 * (1.0 + scale_ref[...]) + shift_ref[...]).reshape(tm, d)


def _prenorm_kernel(xp_ref, xs_ref, g_ref, shp_ref, scp_ref, shs_ref, scs_ref, o_ref, *, n_prompt_tiles):
    i = pl.program_id(0)

    def run(x_ref, shift_ref, scale_ref):
        x = x_ref[...]
        y = x * _rms_scale(x) * g_ref[...]
        o_ref[...] = _modulate(y, shift_ref, scale_ref).astype(o_ref.dtype)

    @pl.when(i < n_prompt_tiles)
    def _():
        run(xp_ref, shp_ref, scp_ref)

    @pl.when(i >= n_prompt_tiles)
    def _():
        run(xs_ref, shs_ref, scs_ref)


def _group_specs(tm, d, n_prompt_tiles, prompt_rpb, sample_rpb, n_sample_batches):
    tiles_per_prompt_batch = prompt_rpb // tm
    nb_s = tm // sample_rpb
    rows_p = pl.BlockSpec((tm, d), lambda i: (jnp.minimum(i, n_prompt_tiles - 1), 0))
    rows_s = pl.BlockSpec((tm, d), lambda i: (jnp.maximum(i - n_prompt_tiles, 0), 0))
    tab_p = pl.BlockSpec(
        (1, 1, d),
        lambda i: (n_sample_batches + jnp.minimum(i, n_prompt_tiles - 1) // tiles_per_prompt_batch, 0, 0))
    tab_s = pl.BlockSpec((nb_s, 1, d), lambda i: (jnp.maximum(i - n_prompt_tiles, 0), 0, 0))
    return rows_p, rows_s, tab_p, tab_s


def _prenorm(x_p, x_s, g, shift, scale, prompt_rpb, sample_rpb):
    mp, d = x_p.shape
    ms = x_s.shape[0]
    tm = NORM_TM
    npt = mp // tm
    rows_p, rows_s, tab_p, tab_s = _group_specs(tm, d, npt, prompt_rpb, sample_rpb, ms // sample_rpb)
    return pl.pallas_call(
        functools.partial(_prenorm_kernel, n_prompt_tiles=npt),
        out_shape=jax.ShapeDtypeStruct((mp + ms, d), BF16),
        grid=((mp + ms) // tm,),
        in_specs=[rows_p, rows_s, pl.BlockSpec((1, d), lambda i: (0, 0)), tab_p, tab_p, tab_s, tab_s],
        out_specs=pl.BlockSpec((tm, d), lambda i: (i, 0)),
        compiler_params=_params(1),
        name="prenorm",
    )(x_p, x_s, g.reshape(1, d), shift, scale, shift, scale)


def _dualnorm_kernel(xp_ref, xs_ref, gkv_ref, gb_ref, shp_ref, scp_ref, shs_ref, scs_ref,
                     okv_ref, ob_ref, *, n_prompt_tiles):
    i = pl.program_id(0)

    def run(x_ref, shift_ref, scale_ref):
        x = x_ref[...]
        y = x * _rms_scale(x)
        okv_ref[...] = (y * gkv_ref[...]).astype(okv_ref.dtype)
        ob_ref[...] = _modulate(y * gb_ref[...], shift_ref, scale_ref).astype(ob_ref.dtype)

    @pl.when(i < n_prompt_tiles)
    def _():
        run(xp_ref, shp_ref, scp_ref)

    @pl.when(i >= n_prompt_tiles)
    def _():
        run(xs_ref, shs_ref, scs_ref)


def _dualnorm(x_p, x_s, g_kv, g_b, shift, scale, prompt_rpb, sample_rpb):
    mp, d = x_p.shape
    ms = x_s.shape[0]
    tm = NORM_TM
    npt = mp // tm
    rows_p, rows_s, tab_p, tab_s = _group_specs(tm, d, npt, prompt_rpb, sample_rpb, ms // sample_rpb)
    g_spec = pl.BlockSpec((1, d), lambda i: (0, 0))
    o_spec = pl.BlockSpec((tm, d), lambda i: (i, 0))
    return pl.pallas_call(
        functools.partial(_dualnorm_kernel, n_prompt_tiles=npt),
        out_shape=(jax.ShapeDtypeStruct((mp + ms, d), BF16),) * 2,
        grid=((mp + ms) // tm,),
        in_specs=[rows_p, rows_s, g_spec, g_spec, tab_p, tab_p, tab_s, tab_s],
        out_specs=(o_spec, o_spec),
        compiler_params=_params(1),
        name="dualnorm",
    )(x_p, x_s, g_kv.reshape(1, d), g_b.reshape(1, d), shift, scale, shift, scale)


def _finalnorm_kernel(x_ref, g_ref, o_ref):
    x = x_ref[...]
    o_ref[...] = x * _rms_scale(x) * g_ref[...]


def _finalnorm(x, g):
    m, d = x.shape
    tm = NORM_TM
    return pl.pallas_call(
        _finalnorm_kernel,
        out_shape=jax.ShapeDtypeStruct((m, d), F32),
        grid=(m // tm,),
        in_specs=[pl.BlockSpec((tm, d), lambda i: (i, 0)), pl.BlockSpec((1, d), lambda i: (0, 0))],
        out_specs=pl.BlockSpec((tm, d), lambda i: (i, 0)),
        compiler_params=_params(1),
        name="finalnorm",
    )(x, g.reshape(1, d))


def _stage_weight_tile(w_hbm, stage_ref, wbf_ref, sem, col0):
    j, nj = pl.program_id(0), pl.num_programs(0)
    tn = wbf_ref.shape[1]

    def tile_copy(jj):
        return pltpu.make_async_copy(w_hbm.at[:, pl.ds(col0 + jj * tn, tn)], stage_ref, sem)

    @pl.when(pl.program_id(1) == 0)
    def _():
        @pl.when(j == 0)
        def _():
            tile_copy(0).start()

        tile_copy(j).wait()
        wbf_ref[...] = stage_ref[...].astype(BF16)

        @pl.when(j + 1 < nj)
        def _():
            tile_copy(j + 1).start()


def _weight_scratch(k, tn):
    return [pltpu.VMEM((k, tn), F32), pltpu.VMEM((k, tn), BF16), pltpu.SemaphoreType.DMA(())]


def _mm_kernel(h_ref, w_hbm, *refs, silu, scale, col0):
    *o_refs, stage_ref, wbf_ref, sem = refs
    _stage_weight_tile(w_hbm, stage_ref, wbf_ref, sem, col0)
    acc = _dot(h_ref[...], wbf_ref[...])
    if silu:
        acc = _silu(acc)
    if scale is not None:
        acc = acc * scale
    for o_ref in o_refs:
        o_ref[...] = acc.astype(o_ref.dtype)


def _mm(h, w, col0, n_out, out_dtypes, *, silu=False, scale=None, row0=0, rows=None):
    k = h.shape[1]
    rows = h.shape[0] if rows is None else rows
    tm, tn = MM_TM, MM_TN
    i0 = row0 // tm
    o_spec = pl.BlockSpec((tm, tn), lambda j, i: (i, j))
    return pl.pallas_call(
        functools.partial(_mm_kernel, silu=silu, scale=scale, col0=col0),
        out_shape=tuple(jax.ShapeDtypeStruct((rows, n_out), dt) for dt in out_dtypes),
        grid=(n_out // tn, rows // tm),
        in_specs=[pl.BlockSpec((tm, k), lambda j, i: (i0 + i, 0)),
                  pl.BlockSpec(memory_space=pl.ANY)],
        out_specs=tuple(o_spec for _ in out_dtypes),
        scratch_shapes=_weight_scratch(k, tn),
        compiler_params=_params(2),
        name="proj",
    )(h, w)


def _mm_resid_kernel(u_ref, w_hbm, x_ref, gate_ref, o_ref, stage_ref, wbf_ref, sem):
    _stage_weight_tile(w_hbm, stage_ref, wbf_ref, sem, 0)
    acc = _dot(u_ref[...], wbf_ref[...])
    nb = gate_ref.shape[0]
    tm, tn = acc.shape
    y = acc.reshape(nb, tm // nb, tn) * gate_ref[...]
    o_ref[...] = x_ref[...] + y.reshape(tm, tn)


def _mm_resid(u, w, x, gate, rows_per_batch, table_row0):
    m, k = u.shape
    n = w.shape[1]
    tm, tn = RESID_TM, MM_TN
    if rows_per_batch >= tm:
        nb = 1
        tiles_per_batch = rows_per_batch // tm
        gate_map = lambda j, i: (table_row0 + i // tiles_per_batch, 0, j)
    else:
        nb = tm // rows_per_batch
        blk0 = table_row0 // nb
        gate_map = lambda j, i: (blk0 + i, 0, j)
    return pl.pallas_call(
        _mm_resid_kernel,
        out_shape=jax.ShapeDtypeStruct((m, n), F32),
        grid=(n // tn, m // tm),
        in_specs=[pl.BlockSpec((tm, k), lambda j, i: (i, 0)),
                  pl.BlockSpec(memory_space=pl.ANY),
                  pl.BlockSpec((tm, tn), lambda j, i: (i, j)),
                  pl.BlockSpec((nb, 1, tn), gate_map)],
        out_specs=pl.BlockSpec((tm, tn), lambda j, i: (i, j)),
        scratch_shapes=_weight_scratch(k, tn),
        compiler_params=_params(2),
        name="out_proj",
    )(u, w, x, gate)


def _softplus(z):
    return jnp.maximum(z, 0.0) + jnp.log(1.0 + jnp.exp(-jnp.abs(z)))


def _suffix_sum(x, tri):
    hi = x.astype(BF16)
    lo = (x - hi.astype(F32)).astype(BF16)
    return _dot(hi, tri) + _dot(lo, tri)


def _stick_blocks(qs, kblks, vblks, tri, carries, accs, causal):
    zs = [_dot_nt(q, kblk) for q, kblk in zip(qs, kblks)]
    sps = [_softplus(z) if causal is None else jnp.where(causal, _softplus(z), 0.0) for z in zs]
    incls = [_suffix_sum(sp, tri) for sp in sps]
    ws = [jnp.exp(z - incl + carry) for z, incl, carry in zip(zs, incls, carries)]
    if causal is not None:
        ws = [jnp.where(causal, w, 0.0) for w in ws]
    accs = [acc + _dot(w.astype(BF16), vblk) for acc, w, vblk in zip(accs, ws, vblks)]
    carries = [carry - incl[:, :1] for carry, incl in zip(carries, incls)]
    return carries, accs


def _strict_lower(tq, tk):
    return lax.broadcasted_iota(jnp.int32, (tq, tk), 1) < lax.broadcasted_iota(jnp.int32, (tq, tk), 0)


def _all_weights_vanish(carry):
    return jnp.max(carry) <= F32_EXP_UNDERFLOW_LOG


def _stick_prompt_kernel(q_ref, k_ref, v_ref, sg_ref, tri_ref, o_ref, *, tk):
    qi = pl.program_id(2)
    tq = q_ref.shape[1]
    n_sub = tq // tk
    heads = range(q_ref.shape[2] // HEAD_DIM)
    lanes = [slice(hh * HEAD_DIM, (hh + 1) * HEAD_DIM) for hh in heads]
    qs = [q_ref[0, :, lanes[hh]] for hh in heads]
    tri = tri_ref[...]

    def kv(k0):
        k0 = pl.multiple_of(k0, tk)
        return ([k_ref[0, pl.ds(k0, tk), lanes[hh]] for hh in heads],
                [v_ref[0, pl.ds(k0, tk), lanes[hh]] for hh in heads])

    carries = [jnp.zeros((tq, 1), F32) for _ in heads]
    accs = [jnp.zeros((tq, HEAD_DIM), F32) for _ in heads]
    for sb in reversed(range(n_sub)):
        r0 = sb * tk
        kblks, vblks = kv(qi * tq + r0)
        c_sub, a_sub = _stick_blocks([q[r0:] for q in qs], kblks, vblks, tri, [c[r0:] for c in carries],
                                     [a[r0:] for a in accs], _strict_lower(tq - r0, tk))
        if r0:
            c_sub = [jnp.concatenate([c[:r0], cs], axis=0) for c, cs in zip(carries, c_sub)]
            a_sub = [jnp.concatenate([a[:r0], as_], axis=0) for a, as_ in zip(accs, a_sub)]
        carries, accs = c_sub, a_sub

    def cond(state):
        it, carries, _ = state
        return jnp.logical_and(
            it < qi * n_sub, jnp.logical_not(_all_weights_vanish(functools.reduce(jnp.maximum, carries))))

    def body(state):
        it, carries, accs = state
        kblks, vblks = kv((qi * n_sub - 1 - it) * tk)
        carries, accs = _stick_blocks(qs, kblks, vblks, tri, carries, accs, None)
        return it + 1, tuple(carries), tuple(accs)

    _, carries, accs = lax.while_loop(cond, body, (jnp.int32(0), tuple(carries), tuple(accs)))
    for hh in heads:
        o_ref[0, :, lanes[hh]] = (accs[hh] * sg_ref[0, :, lanes[hh]].astype(F32)).astype(o_ref.dtype)


def _tri_incl(t):
    return (jnp.arange(t)[:, None] >= jnp.arange(t)[None, :]).astype(BF16)


def _stick_prompt(q, k, v, sg, batch, seq):
    d = k.shape[1]
    tq, tk = STICK_TQ, ATTN_T
    tiles = seq // tq
    hw = STICK_HEADS * HEAD_DIM
    flat = lambda a: a.reshape(1, a.shape[0], d)
    blk_in = pl.BlockSpec((1, tq, hw), lambda b, h, i: (0, b * tiles + i, h))
    kv = pl.BlockSpec((1, seq, hw), lambda b, h, i: (0, b, h))
    out = pl.pallas_call(
        functools.partial(_stick_prompt_kernel, tk=tk),
        out_shape=jax.ShapeDtypeStruct((batch, seq, d), BF16),
        grid=(batch, d // hw, tiles),
        in_specs=[blk_in, kv, kv, blk_in, pl.BlockSpec((tk, tk), lambda b, h, i: (0, 0))],
        out_specs=pl.BlockSpec((1, tq, hw), lambda b, h, i: (b, i, h)),
        compiler_params=_params(3),
        name="stick_prompt",
    )(flat(q), flat(k), flat(v), flat(sg), _tri_incl(tk))
    return out.reshape(batch * seq, d)


def _head_rows(cache_ref, hh, t0, n):
    nh, hd = cache_ref.shape[-2:]
    flat = cache_ref.reshape(math.prod(cache_ref.shape[:-1]), hd)
    return flat[pl.ds(t0 * nh + hh, n, stride=nh), :]


def _stick_sample_kernel(q_ref, ck_hbm, cv_hbm, k_ref, v_ref, sg_ref, tri_ref, o_ref, kbuf, vbuf, sem):
    bi, g = pl.program_id(0), pl.program_id(1)
    t_new = q_ref.shape[1]
    past = ck_hbm.shape[1]
    n_slots, kblock, nh, _ = kbuf.shape
    ahead = n_slots - 1
    n_blk = past // kblock
    heads = range(nh)
    lanes = [slice(hh * HEAD_DIM, (hh + 1) * HEAD_DIM) for hh in heads]
    tri = tri_ref[...]
    qs = [q_ref[0, :, lanes[hh]] for hh in heads]

    def block_copies(blk):
        slot = blk % n_slots
        rows = pl.ds(past - (blk + 1) * kblock, kblock)
        group = pl.ds(g * nh, nh)
        return (pltpu.make_async_copy(ck_hbm.at[bi, rows, group, :], kbuf.at[slot], sem.at[0, slot]),
                pltpu.make_async_copy(cv_hbm.at[bi, rows, group, :], vbuf.at[slot], sem.at[1, slot]))

    for blk in range(min(ahead, n_blk)):
        for c in block_copies(blk):
            c.start()

    carries, accs = _stick_blocks(
        qs, [k_ref[0, :, lanes[hh]].astype(BF16) for hh in heads],
        [v_ref[0, :, lanes[hh]].astype(BF16) for hh in heads], tri_ref[:t_new, :t_new],
        [jnp.zeros((t_new, 1), F32) for _ in heads], [jnp.zeros((t_new, HEAD_DIM), F32) for _ in heads],
        _strict_lower(t_new, t_new))

    def cond(state):
        it, carries, _ = state
        return jnp.logical_and(it < n_blk,
                               jnp.logical_not(_all_weights_vanish(functools.reduce(jnp.maximum, carries))))

    def body(state):
        it, carries, accs = state
        for c in block_copies(it):
            c.wait()

        @pl.when(it + ahead < n_blk)
        def _():
            for c in block_copies(it + ahead):
                c.start()

        t0 = (it % n_slots) * kblock
        carries, accs = _stick_blocks(
            qs, [_head_rows(kbuf, hh, t0, kblock).astype(BF16) for hh in heads],
            [_head_rows(vbuf, hh, t0, kblock).astype(BF16) for hh in heads], tri, carries, accs, None)
        return it + 1, tuple(carries), tuple(accs)

    done, carries, accs = lax.while_loop(cond, body, (jnp.int32(0), tuple(carries), tuple(accs)))

    for pending in range(ahead):
        @pl.when(done + pending < n_blk)
        def _():
            for c in block_copies(done + pending):
                c.wait()

    for hh in heads:
        o_ref[0, :, lanes[hh]] = (accs[hh] * sg_ref[0, :, lanes[hh]].astype(F32)).astype(o_ref.dtype)


def _stick_sample(q, cache_k, cache_v, k, v, sg, row0):
    b, past, n_heads, _ = cache_k.shape
    d = n_heads * HEAD_DIM
    t_new = k.shape[0] // b
    hw = SAMPLE_HEADS * HEAD_DIM
    t = ATTN_T
    blk0 = row0 // t_new
    q3 = q.reshape(q.shape[0] // t_new, t_new, d)
    sg3 = sg.reshape(q3.shape)
    merged = pl.BlockSpec((1, t_new, hw), lambda bi, g: (blk0 + bi, 0, g))
    new = pl.BlockSpec((1, t_new, hw), lambda bi, g: (bi, 0, g))
    cache = pl.BlockSpec(memory_space=pl.ANY)
    block_buf = pltpu.VMEM((CACHE_SLOTS, t, SAMPLE_HEADS, HEAD_DIM), F32)
    out = pl.pallas_call(
        _stick_sample_kernel,
        out_shape=jax.ShapeDtypeStruct((b, t_new, d), BF16),
        grid=(b, d // hw),
        in_specs=[merged, cache, cache, new, new, merged, pl.BlockSpec((t, t), lambda bi, g: (0, 0))],
        out_specs=new,
        scratch_shapes=[block_buf, block_buf, pltpu.SemaphoreType.DMA((2, CACHE_SLOTS))],
        compiler_params=_params(2),
        name="stick_sample",
    )(q3, cache_k, cache_v, k.reshape(b, t_new, d), v.reshape(b, t_new, d), sg3, _tri_incl(t))
    return out.reshape(b * t_new, d)


def _band_prompt_kernel(q_ref, sg_ref, bias_ref, *refs):
    qi = pl.program_id(2)
    t = q_ref.shape[1]
    n_kb = bias_ref.shape[2] // t
    k_refs, v_refs, o_ref = refs[:n_kb], refs[n_kb:2 * n_kb], refs[2 * n_kb]
    heads = range(q_ref.shape[2] // HEAD_DIM)
    lanes = [slice(hh * HEAD_DIM, (hh + 1) * HEAD_DIM) for hh in heads]
    exists = [qi - (n_kb - 1) + dd >= 0 for dd in range(n_kb)]
    scores = [[jnp.where(exists[dd], _dot_nt(q_ref[0, :, lanes[hh]], k_refs[dd][0, :, lanes[hh]])
                         + bias_ref[hh, :, dd * t:(dd + 1) * t], NEG_INF)
               for dd in range(n_kb)] for hh in heads]
    maxes = [functools.reduce(jnp.maximum, [s.max(axis=-1, keepdims=True) for s in scores[hh]]) for hh in heads]
    probs = [[jnp.exp(s - maxes[hh]) for s in scores[hh]] for hh in heads]
    accs = [sum(_dot(p.astype(BF16), v_refs[dd][0, :, lanes[hh]]) for dd, p in enumerate(probs[hh]))
            for hh in heads]
    for hh in heads:
        denom = sum(p.sum(axis=-1, keepdims=True) for p in probs[hh])
        o_ref[0, :, lanes[hh]] = (accs[hh] / denom * sg_ref[0, :, lanes[hh]].astype(F32)).astype(o_ref.dtype)


def _toeplitz(values_at, rows, cols):
    period = rows + cols
    j = jnp.arange(period)
    v = values_at(jnp.where(j < cols, j, j - period))
    flat = jnp.tile(v, (1,) * (v.ndim - 1) + (rows,))[..., :rows * (period - 1)]
    return flat.reshape(v.shape[:-1] + (rows, period - 1))[..., :cols]


def _band_bias_prompt(rel_bias, t):
    n_kb = (LEFT_CHUNKS * CHUNK) // t + 1
    back = (n_kb - 1) * t
    rb = rel_bias.astype(F32)
    bias = _toeplitz(lambda m: rb[:, jnp.clip(back - m, -REL_CLIP, REL_CLIP) + REL_CLIP], t, n_kb * t)
    r = jnp.arange(t)[:, None]
    c = jnp.arange(n_kb * t)[None, :]
    chunk_diff = (back + r) // CHUNK - c // CHUNK
    visible = (chunk_diff >= 0) & (chunk_diff <= LEFT_CHUNKS)
    return jnp.where(visible[None], bias, NEG_INF)


def _band_prompt(q, k, v, sg, rel_bias, batch, seq):
    d = k.shape[1]
    t = ATTN_T
    tiles = seq // t
    hw = BAND_HEADS * HEAD_DIM
    bias = _band_bias_prompt(rel_bias, t)
    n_kb = bias.shape[2] // t
    blk_in = pl.BlockSpec((1, t, hw), lambda b, h, i: (0, b * tiles + i, h))
    kv = [pl.BlockSpec((1, t, hw), functools.partial(
        lambda b, h, i, back: (0, b * tiles + jnp.maximum(i - back, 0), h), back=n_kb - 1 - dd))
        for dd in range(n_kb)]
    flat = lambda a: a.reshape(1, a.shape[0], d)
    out = pl.pallas_call(
        _band_prompt_kernel,
        out_shape=jax.ShapeDtypeStruct((batch, seq, d), BF16),
        grid=(batch, d // hw, tiles),
        in_specs=[blk_in, blk_in, pl.BlockSpec((BAND_HEADS,) + bias.shape[1:], lambda b, h, i: (h, 0, 0))]
        + kv + kv,
        out_specs=pl.BlockSpec((1, t, hw), lambda b, h, i: (b, i, h)),
        compiler_params=_params(3),
        name="band_prompt",
    )(flat(q), flat(sg), bias, *([flat(k)] * n_kb), *([flat(v)] * n_kb))
    return out.reshape(batch * seq, d)


def _band_sample_kernel(q_ref, ck_ref, cv_ref, k_ref, v_ref, sg_ref, bias_ref, o_ref):
    past = ck_ref.shape[1]
    heads = range(ck_ref.shape[2])
    lanes = [slice(hh * HEAD_DIM, (hh + 1) * HEAD_DIM) for hh in heads]
    s_old = [_dot_nt(q_ref[0, :, lanes[hh]], _head_rows(ck_ref, hh, 0, past).astype(BF16)) + bias_ref[hh, :, :past]
             for hh in heads]
    s_new = [_dot_nt(q_ref[0, :, lanes[hh]], k_ref[0, :, lanes[hh]].astype(BF16)) + bias_ref[hh, :, past:]
             for hh in heads]
    m = [jnp.maximum(a.max(axis=-1, keepdims=True), b.max(axis=-1, keepdims=True)) for a, b in zip(s_old, s_new)]
    p_old = [jnp.exp(s - mm) for s, mm in zip(s_old, m)]
    p_new = [jnp.exp(s - mm) for s, mm in zip(s_new, m)]
    accs = [_dot(p_old[hh].astype(BF16), _head_rows(cv_ref, hh, 0, past).astype(BF16))
            + _dot(p_new[hh].astype(BF16), v_ref[0, :, lanes[hh]].astype(BF16)) for hh in heads]
    for hh in heads:
        denom = p_old[hh].sum(axis=-1, keepdims=True) + p_new[hh].sum(axis=-1, keepdims=True)
        o_ref[0, :, lanes[hh]] = (accs[hh] / denom * sg_ref[0, :, lanes[hh]].astype(F32)).astype(o_ref.dtype)


def _band_bias_sample(rel_bias, past_total, past_b, t_new):
    rb = rel_bias.astype(F32)
    bias = _toeplitz(lambda m: rb[:, jnp.clip(past_b - m, -REL_CLIP, REL_CLIP) + REL_CLIP],
                     t_new, past_b + t_new)
    q_pos = past_total + jnp.arange(t_new)
    k_pos = past_total - past_b + jnp.arange(past_b + t_new)
    q_chunk = q_pos // CHUNK
    k_chunk = k_pos // CHUNK
    visible = ((k_pos[None, :] >= 0) & (k_chunk[None, :] <= q_chunk[:, None])
               & (k_chunk[None, :] >= q_chunk[:, None] - LEFT_CHUNKS))
    return jnp.where(visible[None], bias, NEG_INF)


def _band_sample(q, cache_k, cache_v, k, v, sg, rel_bias, row0, past_total):
    b, past_b, n_heads, _ = cache_k.shape
    d = n_heads * HEAD_DIM
    t_new = (q.shape[0] - row0) // b
    hw = SAMPLE_HEADS * HEAD_DIM
    blk0 = row0 // t_new
    bias = _band_bias_sample(rel_bias, past_total, past_b, t_new)
    r3 = lambda a: a.reshape(a.shape[0] // t_new, t_new, d)
    merged = pl.BlockSpec((1, t_new, hw), lambda bi, g: (blk0 + bi, 0, g))
    cache = pl.BlockSpec((1, past_b, SAMPLE_HEADS, HEAD_DIM), lambda bi, g: (bi, 0, g, 0))
    out = pl.pallas_call(
        _band_sample_kernel,
        out_shape=jax.ShapeDtypeStruct((b, t_new, d), BF16),
        grid=(b, d // hw),
        in_specs=[merged, cache, cache, merged, merged, merged,
                  pl.BlockSpec((SAMPLE_HEADS,) + bias.shape[1:], lambda bi, g: (g, 0, 0))],
        out_specs=pl.BlockSpec((1, t_new, hw), lambda bi, g: (bi, 0, g)),
        compiler_params=_params(2),
        name="band_sample",
    )(r3(q), cache_k, cache_v, r3(k), r3(v), r3(sg), bias)
    return out.reshape(b * t_new, d)


def kernel(x_prompt, x_sample, c_prompt, c_sample, cache_a_k, cache_a_v, cache_b_k, cache_b_v, w_mod_a, b_mod_a, g_norm_a, w_in_a, w_out_a, g_kv, w_kv, w_mod_b, b_mod_b, g_norm_b, w_in_b, rel_bias_b, w_out_b, g_final):
    bp, seq, d = x_prompt.shape
    bs, t_new, _ = x_sample.shape
    n_heads = d // HEAD_DIM
    past = cache_a_k.shape[2]
    past_b = cache_b_k.shape[1]
    mp, ms = bp * seq, bs * t_new
    assert w_mod_a.shape[0] == 1 and w_mod_b.shape[0] == 1, "one layer of each mixer"
    assert seq % MM_TM == 0 and mp % MM_TM == 0 and ms % MM_TM == 0 and MM_TM % t_new == 0
    assert seq % RESID_TM == 0 and ms % RESID_TM == 0 and RESID_TM % t_new == 0 and d % MM_TN == 0
    assert bs % (RESID_TM // t_new) == 0 and bs % (NORM_TM // t_new) == 0
    assert seq % STICK_TQ == 0 and STICK_TQ % ATTN_T == 0
    assert seq % ATTN_T == 0 and past % ATTN_T == 0 and (LEFT_CHUNKS * CHUNK) % ATTN_T == 0

    xp = x_prompt.reshape(mp, d)
    xs = x_sample.reshape(ms, d)

    n_c = bs + bp
    pad = -n_c % BF16_SUBLANES
    c_all = jnp.concatenate([c_sample, c_prompt, jnp.zeros((pad, d), F32)], axis=0)
    mod_a, mod_b = _adaln_tables(c_all, w_mod_a[0], b_mod_a[0], w_mod_b[0], b_mod_b[0])
    shift_a, scale_a, gate_a = _split_mod(mod_a, d)
    shift_b, scale_b, gate_b = _split_mod(mod_b, d)

    h_a = _prenorm(xp, xs, g_norm_a[0], shift_a, scale_a, seq, t_new)
    w_in = w_in_a[0]
    q_scale = 1.0 / math.sqrt(HEAD_DIM)
    q_a, = _mm(h_a, w_in, 0, d, (BF16,), scale=q_scale)
    k_p, k_p16 = _mm(h_a, w_in, d, d, (F32, BF16), row0=0, rows=mp)
    k_s, = _mm(h_a, w_in, d, d, (F32,), row0=mp, rows=ms)
    v_p, v_p16 = _mm(h_a, w_in, 2 * d, d, (F32, BF16), row0=0, rows=mp)
    v_s, = _mm(h_a, w_in, 2 * d, d, (F32,), row0=mp, rows=ms)
    sg_a, = _mm(h_a, w_in, 3 * d, d, (BF16,), silu=True)

    u_p = _stick_prompt(q_a, k_p16, v_p16, sg_a, bp, seq)
    u_s = _stick_sample(q_a, cache_a_k[0], cache_a_v[0], k_s, v_s, sg_a, mp)
    x1_p = _mm_resid(u_p, w_out_a[0], xp, gate_a, seq, bs)
    x1_s = _mm_resid(u_s, w_out_a[0], xs, gate_a, t_new, 0)

    h_kv, h_b = _dualnorm(x1_p, x1_s, g_kv, g_norm_b[0], shift_b, scale_b, seq, t_new)
    kb, kb16 = _mm(h_kv, w_kv, 0, d, (F32, BF16))
    vb, vb16 = _mm(h_kv, w_kv, d, d, (F32, BF16))
    q_b, = _mm(h_b, w_in_b[0], 0, d, (BF16,), scale=q_scale)
    sg_b, = _mm(h_b, w_in_b[0], d, d, (BF16,), silu=True)

    ub_p = _band_prompt(q_b, kb16, vb16, sg_b, rel_bias_b[0], bp, seq)
    ub_s = _band_sample(q_b, cache_b_k, cache_b_v, kb16, vb16, sg_b, rel_bias_b[0], mp, past)
    x2_p = _mm_resid(ub_p, w_out_b[0], x1_p, gate_b, seq, bs)
    x2_s = _mm_resid(ub_s, w_out_b[0], x1_s, gate_b, t_new, 0)

    y_p = _finalnorm(x2_p, g_final).reshape(bp, seq, d)
    y_s = _finalnorm(x2_s, g_final).reshape(bs, t_new, d)

    keep_b = min(LEFT_CHUNKS * CHUNK, seq)
    heads = lambda a, b, t: a.reshape(b, t, n_heads, HEAD_DIM)
    tail = lambda a: jnp.stack([a[(b + 1) * seq - keep_b:(b + 1) * seq] for b in range(bp)])
    return (y_p, y_s,
            heads(k_p, bp, seq)[None], heads(v_p, bp, seq)[None],
            heads(k_s, bs, t_new)[None], heads(v_s, bs, t_new)[None],
            heads(tail(kb), bp, keep_b), heads(tail(vb), bp, keep_b),
            heads(kb[mp:], bs, t_new), heads(vb[mp:], bs, t_new))
```

```python
import functools
import math

import jax
import jax.numpy as jnp
from jax import lax
from jax.experimental import pallas as pl
from jax.experimental.pallas import tpu as pltpu

F32 = jnp.float32
BF16 = jnp.bfloat16

HEAD_DIM = 128
CHUNK = 64
LEFT_CHUNKS = 8
REL_CLIP = 128
EPS = 1e-6
NEG_INF = -1e30

V7X_VMEM_LIMIT_BYTES = 54 * 1024 * 1024
LANES = 128
BF16_SUBLANES = 16

MM_TM = 512
MM_TM_NARROW_OUT = 1024
RESID_TM = 512
MM_TN = 1024
NORM_TM = 256
MOD_TN = 512
ATTN_T = 256
STICK_TQ = 256
STICK_HEADS = 8
BAND_HEADS = 4

F32_EXP_UNDERFLOW_LOG = -104.0
SAMPLE_HEADS = 8
CACHE_SLOTS = 4


def _params(n_grid):
    return pltpu.CompilerParams(
        dimension_semantics=("arbitrary",) * n_grid,
        vmem_limit_bytes=V7X_VMEM_LIMIT_BYTES)


def _silu(x):
    return x / (1.0 + jnp.exp(-x))


def _dot(a, b):
    return jnp.dot(a, b, preferred_element_type=F32)


def _dot_nt(a, b):
    return lax.dot_general(a, b, (((1,), (1,)), ((), ())), preferred_element_type=F32)


def _mod_kernel(c_ref, wa_ref, ba_ref, wb_ref, bb_ref, oa_ref, ob_ref):
    a = _silu(c_ref[...]).astype(BF16)
    oa_ref[...] = _dot(a, wa_ref[...].astype(BF16)) + ba_ref[...]
    ob_ref[...] = _dot(a, wb_ref[...].astype(BF16)) + bb_ref[...]


def _adaln_tables(c_all, w_a, b_a, w_b, b_b):
    rows, d = c_all.shape
    n = w_a.shape[1]
    w_spec = pl.BlockSpec((d, MOD_TN), lambda j: (0, j))
    b_spec = pl.BlockSpec((1, MOD_TN), lambda j: (0, j))
    o_spec = pl.BlockSpec((rows, MOD_TN), lambda j: (0, j))
    return pl.pallas_call(
        _mod_kernel,
        out_shape=(jax.ShapeDtypeStruct((rows, n), F32),) * 2,
        grid=(n // MOD_TN,),
        in_specs=[pl.BlockSpec((rows, d), lambda j: (0, 0)), w_spec, b_spec, w_spec, b_spec],
        out_specs=(o_spec, o_spec),
        compiler_params=_params(1),
        name="adaln_mod",
    )(c_all, w_a, b_a.reshape(1, n), w_b, b_b.reshape(1, n))


def _split_mod(mod, d):
    rows = mod.shape[0]
    return tuple(mod[:, k * d:(k + 1) * d].reshape(rows, 1, d) for k in range(3))


def _rms_scale(x):
    return lax.rsqrt(jnp.mean(x * x, axis=-1, keepdims=True) + EPS)


def _modulate(y, shift_ref, scale_ref):
    nb = shift_ref.shape[0]
    tm, d = y.shape
    y3 = y.reshape(nb, tm // nb, d)
    return (y3 * (1.0 + scale_ref[...]) + shift_ref[...]).reshape(tm, d)


def _prenorm_kernel(xp_ref, xs_ref, g_ref, shp_ref, scp_ref, shs_ref, scs_ref, o_ref, *, n_prompt_tiles):
    i = pl.program_id(0)

    def run(x_ref, shift_ref, scale_ref):
        x = x_ref[...]
        y = x * _rms_scale(x) * g_ref[...]
        o_ref[...] = _modulate(y, shift_ref, scale_ref).astype(o_ref.dtype)

    @pl.when(i < n_prompt_tiles)
    def _():
        run(xp_ref, shp_ref, scp_ref)

    @pl.when(i >= n_prompt_tiles)
    def _():
        run(xs_ref, shs_ref, scs_ref)


def _group_specs(tm, d, n_prompt_tiles, prompt_rpb, sample_rpb, n_sample_batches):
    tiles_per_prompt_batch = prompt_rpb // tm
    nb_s = tm // sample_rpb
    rows_p = pl.BlockSpec((tm, d), lambda i: (jnp.minimum(i, n_prompt_tiles - 1), 0))
    rows_s = pl.BlockSpec((tm, d), lambda i: (jnp.maximum(i - n_prompt_tiles, 0), 0))
    tab_p = pl.BlockSpec(
        (1, 1, d),
        lambda i: (n_sample_batches + jnp.minimum(i, n_prompt_tiles - 1) // tiles_per_prompt_batch, 0, 0))
    tab_s = pl.BlockSpec((nb_s, 1, d), lambda i: (jnp.maximum(i - n_prompt_tiles, 0), 0, 0))
    return rows_p, rows_s, tab_p, tab_s


def _prenorm(x_p, x_s, g, shift, scale, prompt_rpb, sample_rpb):
    mp, d = x_p.shape
    ms = x_s.shape[0]
    tm = NORM_TM
    npt = mp // tm
    rows_p, rows_s, tab_p, tab_s = _group_specs(tm, d, npt, prompt_rpb, sample_rpb, ms // sample_rpb)
    return pl.pallas_call(
        functools.partial(_prenorm_kernel, n_prompt_tiles=npt),
        out_shape=jax.ShapeDtypeStruct((mp + ms, d), BF16),
        grid=((mp + ms) // tm,),
        in_specs=[rows_p, rows_s, pl.BlockSpec((1, d), lambda i: (0, 0)), tab_p, tab_p, tab_s, tab_s],
        out_specs=pl.BlockSpec((tm, d), lambda i: (i, 0)),
        compiler_params=_params(1),
        name="prenorm",
    )(x_p, x_s, g.reshape(1, d), shift, scale, shift, scale)


def _dualnorm_kernel(xp_ref, xs_ref, gkv_ref, gb_ref, shp_ref, scp_ref, shs_ref, scs_ref,
                     okv_ref, ob_ref, *, n_prompt_tiles):
    i = pl.program_id(0)

    def run(x_ref, shift_ref, scale_ref):
        x = x_ref[...]
        y = x * _rms_scale(x)
        okv_ref[...] = (y * gkv_ref[...]).astype(okv_ref.dtype)
        ob_ref[...] = _modulate(y * gb_ref[...], shift_ref, scale_ref).astype(ob_ref.dtype)

    @pl.when(i < n_prompt_tiles)
    def _():
        run(xp_ref, shp_ref, scp_ref)

    @pl.when(i >= n_prompt_tiles)
    def _():
        run(xs_ref, shs_ref, scs_ref)


def _dualnorm(x_p, x_s, g_kv, g_b, shift, scale, prompt_rpb, sample_rpb):
    mp, d = x_p.shape
    ms = x_s.shape[0]
    tm = NORM_TM
    npt = mp // tm
    rows_p, rows_s, tab_p, tab_s = _group_specs(tm, d, npt, prompt_rpb, sample_rpb, ms // sample_rpb)
    g_spec = pl.BlockSpec((1, d), lambda i: (0, 0))
    o_spec = pl.BlockSpec((tm, d), lambda i: (i, 0))
    return pl.pallas_call(
        functools.partial(_dualnorm_kernel, n_prompt_tiles=npt),
        out_shape=(jax.ShapeDtypeStruct((mp + ms, d), BF16),) * 2,
        grid=((mp + ms) // tm,),
        in_specs=[rows_p, rows_s, g_spec, g_spec, tab_p, tab_p, tab_s, tab_s],
        out_specs=(o_spec, o_spec),
        compiler_params=_params(1),
        name="dualnorm",
    )(x_p, x_s, g_kv.reshape(1, d), g_b.reshape(1, d), shift, scale, shift, scale)


def _finalnorm_kernel(x_ref, g_ref, o_ref):
    x = x_ref[...]
    o_ref[...] = x * _rms_scale(x) * g_ref[...]


def _finalnorm(x, g):
    m, d = x.shape
    tm = NORM_TM
    return pl.pallas_call(
        _finalnorm_kernel,
        out_shape=jax.ShapeDtypeStruct((m, d), F32),
        grid=(m // tm,),
        in_specs=[pl.BlockSpec((tm, d), lambda i: (i, 0)), pl.BlockSpec((1, d), lambda i: (0, 0))],
        out_specs=pl.BlockSpec((tm, d), lambda i: (i, 0)),
        compiler_params=_params(1),
        name="finalnorm",
    )(x, g.reshape(1, d))


def _stage_weight_tile(w_hbm, stage_ref, wbf_ref, sem, col0):
    j, nj = pl.program_id(0), pl.num_programs(0)
    tn = wbf_ref.shape[1]

    def tile_copy(jj):
        return pltpu.make_async_copy(w_hbm.at[:, pl.ds(col0 + jj * tn, tn)], stage_ref, sem)

    @pl.when(pl.program_id(1) == 0)
    def _():
        @pl.when(j == 0)
        def _():
            tile_copy(0).start()

        tile_copy(j).wait()
        wbf_ref[...] = stage_ref[...].astype(BF16)

        @pl.when(j + 1 < nj)
        def _():
            tile_copy(j + 1).start()


def _weight_scratch(k, tn):
    return [pltpu.VMEM((k, tn), F32), pltpu.VMEM((k, tn), BF16), pltpu.SemaphoreType.DMA(())]


def _mm_kernel(h_ref, w_hbm, *refs, silu, scale, col0):
    *o_refs, stage_ref, wbf_ref, sem = refs
    _stage_weight_tile(w_hbm, stage_ref, wbf_ref, sem, col0)
    acc = _dot(h_ref[...], wbf_ref[...])
    if silu:
        acc = _silu(acc)
    if scale is not None:
        acc = acc * scale
    for o_ref in o_refs:
        o_ref[...] = acc.astype(o_ref.dtype)


def _mm(h, w, col0, n_out, out_dtypes, *, silu=False, scale=None, row0=0, rows=None):
    k = h.shape[1]
    rows = h.shape[0] if rows is None else rows
    tm = MM_TM if F32 in out_dtypes else MM_TM_NARROW_OUT
    tn = MM_TN
    i0 = row0 // tm
    o_spec = pl.BlockSpec((tm, tn), lambda j, i: (i, j))
    return pl.pallas_call(
        functools.partial(_mm_kernel, silu=silu, scale=scale, col0=col0),
        out_shape=tuple(jax.ShapeDtypeStruct((rows, n_out), dt) for dt in out_dtypes),
        grid=(n_out // tn, rows // tm),
        in_specs=[pl.BlockSpec((tm, k), lambda j, i: (i0 + i, 0)),
                  pl.BlockSpec(memory_space=pl.ANY)],
        out_specs=tuple(o_spec for _ in out_dtypes),
        scratch_shapes=_weight_scratch(k, tn),
        compiler_params=_params(2),
        name="proj",
    )(h, w)


def _mm_resid_kernel(u_ref, w_hbm, x_ref, gate_ref, o_ref, stage_ref, wbf_ref, sem):
    _stage_weight_tile(w_hbm, stage_ref, wbf_ref, sem, 0)
    acc = _dot(u_ref[...], wbf_ref[...])
    nb = gate_ref.shape[0]
    tm, tn = acc.shape
    y = acc.reshape(nb, tm // nb, tn) * gate_ref[...]
    o_ref[...] = x_ref[...] + y.reshape(tm, tn)


def _mm_resid(u, w, x, gate, rows_per_batch, table_row0):
    m, k = u.shape
    n = w.shape[1]
    tm, tn = RESID_TM, MM_TN
    if rows_per_batch >= tm:
        nb = 1
        tiles_per_batch = rows_per_batch // tm
        gate_map = lambda j, i: (table_row0 + i // tiles_per_batch, 0, j)
    else:
        nb = tm // rows_per_batch
        blk0 = table_row0 // nb
        gate_map = lambda j, i: (blk0 + i, 0, j)
    return pl.pallas_call(
        _mm_resid_kernel,
        out_shape=jax.ShapeDtypeStruct((m, n), F32),
        grid=(n // tn, m // tm),
        in_specs=[pl.BlockSpec((tm, k), lambda j, i: (i, 0)),
                  pl.BlockSpec(memory_space=pl.ANY),
                  pl.BlockSpec((tm, tn), lambda j, i: (i, j)),
                  pl.BlockSpec((nb, 1, tn), gate_map)],
        out_specs=pl.BlockSpec((tm, tn), lambda j, i: (i, j)),
        scratch_shapes=_weight_scratch(k, tn),
        compiler_params=_params(2),
        name="out_proj",
    )(u, w, x, gate)


def _softplus(z):
    return jnp.maximum(z, 0.0) + jnp.log(1.0 + jnp.exp(-jnp.abs(z)))


def _suffix_sum(x, tri):
    hi = x.astype(BF16)
    lo = (x - hi.astype(F32)).astype(BF16)
    return _dot(hi, tri) + _dot(lo, tri)


def _stick_blocks(qs, kblks, vblks, tri, carries, accs, causal):
    zs = [_dot_nt(q, kblk) for q, kblk in zip(qs, kblks)]
    sps = [_softplus(z) if causal is None else jnp.where(causal, _softplus(z), 0.0) for z in zs]
    incls = [_suffix_sum(sp, tri) for sp in sps]
    ws = [jnp.exp(z - incl + carry) for z, incl, carry in zip(zs, incls, carries)]
    if causal is not None:
        ws = [jnp.where(causal, w, 0.0) for w in ws]
    accs = [acc + _dot(w.astype(BF16), vblk) for acc, w, vblk in zip(accs, ws, vblks)]
    carries = [carry - incl[:, :1] for carry, incl in zip(carries, incls)]
    return carries, accs


def _strict_lower(tq, tk):
    return lax.broadcasted_iota(jnp.int32, (tq, tk), 1) < lax.broadcasted_iota(jnp.int32, (tq, tk), 0)


def _all_weights_vanish(carry):
    return jnp.max(carry) <= F32_EXP_UNDERFLOW_LOG


def _stick_prompt_kernel(q_ref, k_ref, v_ref, sg_ref, tri_ref, o_ref, *, tk):
    qi = pl.program_id(2)
    tq = q_ref.shape[1]
    n_sub = tq // tk
    heads = range(q_ref.shape[2] // HEAD_DIM)
    lanes = [slice(hh * HEAD_DIM, (hh + 1) * HEAD_DIM) for hh in heads]
    qs = [q_ref[0, :, lanes[hh]] for hh in heads]
    tri = tri_ref[...]

    def kv(k0):
        k0 = pl.multiple_of(k0, tk)
        return ([k_ref[0, pl.ds(k0, tk), lanes[hh]] for hh in heads],
                [v_ref[0, pl.ds(k0, tk), lanes[hh]] for hh in heads])

    carries = [jnp.zeros((tq, 1), F32) for _ in heads]
    accs = [jnp.zeros((tq, HEAD_DIM), F32) for _ in heads]
    for sb in reversed(range(n_sub)):
        r0 = sb * tk
        kblks, vblks = kv(qi * tq + r0)
        c_sub, a_sub = _stick_blocks([q[r0:] for q in qs], kblks, vblks, tri, [c[r0:] for c in carries],
                                     [a[r0:] for a in accs], _strict_lower(tq - r0, tk))
        if r0:
            c_sub = [jnp.concatenate([c[:r0], cs], axis=0) for c, cs in zip(carries, c_sub)]
            a_sub = [jnp.concatenate([a[:r0], as_], axis=0) for a, as_ in zip(accs, a_sub)]
        carries, accs = c_sub, a_sub

    def cond(state):
        it, carries, _ = state
        return jnp.logical_and(
            it < qi * n_sub, jnp.logical_not(_all_weights_vanish(functools.reduce(jnp.maximum, carries))))

    def body(state):
        it, carries, accs = state
        kblks, vblks = kv((qi * n_sub - 1 - it) * tk)
        carries, accs = _stick_blocks(qs, kblks, vblks, tri, carries, accs, None)
        return it + 1, tuple(carries), tuple(accs)

    _, carries, accs = lax.while_loop(cond, body, (jnp.int32(0), tuple(carries), tuple(accs)))
    for hh in heads:
        o_ref[0, :, lanes[hh]] = (accs[hh] * sg_ref[0, :, lanes[hh]].astype(F32)).astype(o_ref.dtype)


def _tri_incl(t):
    return (jnp.arange(t)[:, None] >= jnp.arange(t)[None, :]).astype(BF16)


def _stick_prompt(q, k, v, sg, batch, seq):
    d = k.shape[1]
    tq, tk = STICK_TQ, ATTN_T
    tiles = seq // tq
    hw = STICK_HEADS * HEAD_DIM
    flat = lambda a: a.reshape(1, a.shape[0], d)
    blk_in = pl.BlockSpec((1, tq, hw), lambda b, h, i: (0, b * tiles + i, h))
    kv = pl.BlockSpec((1, seq, hw), lambda b, h, i: (0, b, h))
    out = pl.pallas_call(
        functools.partial(_stick_prompt_kernel, tk=tk),
        out_shape=jax.ShapeDtypeStruct((batch, seq, d), BF16),
        grid=(batch, d // hw, tiles),
        in_specs=[blk_in, kv, kv, blk_in, pl.BlockSpec((tk, tk), lambda b, h, i: (0, 0))],
        out_specs=pl.BlockSpec((1, tq, hw), lambda b, h, i: (b, i, h)),
        compiler_params=_params(3),
        name="stick_prompt",
    )(flat(q), flat(k), flat(v), flat(sg), _tri_incl(tk))
    return out.reshape(batch * seq, d)


def _head_rows(cache_ref, hh, t0, n):
    nh, hd = cache_ref.shape[-2:]
    flat = cache_ref.reshape(math.prod(cache_ref.shape[:-1]), hd)
    return flat[pl.ds(t0 * nh + hh, n, stride=nh), :]


def _stick_sample_kernel(q_ref, ck_hbm, cv_hbm, k_ref, v_ref, sg_ref, tri_ref, o_ref, kbuf, vbuf, sem):
    bi, g = pl.program_id(0), pl.program_id(1)
    ng = pl.num_programs(1)
    step = bi * ng + g
    parity = step % 2
    t_new = q_ref.shape[1]
    past = ck_hbm.shape[1]
    _, kblock, nh, _ = kbuf.shape
    n_blk = past // kblock
    heads = range(nh)
    lanes = [slice(hh * HEAD_DIM, (hh + 1) * HEAD_DIM) for hh in heads]
    tri = tri_ref[...]
    qs = [q_ref[0, :, lanes[hh]] for hh in heads]

    def block_copies(batch, group, blk, slot):
        rows = pl.ds(past - (blk + 1) * kblock, kblock)
        cols = pl.ds(group * nh, nh)
        return (pltpu.make_async_copy(ck_hbm.at[batch, rows, cols, :], kbuf.at[slot], sem.at[0, slot]),
                pltpu.make_async_copy(cv_hbm.at[batch, rows, cols, :], vbuf.at[slot], sem.at[1, slot]))

    def newest_copies(of_step):
        return block_copies(of_step // ng, of_step % ng, 0, of_step % 2)

    def older_copies(blk):
        return block_copies(bi, g, blk, 2 + blk % 2)

    def attend(slot, carries, accs):
        t0 = slot * kblock
        return _stick_blocks(
            qs, [_head_rows(kbuf, hh, t0, kblock).astype(BF16) for hh in heads],
            [_head_rows(vbuf, hh, t0, kblock).astype(BF16) for hh in heads], tri, carries, accs, None)

    def alive(carries):
        return jnp.logical_not(_all_weights_vanish(functools.reduce(jnp.maximum, carries)))

    @pl.when(step == 0)
    def _():
        for c in newest_copies(step):
            c.start()

    @pl.when(step + 1 < pl.num_programs(0) * ng)
    def _():
        for c in newest_copies(step + 1):
            c.start()

    carries, accs = _stick_blocks(
        qs, [k_ref[0, :, lanes[hh]].astype(BF16) for hh in heads],
        [v_ref[0, :, lanes[hh]].astype(BF16) for hh in heads], tri_ref[:t_new, :t_new],
        [jnp.zeros((t_new, 1), F32) for _ in heads], [jnp.zeros((t_new, HEAD_DIM), F32) for _ in heads],
        _strict_lower(t_new, t_new))

    for c in newest_copies(step):
        c.wait()
    carries, accs = attend(parity, carries, accs)

    more = jnp.logical_and(n_blk > 1, alive(carries))

    @pl.when(more)
    def _():
        for c in older_copies(1):
            c.start()

    def cond(state):
        it, carries, _ = state
        return jnp.logical_and(it < n_blk, alive(carries))

    def body(state):
        it, carries, accs = state
        for c in older_copies(it):
            c.wait()

        @pl.when(it + 1 < n_blk)
        def _():
            for c in older_copies(it + 1):
                c.start()

        carries, accs = attend(2 + it % 2, carries, accs)
        return it + 1, tuple(carries), tuple(accs)

    done, carries, accs = lax.while_loop(cond, body, (jnp.int32(1), tuple(carries), tuple(accs)))

    @pl.when(jnp.logical_and(more, done < n_blk))
    def _():
        for c in older_copies(done):
            c.wait()

    for hh in heads:
        o_ref[0, :, lanes[hh]] = (accs[hh] * sg_ref[0, :, lanes[hh]].astype(F32)).astype(o_ref.dtype)


def _stick_sample(q, cache_k, cache_v, k, v, sg, row0):
    b, past, n_heads, _ = cache_k.shape
    d = n_heads * HEAD_DIM
    t_new = k.shape[0] // b
    hw = SAMPLE_HEADS * HEAD_DIM
    t = ATTN_T
    blk0 = row0 // t_new
    q3 = q.reshape(q.shape[0] // t_new, t_new, d)
    sg3 = sg.reshape(q3.shape)
    merged = pl.BlockSpec((1, t_new, hw), lambda bi, g: (blk0 + bi, 0, g))
    new = pl.BlockSpec((1, t_new, hw), lambda bi, g: (bi, 0, g))
    cache = pl.BlockSpec(memory_space=pl.ANY)
    block_buf = pltpu.VMEM((CACHE_SLOTS, t, SAMPLE_HEADS, HEAD_DIM), F32)
    out = pl.pallas_call(
        _stick_sample_kernel,
        out_shape=jax.ShapeDtypeStruct((b, t_new, d), BF16),
        grid=(b, d // hw),
        in_specs=[merged, cache, cache, new, new, merged, pl.BlockSpec((t, t), lambda bi, g: (0, 0))],
        out_specs=new,
        scratch_shapes=[block_buf, block_buf, pltpu.SemaphoreType.DMA((2, CACHE_SLOTS))],
        compiler_params=_params(2),
        name="stick_sample",
    )(q3, cache_k, cache_v, k.reshape(b, t_new, d), v.reshape(b, t_new, d), sg3, _tri_incl(t))
    return out.reshape(b * t_new, d)


def _band_prompt_kernel(q_ref, sg_ref, bias_ref, *refs):
    qi = pl.program_id(2)
    t = q_ref.shape[1]
    n_kb = bias_ref.shape[2] // t
    k_refs, v_refs, o_ref = refs[:n_kb], refs[n_kb:2 * n_kb], refs[2 * n_kb]
    heads = range(q_ref.shape[2] // HEAD_DIM)
    lanes = [slice(hh * HEAD_DIM, (hh + 1) * HEAD_DIM) for hh in heads]
    exists = [qi - (n_kb - 1) + dd >= 0 for dd in range(n_kb)]
    scores = [[jnp.where(exists[dd], _dot_nt(q_ref[0, :, lanes[hh]], k_refs[dd][0, :, lanes[hh]])
                         + bias_ref[hh, :, dd * t:(dd + 1) * t], NEG_INF)
               for dd in range(n_kb)] for hh in heads]
    maxes = [functools.reduce(jnp.maximum, [s.max(axis=-1, keepdims=True) for s in scores[hh]]) for hh in heads]
    probs = [[jnp.exp(s - maxes[hh]) for s in scores[hh]] for hh in heads]
    accs = [sum(_dot(p.astype(BF16), v_refs[dd][0, :, lanes[hh]]) for dd, p in enumerate(probs[hh]))
            for hh in heads]
    for hh in heads:
        denom = sum(p.sum(axis=-1, keepdims=True) for p in probs[hh])
        o_ref[0, :, lanes[hh]] = (accs[hh] / denom * sg_ref[0, :, lanes[hh]].astype(F32)).astype(o_ref.dtype)


def _toeplitz(values_at, rows, cols):
    period = rows + cols
    j = jnp.arange(period)
    v = values_at(jnp.where(j < cols, j, j - period))
    flat = jnp.tile(v, (1,) * (v.ndim - 1) + (rows,))[..., :rows * (period - 1)]
    return flat.reshape(v.shape[:-1] + (rows, period - 1))[..., :cols]


def _band_bias_prompt(rel_bias, t):
    n_kb = (LEFT_CHUNKS * CHUNK) // t + 1
    back = (n_kb - 1) * t
    rb = rel_bias.astype(F32)
    bias = _toeplitz(lambda m: rb[:, jnp.clip(back - m, -REL_CLIP, REL_CLIP) + REL_CLIP], t, n_kb * t)
    r = jnp.arange(t)[:, None]
    c = jnp.arange(n_kb * t)[None, :]
    chunk_diff = (back + r) // CHUNK - c // CHUNK
    visible = (chunk_diff >= 0) & (chunk_diff <= LEFT_CHUNKS)
    return jnp.where(visible[None], bias, NEG_INF)


def _band_prompt(q, k, v, sg, rel_bias, batch, seq):
    d = k.shape[1]
    t = ATTN_T
    tiles = seq // t
    hw = BAND_HEADS * HEAD_DIM
    bias = _band_bias_prompt(rel_bias, t)
    n_kb = bias.shape[2] // t
    blk_in = pl.BlockSpec((1, t, hw), lambda b, h, i: (0, b * tiles + i, h))
    kv = [pl.BlockSpec((1, t, hw), functools.partial(
        lambda b, h, i, back: (0, b * tiles + jnp.maximum(i - back, 0), h), back=n_kb - 1 - dd))
        for dd in range(n_kb)]
    flat = lambda a: a.reshape(1, a.shape[0], d)
    out = pl.pallas_call(
        _band_prompt_kernel,
        out_shape=jax.ShapeDtypeStruct((batch, seq, d), BF16),
        grid=(batch, d // hw, tiles),
        in_specs=[blk_in, blk_in, pl.BlockSpec((BAND_HEADS,) + bias.shape[1:], lambda b, h, i: (h, 0, 0))]
        + kv + kv,
        out_specs=pl.BlockSpec((1, t, hw), lambda b, h, i: (b, i, h)),
        compiler_params=_params(3),
        name="band_prompt",
    )(flat(q), flat(sg), bias, *([flat(k)] * n_kb), *([flat(v)] * n_kb))
    return out.reshape(batch * seq, d)


def _band_sample_kernel(q_ref, ck_ref, cv_ref, k_ref, v_ref, sg_ref, bias_ref, o_ref):
    past = ck_ref.shape[1]
    heads = range(ck_ref.shape[2])
    lanes = [slice(hh * HEAD_DIM, (hh + 1) * HEAD_DIM) for hh in heads]
    s_old = [_dot_nt(q_ref[0, :, lanes[hh]], _head_rows(ck_ref, hh, 0, past).astype(BF16)) + bias_ref[hh, :, :past]
             for hh in heads]
    s_new = [_dot_nt(q_ref[0, :, lanes[hh]], k_ref[0, :, lanes[hh]].astype(BF16)) + bias_ref[hh, :, past:]
             for hh in heads]
    m = [jnp.maximum(a.max(axis=-1, keepdims=True), b.max(axis=-1, keepdims=True)) for a, b in zip(s_old, s_new)]
    p_old = [jnp.exp(s - mm) for s, mm in zip(s_old, m)]
    p_new = [jnp.exp(s - mm) for s, mm in zip(s_new, m)]
    accs = [_dot(p_old[hh].astype(BF16), _head_rows(cv_ref, hh, 0, past).astype(BF16))
            + _dot(p_new[hh].astype(BF16), v_ref[0, :, lanes[hh]].astype(BF16)) for hh in heads]
    for hh in heads:
        denom = p_old[hh].sum(axis=-1, keepdims=True) + p_new[hh].sum(axis=-1, keepdims=True)
        o_ref[0, :, lanes[hh]] = (accs[hh] / denom * sg_ref[0, :, lanes[hh]].astype(F32)).astype(o_ref.dtype)


def _band_bias_sample(rel_bias, past_total, past_b, t_new):
    rb = rel_bias.astype(F32)
    bias = _toeplitz(lambda m: rb[:, jnp.clip(past_b - m, -REL_CLIP, REL_CLIP) + REL_CLIP],
                     t_new, past_b + t_new)
    q_pos = past_total + jnp.arange(t_new)
    k_pos = past_total - past_b + jnp.arange(past_b + t_new)
    q_chunk = q_pos // CHUNK
    k_chunk = k_pos // CHUNK
    visible = ((k_pos[None, :] >= 0) & (k_chunk[None, :] <= q_chunk[:, None])
               & (k_chunk[None, :] >= q_chunk[:, None] - LEFT_CHUNKS))
    return jnp.where(visible[None], bias, NEG_INF)


def _band_sample(q, cache_k, cache_v, k, v, sg, rel_bias, row0, past_total):
    b, past_b, n_heads, _ = cache_k.shape
    d = n_heads * HEAD_DIM
    t_new = (q.shape[0] - row0) // b
    hw = SAMPLE_HEADS * HEAD_DIM
    blk0 = row0 // t_new
    bias = _band_bias_sample(rel_bias, past_total, past_b, t_new)
    r3 = lambda a: a.reshape(a.shape[0] // t_new, t_new, d)
    merged = pl.BlockSpec((1, t_new, hw), lambda bi, g: (blk0 + bi, 0, g))
    cache = pl.BlockSpec((1, past_b, SAMPLE_HEADS, HEAD_DIM), lambda bi, g: (bi, 0, g, 0))
    out = pl.pallas_call(
        _band_sample_kernel,
        out_shape=jax.ShapeDtypeStruct((b, t_new, d), BF16),
        grid=(b, d // hw),
        in_specs=[merged, cache, cache, merged, merged, merged,
                  pl.BlockSpec((SAMPLE_HEADS,) + bias.shape[1:], lambda bi, g: (g, 0, 0))],
        out_specs=pl.BlockSpec((1, t_new, hw), lambda bi, g: (bi, 0, g)),
        compiler_params=_params(2),
        name="band_sample",
    )(r3(q), cache_k, cache_v, r3(k), r3(v), r3(sg), bias)
    return out.reshape(b * t_new, d)


def kernel(x_prompt, x_sample, c_prompt, c_sample, cache_a_k, cache_a_v, cache_b_k, cache_b_v, w_mod_a, b_mod_a, g_norm_a, w_in_a, w_out_a, g_kv, w_kv, w_mod_b, b_mod_b, g_norm_b, w_in_b, rel_bias_b, w_out_b, g_final):
    bp, seq, d = x_prompt.shape
    bs, t_new, _ = x_sample.shape
    n_heads = d // HEAD_DIM
    past = cache_a_k.shape[2]
    past_b = cache_b_k.shape[1]
    mp, ms = bp * seq, bs * t_new
    assert w_mod_a.shape[0] == 1 and w_mod_b.shape[0] == 1, "one layer of each mixer"
    assert seq % MM_TM == 0 and mp % MM_TM == 0 and ms % MM_TM == 0 and MM_TM % t_new == 0
    assert (mp + ms) % MM_TM_NARROW_OUT == 0
    assert seq % RESID_TM == 0 and ms % RESID_TM == 0 and RESID_TM % t_new == 0 and d % MM_TN == 0
    assert bs % (RESID_TM // t_new) == 0 and bs % (NORM_TM // t_new) == 0
    assert seq % STICK_TQ == 0 and STICK_TQ % ATTN_T == 0
    assert seq % ATTN_T == 0 and past % ATTN_T == 0 and (LEFT_CHUNKS * CHUNK) % ATTN_T == 0

    xp = x_prompt.reshape(mp, d)
    xs = x_sample.reshape(ms, d)

    n_c = bs + bp
    pad = -n_c % BF16_SUBLANES
    c_all = jnp.concatenate([c_sample, c_prompt, jnp.zeros((pad, d), F32)], axis=0)
    mod_a, mod_b = _adaln_tables(c_all, w_mod_a[0], b_mod_a[0], w_mod_b[0], b_mod_b[0])
    shift_a, scale_a, gate_a = _split_mod(mod_a, d)
    shift_b, scale_b, gate_b = _split_mod(mod_b, d)

    h_a = _prenorm(xp, xs, g_norm_a[0], shift_a, scale_a, seq, t_new)
    w_in = w_in_a[0]
    q_scale = 1.0 / math.sqrt(HEAD_DIM)
    q_a, = _mm(h_a, w_in, 0, d, (BF16,), scale=q_scale)
    k_p, k_p16 = _mm(h_a, w_in, d, d, (F32, BF16), row0=0, rows=mp)
    k_s, = _mm(h_a, w_in, d, d, (F32,), row0=mp, rows=ms)
    v_p, v_p16 = _mm(h_a, w_in, 2 * d, d, (F32, BF16), row0=0, rows=mp)
    v_s, = _mm(h_a, w_in, 2 * d, d, (F32,), row0=mp, rows=ms)
    sg_a, = _mm(h_a, w_in, 3 * d, d, (BF16,), silu=True)

    u_p = _stick_prompt(q_a, k_p16, v_p16, sg_a, bp, seq)
    u_s = _stick_sample(q_a, cache_a_k[0], cache_a_v[0], k_s, v_s, sg_a, mp)
    x1_p = _mm_resid(u_p, w_out_a[0], xp, gate_a, seq, bs)
    x1_s = _mm_resid(u_s, w_out_a[0], xs, gate_a, t_new, 0)

    h_kv, h_b = _dualnorm(x1_p, x1_s, g_kv, g_norm_b[0], shift_b, scale_b, seq, t_new)
    kb, kb16 = _mm(h_kv, w_kv, 0, d, (F32, BF16))
    vb, vb16 = _mm(h_kv, w_kv, d, d, (F32, BF16))
    q_b, = _mm(h_b, w_in_b[0], 0, d, (BF16,), scale=q_scale)
    sg_b, = _mm(h_b, w_in_b[0], d, d, (BF16,), silu=True)

    ub_p = _band_prompt(q_b, kb16, vb16, sg_b, rel_bias_b[0], bp, seq)
    ub_s = _band_sample(q_b, cache_b_k, cache_b_v, kb16, vb16, sg_b, rel_bias_b[0], mp, past)
    x2_p = _mm_resid(ub_p, w_out_b[0], x1_p, gate_b, seq, bs)
    x2_s = _mm_resid(ub_s, w_out_b[0], x1_s, gate_b, t_new, 0)

    y_p = _finalnorm(x2_p, g_final).reshape(bp, seq, d)
    y_s = _finalnorm(x2_s, g_final).reshape(bs, t_new, d)

    keep_b = min(LEFT_CHUNKS * CHUNK, seq)
    heads = lambda a, b, t: a.reshape(b, t, n_heads, HEAD_DIM)
    tail = lambda a: jnp.stack([a[(b + 1) * seq - keep_b:(b + 1) * seq] for b in range(bp)])
    return (y_p, y_s,
            heads(k_p, bp, seq)[None], heads(v_p, bp, seq)[None],
            heads(k_s, bs, t_new)[None], heads(v_s, bs, t_new)[None],
            heads(tail(kb), bp, keep_b), heads(tail(vb), bp, keep_b),
            heads(kb[mp:], bs, t_new), heads(vb[mp:], bs, t_new))
```

```python
import functools
import math

import jax
import jax.numpy as jnp
from jax import lax
from jax.experimental import pallas as pl
from jax.experimental.pallas import tpu as pltpu

F32 = jnp.float32
BF16 = jnp.bfloat16

HEAD_DIM = 128
CHUNK = 64
LEFT_CHUNKS = 8
REL_CLIP = 128
EPS = 1e-6
NEG_INF = -1e30

V7X_VMEM_LIMIT_BYTES = 54 * 1024 * 1024
LANES = 128
BF16_SUBLANES = 16

MM_TM = 512
MM_TM_NARROW_OUT = 1024
RESID_TM = 512
MM_TN = 1024
WEIGHT_DMA_STREAMS = 8
CACHE_DMA_STREAMS = 2
NORM_TM = 256
MOD_TN = 512
MOD_DMA_STREAMS = 2
ATTN_T = 256
STICK_TQ = 256
STICK_HEADS = 8
BAND_HEADS = 4

F32_EXP_UNDERFLOW_LOG = -104.0
SAMPLE_HEADS = 8
CACHE_SLOTS = 4


def _params(n_grid):
    return pltpu.CompilerParams(
        dimension_semantics=("arbitrary",) * n_grid,
        vmem_limit_bytes=V7X_VMEM_LIMIT_BYTES)


def _silu(x):
    return x / (1.0 + jnp.exp(-x))


def _dot(a, b):
    return jnp.dot(a, b, preferred_element_type=F32)


def _dot_nt(a, b):
    return lax.dot_general(a, b, (((1,), (1,)), ((), ())), preferred_element_type=F32)


def _mod_kernel(c_ref, ba_ref, bb_ref, *refs):
    n = MOD_DMA_STREAMS
    wa_refs, wb_refs, (oa_ref, ob_ref) = refs[:n], refs[n:2 * n], refs[2 * n:]
    a = _silu(c_ref[...]).astype(BF16)
    for w_refs, b_ref, o_ref in ((wa_refs, ba_ref, oa_ref), (wb_refs, bb_ref, ob_ref)):
        for s, w_ref in enumerate(w_refs):
            cols = slice(s * w_ref.shape[1], (s + 1) * w_ref.shape[1])
            o_ref[:, cols] = _dot(a, w_ref[...].astype(BF16)) + b_ref[:, cols]


def _adaln_tables(c_all, w_a, b_a, w_b, b_b):
    rows, d = c_all.shape
    n = w_a.shape[1]
    ns = MOD_DMA_STREAMS
    w_specs = [pl.BlockSpec((d, MOD_TN // ns), functools.partial(lambda j, s: (0, j * ns + s), s=s)) for s in range(ns)]
    b_spec = pl.BlockSpec((1, MOD_TN), lambda j: (0, j))
    o_spec = pl.BlockSpec((rows, MOD_TN), lambda j: (0, j))
    return pl.pallas_call(
        _mod_kernel,
        out_shape=(jax.ShapeDtypeStruct((rows, n), F32),) * 2,
        grid=(n // MOD_TN,),
        in_specs=[pl.BlockSpec((rows, d), lambda j: (0, 0)), b_spec, b_spec] + w_specs + w_specs,
        out_specs=(o_spec, o_spec),
        compiler_params=_params(1),
        name="adaln_mod",
    )(c_all, b_a.reshape(1, n), b_b.reshape(1, n), *([w_a] * ns), *([w_b] * ns))


def _split_mod(mod, d):
    rows = mod.shape[0]
    return tuple(mod[:, k * d:(k + 1) * d].reshape(rows, 1, d) for k in range(3))


def _rms_scale(x):
    return lax.rsqrt(jnp.mean(x * x, axis=-1, keepdims=True) + EPS)


def _modulate(y, shift_ref, scale_ref):
    nb = shift_ref.shape[0]
    tm, d = y.shape
    y3 = y.reshape(nb, tm // nb, d)
    return (y3 * (1.0 + scale_ref[...]) + shift_ref[...]).reshape(tm, d)


def _prenorm_kernel(xp_ref, xs_ref, g_ref, shp_ref, scp_ref, shs_ref, scs_ref, o_ref, *, n_prompt_tiles):
    i = pl.program_id(0)

    def run(x_ref, shift_ref, scale_ref):
        x = x_ref[...]
        y = x * _rms_scale(x) * g_ref[...]
        o_ref[...] = _modulate(y, shift_ref, scale_ref).astype(o_ref.dtype)

    @pl.when(i < n_prompt_tiles)
    def _():
        run(xp_ref, shp_ref, scp_ref)

    @pl.when(i >= n_prompt_tiles)
    def _():
        run(xs_ref, shs_ref, scs_ref)


def _group_specs(tm, d, n_prompt_tiles, prompt_rpb, sample_rpb, n_sample_batches):
    tiles_per_prompt_batch = prompt_rpb // tm
    nb_s = tm // sample_rpb
    rows_p = pl.BlockSpec((tm, d), lambda i: (jnp.minimum(i, n_prompt_tiles - 1), 0))
    rows_s = pl.BlockSpec((tm, d), lambda i: (jnp.maximum(i - n_prompt_tiles, 0), 0))
    tab_p = pl.BlockSpec(
        (1, 1, d),
        lambda i: (n_sample_batches + jnp.minimum(i, n_prompt_tiles - 1) // tiles_per_prompt_batch, 0, 0))
    tab_s = pl.BlockSpec((nb_s, 1, d), lambda i: (jnp.maximum(i - n_prompt_tiles, 0), 0, 0))
    return rows_p, rows_s, tab_p, tab_s


def _prenorm(x_p, x_s, g, shift, scale, prompt_rpb, sample_rpb):
    mp, d = x_p.shape
    ms = x_s.shape[0]
    tm = NORM_TM
    npt = mp // tm
    rows_p, rows_s, tab_p, tab_s = _group_specs(tm, d, npt, prompt_rpb, sample_rpb, ms // sample_rpb)
    return pl.pallas_call(
        functools.partial(_prenorm_kernel, n_prompt_tiles=npt),
        out_shape=jax.ShapeDtypeStruct((mp + ms, d), BF16),
        grid=((mp + ms) // tm,),
        in_specs=[rows_p, rows_s, pl.BlockSpec((1, d), lambda i: (0, 0)), tab_p, tab_p, tab_s, tab_s],
        out_specs=pl.BlockSpec((tm, d), lambda i: (i, 0)),
        compiler_params=_params(1),
        name="prenorm",
    )(x_p, x_s, g.reshape(1, d), shift, scale, shift, scale)


def _dualnorm_kernel(xp_ref, xs_ref, gkv_ref, gb_ref, shp_ref, scp_ref, shs_ref, scs_ref,
                     okv_ref, ob_ref, *, n_prompt_tiles):
    i = pl.program_id(0)

    def run(x_ref, shift_ref, scale_ref):
        x = x_ref[...]
        y = x * _rms_scale(x)
        okv_ref[...] = (y * gkv_ref[...]).astype(okv_ref.dtype)
        ob_ref[...] = _modulate(y * gb_ref[...], shift_ref, scale_ref).astype(ob_ref.dtype)

    @pl.when(i < n_prompt_tiles)
    def _():
        run(xp_ref, shp_ref, scp_ref)

    @pl.when(i >= n_prompt_tiles)
    def _():
        run(xs_ref, shs_ref, scs_ref)


def _dualnorm(x_p, x_s, g_kv, g_b, shift, scale, prompt_rpb, sample_rpb):
    mp, d = x_p.shape
    ms = x_s.shape[0]
    tm = NORM_TM
    npt = mp // tm
    rows_p, rows_s, tab_p, tab_s = _group_specs(tm, d, npt, prompt_rpb, sample_rpb, ms // sample_rpb)
    g_spec = pl.BlockSpec((1, d), lambda i: (0, 0))
    o_spec = pl.BlockSpec((tm, d), lambda i: (i, 0))
    return pl.pallas_call(
        functools.partial(_dualnorm_kernel, n_prompt_tiles=npt),
        out_shape=(jax.ShapeDtypeStruct((mp + ms, d), BF16),) * 2,
        grid=((mp + ms) // tm,),
        in_specs=[rows_p, rows_s, g_spec, g_spec, tab_p, tab_p, tab_s, tab_s],
        out_specs=(o_spec, o_spec),
        compiler_params=_params(1),
        name="dualnorm",
    )(x_p, x_s, g_kv.reshape(1, d), g_b.reshape(1, d), shift, scale, shift, scale)


def _finalnorm_kernel(x_ref, g_ref, o_ref):
    x = x_ref[...]
    o_ref[...] = x * _rms_scale(x) * g_ref[...]


def _finalnorm(x, g):
    m, d = x.shape
    tm = NORM_TM
    return pl.pallas_call(
        _finalnorm_kernel,
        out_shape=jax.ShapeDtypeStruct((m, d), F32),
        grid=(m // tm,),
        in_specs=[pl.BlockSpec((tm, d), lambda i: (i, 0)), pl.BlockSpec((1, d), lambda i: (0, 0))],
        out_specs=pl.BlockSpec((tm, d), lambda i: (i, 0)),
        compiler_params=_params(1),
        name="finalnorm",
    )(x, g.reshape(1, d))


def _stage_weight_tile(w_hbm, stage_ref, wbf_ref, sem, col0):
    j, nj = pl.program_id(0), pl.num_programs(0)
    k, tn = wbf_ref.shape
    rows = k // WEIGHT_DMA_STREAMS

    def tile_copies(jj):
        return [pltpu.make_async_copy(w_hbm.at[pl.ds(c * rows, rows), pl.ds(col0 + jj * tn, tn)],
                                      stage_ref.at[pl.ds(c * rows, rows)], sem.at[c])
                for c in range(WEIGHT_DMA_STREAMS)]

    @pl.when(pl.program_id(1) == 0)
    def _():
        @pl.when(j == 0)
        def _():
            for c in tile_copies(0):
                c.start()

        for c, copy in enumerate(tile_copies(j)):
            copy.wait()
            chunk = slice(c * rows, (c + 1) * rows)
            wbf_ref[chunk, :] = stage_ref[chunk, :].astype(BF16)

        @pl.when(j + 1 < nj)
        def _():
            for c in tile_copies(j + 1):
                c.start()


def _weight_scratch(k, tn):
    return [pltpu.VMEM((k, tn), F32), pltpu.VMEM((k, tn), BF16), pltpu.SemaphoreType.DMA((WEIGHT_DMA_STREAMS,))]


def _mm_kernel(h_ref, w_hbm, *refs, silu, scale, col0):
    *o_refs, stage_ref, wbf_ref, sem = refs
    _stage_weight_tile(w_hbm, stage_ref, wbf_ref, sem, col0)
    acc = _dot(h_ref[...], wbf_ref[...])
    if silu:
        acc = _silu(acc)
    if scale is not None:
        acc = acc * scale
    for o_ref in o_refs:
        o_ref[...] = acc.astype(o_ref.dtype)


def _mm(h, w, col0, n_out, out_dtypes, *, silu=False, scale=None, row0=0, rows=None):
    k = h.shape[1]
    rows = h.shape[0] if rows is None else rows
    tm = MM_TM if F32 in out_dtypes else MM_TM_NARROW_OUT
    tn = MM_TN
    i0 = row0 // tm
    o_spec = pl.BlockSpec((tm, tn), lambda j, i: (i, j))
    return pl.pallas_call(
        functools.partial(_mm_kernel, silu=silu, scale=scale, col0=col0),
        out_shape=tuple(jax.ShapeDtypeStruct((rows, n_out), dt) for dt in out_dtypes),
        grid=(n_out // tn, rows // tm),
        in_specs=[pl.BlockSpec((tm, k), lambda j, i: (i0 + i, 0)),
                  pl.BlockSpec(memory_space=pl.ANY)],
        out_specs=tuple(o_spec for _ in out_dtypes),
        scratch_shapes=_weight_scratch(k, tn),
        compiler_params=_params(2),
        name="proj",
    )(h, w)


def _mm_resid_kernel(u_ref, w_hbm, x_ref, gate_ref, o_ref, stage_ref, wbf_ref, sem):
    _stage_weight_tile(w_hbm, stage_ref, wbf_ref, sem, 0)
    acc = _dot(u_ref[...], wbf_ref[...])
    nb = gate_ref.shape[0]
    tm, tn = acc.shape
    y = acc.reshape(nb, tm // nb, tn) * gate_ref[...]
    o_ref[...] = x_ref[...] + y.reshape(tm, tn)


def _mm_resid(u, w, x, gate, rows_per_batch, table_row0):
    m, k = u.shape
    n = w.shape[1]
    tm, tn = RESID_TM, MM_TN
    if rows_per_batch >= tm:
        nb = 1
        tiles_per_batch = rows_per_batch // tm
        gate_map = lambda j, i: (table_row0 + i // tiles_per_batch, 0, j)
    else:
        nb = tm // rows_per_batch
        blk0 = table_row0 // nb
        gate_map = lambda j, i: (blk0 + i, 0, j)
    return pl.pallas_call(
        _mm_resid_kernel,
        out_shape=jax.ShapeDtypeStruct((m, n), F32),
        grid=(n // tn, m // tm),
        in_specs=[pl.BlockSpec((tm, k), lambda j, i: (i, 0)),
                  pl.BlockSpec(memory_space=pl.ANY),
                  pl.BlockSpec((tm, tn), lambda j, i: (i, j)),
                  pl.BlockSpec((nb, 1, tn), gate_map)],
        out_specs=pl.BlockSpec((tm, tn), lambda j, i: (i, j)),
        scratch_shapes=_weight_scratch(k, tn),
        compiler_params=_params(2),
        name="out_proj",
    )(u, w, x, gate)


def _softplus(z):
    return jnp.maximum(z, 0.0) + jnp.log(1.0 + jnp.exp(-jnp.abs(z)))


def _suffix_sum(x, tri):
    hi = x.astype(BF16)
    lo = (x - hi.astype(F32)).astype(BF16)
    return _dot(hi, tri) + _dot(lo, tri)


def _stick_blocks(qs, kblks, vblks, tri, carries, accs, causal):
    zs = [_dot_nt(q, kblk) for q, kblk in zip(qs, kblks)]
    sps = [_softplus(z) if causal is None else jnp.where(causal, _softplus(z), 0.0) for z in zs]
    incls = [_suffix_sum(sp, tri) for sp in sps]
    ws = [jnp.exp(z - incl + carry) for z, incl, carry in zip(zs, incls, carries)]
    if causal is not None:
        ws = [jnp.where(causal, w, 0.0) for w in ws]
    accs = [acc + _dot(w.astype(BF16), vblk) for acc, w, vblk in zip(accs, ws, vblks)]
    carries = [carry - incl[:, :1] for carry, incl in zip(carries, incls)]
    return carries, accs


def _strict_lower(tq, tk):
    return lax.broadcasted_iota(jnp.int32, (tq, tk), 1) < lax.broadcasted_iota(jnp.int32, (tq, tk), 0)


def _all_weights_vanish(carry):
    return jnp.max(carry) <= F32_EXP_UNDERFLOW_LOG


def _stick_prompt_kernel(q_ref, k_ref, v_ref, sg_ref, tri_ref, o_ref, *, tk):
    qi = pl.program_id(2)
    tq = q_ref.shape[1]
    n_sub = tq // tk
    heads = range(q_ref.shape[2] // HEAD_DIM)
    lanes = [slice(hh * HEAD_DIM, (hh + 1) * HEAD_DIM) for hh in heads]
    qs = [q_ref[0, :, lanes[hh]] for hh in heads]
    tri = tri_ref[...]

    def kv(k0):
        k0 = pl.multiple_of(k0, tk)
        return ([k_ref[0, pl.ds(k0, tk), lanes[hh]] for hh in heads],
                [v_ref[0, pl.ds(k0, tk), lanes[hh]] for hh in heads])

    carries = [jnp.zeros((tq, 1), F32) for _ in heads]
    accs = [jnp.zeros((tq, HEAD_DIM), F32) for _ in heads]
    for sb in reversed(range(n_sub)):
        r0 = sb * tk
        kblks, vblks = kv(qi * tq + r0)
        c_sub, a_sub = _stick_blocks([q[r0:] for q in qs], kblks, vblks, tri, [c[r0:] for c in carries],
                                     [a[r0:] for a in accs], _strict_lower(tq - r0, tk))
        if r0:
            c_sub = [jnp.concatenate([c[:r0], cs], axis=0) for c, cs in zip(carries, c_sub)]
            a_sub = [jnp.concatenate([a[:r0], as_], axis=0) for a, as_ in zip(accs, a_sub)]
        carries, accs = c_sub, a_sub

    def cond(state):
        it, carries, _ = state
        return jnp.logical_and(
            it < qi * n_sub, jnp.logical_not(_all_weights_vanish(functools.reduce(jnp.maximum, carries))))

    def body(state):
        it, carries, accs = state
        kblks, vblks = kv((qi * n_sub - 1 - it) * tk)
        carries, accs = _stick_blocks(qs, kblks, vblks, tri, carries, accs, None)
        return it + 1, tuple(carries), tuple(accs)

    _, carries, accs = lax.while_loop(cond, body, (jnp.int32(0), tuple(carries), tuple(accs)))
    for hh in heads:
        o_ref[0, :, lanes[hh]] = (accs[hh] * sg_ref[0, :, lanes[hh]].astype(F32)).astype(o_ref.dtype)


def _tri_incl(t):
    return (jnp.arange(t)[:, None] >= jnp.arange(t)[None, :]).astype(BF16)


def _stick_prompt(q, k, v, sg, batch, seq):
    d = k.shape[1]
    tq, tk = STICK_TQ, ATTN_T
    tiles = seq // tq
    hw = STICK_HEADS * HEAD_DIM
    flat = lambda a: a.reshape(1, a.shape[0], d)
    blk_in = pl.BlockSpec((1, tq, hw), lambda b, h, i: (0, b * tiles + i, h))
    kv = pl.BlockSpec((1, seq, hw), lambda b, h, i: (0, b, h))
    out = pl.pallas_call(
        functools.partial(_stick_prompt_kernel, tk=tk),
        out_shape=jax.ShapeDtypeStruct((batch, seq, d), BF16),
        grid=(batch, d // hw, tiles),
        in_specs=[blk_in, kv, kv, blk_in, pl.BlockSpec((tk, tk), lambda b, h, i: (0, 0))],
        out_specs=pl.BlockSpec((1, tq, hw), lambda b, h, i: (b, i, h)),
        compiler_params=_params(3),
        name="stick_prompt",
    )(flat(q), flat(k), flat(v), flat(sg), _tri_incl(tk))
    return out.reshape(batch * seq, d)


def _head_rows(cache_ref, hh, t0, n):
    nh, hd = cache_ref.shape[-2:]
    flat = cache_ref.reshape(math.prod(cache_ref.shape[:-1]), hd)
    return flat[pl.ds(t0 * nh + hh, n, stride=nh), :]


def _stick_sample_kernel(q_ref, ck_hbm, cv_hbm, k_ref, v_ref, sg_ref, tri_ref, o_ref, kbuf, vbuf, sem):
    bi, g = pl.program_id(0), pl.program_id(1)
    ng = pl.num_programs(1)
    step = bi * ng + g
    parity = step % 2
    t_new = q_ref.shape[1]
    past = ck_hbm.shape[1]
    _, kblock, nh, _ = kbuf.shape
    n_blk = past // kblock
    heads = range(nh)
    lanes = [slice(hh * HEAD_DIM, (hh + 1) * HEAD_DIM) for hh in heads]
    tri = tri_ref[...]
    qs = [q_ref[0, :, lanes[hh]] for hh in heads]

    def block_copies(batch, group, blk, slot):
        rows = pl.ds(past - (blk + 1) * kblock, kblock)
        cols = pl.ds(group * nh, nh)
        return (pltpu.make_async_copy(ck_hbm.at[batch, rows, cols, :], kbuf.at[slot], sem.at[0, slot]),
                pltpu.make_async_copy(cv_hbm.at[batch, rows, cols, :], vbuf.at[slot], sem.at[1, slot]))

    def newest_copies(of_step):
        return block_copies(of_step // ng, of_step % ng, 0, of_step % 2)

    def older_copies(blk):
        return block_copies(bi, g, blk, 2 + blk % 2)

    def attend(slot, carries, accs):
        t0 = slot * kblock
        return _stick_blocks(
            qs, [_head_rows(kbuf, hh, t0, kblock).astype(BF16) for hh in heads],
            [_head_rows(vbuf, hh, t0, kblock).astype(BF16) for hh in heads], tri, carries, accs, None)

    def alive(carries):
        return jnp.logical_not(_all_weights_vanish(functools.reduce(jnp.maximum, carries)))

    @pl.when(step == 0)
    def _():
        for c in newest_copies(step):
            c.start()

    @pl.when(step + 1 < pl.num_programs(0) * ng)
    def _():
        for c in newest_copies(step + 1):
            c.start()

    carries, accs = _stick_blocks(
        qs, [k_ref[0, :, lanes[hh]].astype(BF16) for hh in heads],
        [v_ref[0, :, lanes[hh]].astype(BF16) for hh in heads], tri_ref[:t_new, :t_new],
        [jnp.zeros((t_new, 1), F32) for _ in heads], [jnp.zeros((t_new, HEAD_DIM), F32) for _ in heads],
        _strict_lower(t_new, t_new))

    for c in newest_copies(step):
        c.wait()
    carries, accs = attend(parity, carries, accs)

    more = jnp.logical_and(n_blk > 1, alive(carries))

    @pl.when(more)
    def _():
        for c in older_copies(1):
            c.start()

    def cond(state):
        it, carries, _ = state
        return jnp.logical_and(it < n_blk, alive(carries))

    def body(state):
        it, carries, accs = state
        for c in older_copies(it):
            c.wait()

        @pl.when(it + 1 < n_blk)
        def _():
            for c in older_copies(it + 1):
                c.start()

        carries, accs = attend(2 + it % 2, carries, accs)
        return it + 1, tuple(carries), tuple(accs)

    done, carries, accs = lax.while_loop(cond, body, (jnp.int32(1), tuple(carries), tuple(accs)))

    @pl.when(jnp.logical_and(more, done < n_blk))
    def _():
        for c in older_copies(done):
            c.wait()

    for hh in heads:
        o_ref[0, :, lanes[hh]] = (accs[hh] * sg_ref[0, :, lanes[hh]].astype(F32)).astype(o_ref.dtype)


def _stick_sample(q, cache_k, cache_v, k, v, sg, row0):
    b, past, n_heads, _ = cache_k.shape
    d = n_heads * HEAD_DIM
    t_new = k.shape[0] // b
    hw = SAMPLE_HEADS * HEAD_DIM
    t = ATTN_T
    blk0 = row0 // t_new
    q3 = q.reshape(q.shape[0] // t_new, t_new, d)
    sg3 = sg.reshape(q3.shape)
    merged = pl.BlockSpec((1, t_new, hw), lambda bi, g: (blk0 + bi, 0, g))
    new = pl.BlockSpec((1, t_new, hw), lambda bi, g: (bi, 0, g))
    cache = pl.BlockSpec(memory_space=pl.ANY)
    block_buf = pltpu.VMEM((CACHE_SLOTS, t, SAMPLE_HEADS, HEAD_DIM), F32)
    out = pl.pallas_call(
        _stick_sample_kernel,
        out_shape=jax.ShapeDtypeStruct((b, t_new, d), BF16),
        grid=(b, d // hw),
        in_specs=[merged, cache, cache, new, new, merged, pl.BlockSpec((t, t), lambda bi, g: (0, 0))],
        out_specs=new,
        scratch_shapes=[block_buf, block_buf, pltpu.SemaphoreType.DMA((2, CACHE_SLOTS))],
        compiler_params=_params(2),
        name="stick_sample",
    )(q3, cache_k, cache_v, k.reshape(b, t_new, d), v.reshape(b, t_new, d), sg3, _tri_incl(t))
    return out.reshape(b * t_new, d)


def _band_prompt_kernel(q_ref, sg_ref, bias_ref, *refs):
    qi = pl.program_id(2)
    t = q_ref.shape[1]
    n_kb = bias_ref.shape[2] // t
    k_refs, v_refs, o_ref = refs[:n_kb], refs[n_kb:2 * n_kb], refs[2 * n_kb]
    heads = range(q_ref.shape[2] // HEAD_DIM)
    lanes = [slice(hh * HEAD_DIM, (hh + 1) * HEAD_DIM) for hh in heads]
    exists = [qi - (n_kb - 1) + dd >= 0 for dd in range(n_kb)]
    scores = [[jnp.where(exists[dd], _dot_nt(q_ref[0, :, lanes[hh]], k_refs[dd][0, :, lanes[hh]])
                         + bias_ref[hh, :, dd * t:(dd + 1) * t], NEG_INF)
               for dd in range(n_kb)] for hh in heads]
    maxes = [functools.reduce(jnp.maximum, [s.max(axis=-1, keepdims=True) for s in scores[hh]]) for hh in heads]
    probs = [[jnp.exp(s - maxes[hh]) for s in scores[hh]] for hh in heads]
    accs = [sum(_dot(p.astype(BF16), v_refs[dd][0, :, lanes[hh]]) for dd, p in enumerate(probs[hh]))
            for hh in heads]
    for hh in heads:
        denom = sum(p.sum(axis=-1, keepdims=True) for p in probs[hh])
        o_ref[0, :, lanes[hh]] = (accs[hh] / denom * sg_ref[0, :, lanes[hh]].astype(F32)).astype(o_ref.dtype)


def _toeplitz(values_at, rows, cols):
    period = rows + cols
    j = jnp.arange(period)
    v = values_at(jnp.where(j < cols, j, j - period))
    flat = jnp.tile(v, (1,) * (v.ndim - 1) + (rows,))[..., :rows * (period - 1)]
    return flat.reshape(v.shape[:-1] + (rows, period - 1))[..., :cols]


def _band_bias_prompt(rel_bias, t):
    n_kb = (LEFT_CHUNKS * CHUNK) // t + 1
    back = (n_kb - 1) * t
    rb = rel_bias.astype(F32)
    bias = _toeplitz(lambda m: rb[:, jnp.clip(back - m, -REL_CLIP, REL_CLIP) + REL_CLIP], t, n_kb * t)
    r = jnp.arange(t)[:, None]
    c = jnp.arange(n_kb * t)[None, :]
    chunk_diff = (back + r) // CHUNK - c // CHUNK
    visible = (chunk_diff >= 0) & (chunk_diff <= LEFT_CHUNKS)
    return jnp.where(visible[None], bias, NEG_INF)


def _band_prompt(q, k, v, sg, rel_bias, batch, seq):
    d = k.shape[1]
    t = ATTN_T
    tiles = seq // t
    hw = BAND_HEADS * HEAD_DIM
    bias = _band_bias_prompt(rel_bias, t)
    n_kb = bias.shape[2] // t
    blk_in = pl.BlockSpec((1, t, hw), lambda b, h, i: (0, b * tiles + i, h))
    kv = [pl.BlockSpec((1, t, hw), functools.partial(
        lambda b, h, i, back: (0, b * tiles + jnp.maximum(i - back, 0), h), back=n_kb - 1 - dd))
        for dd in range(n_kb)]
    flat = lambda a: a.reshape(1, a.shape[0], d)
    out = pl.pallas_call(
        _band_prompt_kernel,
        out_shape=jax.ShapeDtypeStruct((batch, seq, d), BF16),
        grid=(batch, d // hw, tiles),
        in_specs=[blk_in, blk_in, pl.BlockSpec((BAND_HEADS,) + bias.shape[1:], lambda b, h, i: (h, 0, 0))]
        + kv + kv,
        out_specs=pl.BlockSpec((1, t, hw), lambda b, h, i: (b, i, h)),
        compiler_params=_params(3),
        name="band_prompt",
    )(flat(q), flat(sg), bias, *([flat(k)] * n_kb), *([flat(v)] * n_kb))
    return out.reshape(batch * seq, d)


def _band_sample_kernel(q_ref, k_ref, v_ref, sg_ref, bias_ref, *refs):
    n = CACHE_DMA_STREAMS
    ck_refs, cv_refs, o_ref = refs[:n], refs[n:2 * n], refs[2 * n]
    slab, nh = ck_refs[0].shape[1:3]
    heads = range(nh)
    lanes = [slice(hh * HEAD_DIM, (hh + 1) * HEAD_DIM) for hh in heads]

    def keys(hh):
        old = [(_head_rows(ck_refs[s], hh, 0, slab).astype(BF16), slice(s * slab, (s + 1) * slab)) for s in range(n)]
        return old + [(k_ref[0, :, lanes[hh]].astype(BF16), slice(n * slab, None))]

    def values(hh):
        return [_head_rows(cv_refs[s], hh, 0, slab).astype(BF16) for s in range(n)] + [
            v_ref[0, :, lanes[hh]].astype(BF16)]

    scores = [[_dot_nt(q_ref[0, :, lanes[hh]], kblk) + bias_ref[hh, :, cols] for kblk, cols in keys(hh)]
              for hh in heads]
    maxes = [functools.reduce(jnp.maximum, [s.max(axis=-1, keepdims=True) for s in scores[hh]]) for hh in heads]
    probs = [[jnp.exp(s - maxes[hh]) for s in scores[hh]] for hh in heads]
    accs = [sum(_dot(p.astype(BF16), vblk) for p, vblk in zip(probs[hh], values(hh))) for hh in heads]
    for hh in heads:
        denom = sum(p.sum(axis=-1, keepdims=True) for p in probs[hh])
        o_ref[0, :, lanes[hh]] = (accs[hh] / denom * sg_ref[0, :, lanes[hh]].astype(F32)).astype(o_ref.dtype)


def _band_bias_sample(rel_bias, past_total, past_b, t_new):
    rb = rel_bias.astype(F32)
    bias = _toeplitz(lambda m: rb[:, jnp.clip(past_b - m, -REL_CLIP, REL_CLIP) + REL_CLIP],
                     t_new, past_b + t_new)
    q_pos = past_total + jnp.arange(t_new)
    k_pos = past_total - past_b + jnp.arange(past_b + t_new)
    q_chunk = q_pos // CHUNK
    k_chunk = k_pos // CHUNK
    visible = ((k_pos[None, :] >= 0) & (k_chunk[None, :] <= q_chunk[:, None])
               & (k_chunk[None, :] >= q_chunk[:, None] - LEFT_CHUNKS))
    return jnp.where(visible[None], bias, NEG_INF)


def _band_sample(q, cache_k, cache_v, k, v, sg, rel_bias, row0, past_total):
    b, past_b, n_heads, _ = cache_k.shape
    d = n_heads * HEAD_DIM
    t_new = (q.shape[0] - row0) // b
    hw = SAMPLE_HEADS * HEAD_DIM
    blk0 = row0 // t_new
    bias = _band_bias_sample(rel_bias, past_total, past_b, t_new)
    r3 = lambda a: a.reshape(a.shape[0] // t_new, t_new, d)
    merged = pl.BlockSpec((1, t_new, hw), lambda bi, g: (blk0 + bi, 0, g))
    n = CACHE_DMA_STREAMS
    cache = [pl.BlockSpec((1, past_b // n, SAMPLE_HEADS, HEAD_DIM),
                          functools.partial(lambda bi, g, s: (bi, s, g, 0), s=s)) for s in range(n)]
    out = pl.pallas_call(
        _band_sample_kernel,
        out_shape=jax.ShapeDtypeStruct((b, t_new, d), BF16),
        grid=(b, d // hw),
        in_specs=[merged, merged, merged, merged,
                  pl.BlockSpec((SAMPLE_HEADS,) + bias.shape[1:], lambda bi, g: (g, 0, 0))] + cache + cache,
        out_specs=pl.BlockSpec((1, t_new, hw), lambda bi, g: (bi, 0, g)),
        compiler_params=_params(2),
        name="band_sample",
    )(r3(q), r3(k), r3(v), r3(sg), bias, *([cache_k] * n), *([cache_v] * n))
    return out.reshape(b * t_new, d)


def kernel(x_prompt, x_sample, c_prompt, c_sample, cache_a_k, cache_a_v, cache_b_k, cache_b_v, w_mod_a, b_mod_a, g_norm_a, w_in_a, w_out_a, g_kv, w_kv, w_mod_b, b_mod_b, g_norm_b, w_in_b, rel_bias_b, w_out_b, g_final):
    bp, seq, d = x_prompt.shape
    bs, t_new, _ = x_sample.shape
    n_heads = d // HEAD_DIM
    past = cache_a_k.shape[2]
    past_b = cache_b_k.shape[1]
    mp, ms = bp * seq, bs * t_new
    assert w_mod_a.shape[0] == 1 and w_mod_b.shape[0] == 1, "one layer of each mixer"
    assert seq % MM_TM == 0 and mp % MM_TM == 0 and ms % MM_TM == 0 and MM_TM % t_new == 0
    assert (mp + ms) % MM_TM_NARROW_OUT == 0
    assert seq % RESID_TM == 0 and ms % RESID_TM == 0 and RESID_TM % t_new == 0 and d % MM_TN == 0
    assert bs % (RESID_TM // t_new) == 0 and bs % (NORM_TM // t_new) == 0
    assert seq % STICK_TQ == 0 and STICK_TQ % ATTN_T == 0
    assert seq % ATTN_T == 0 and past % ATTN_T == 0 and (LEFT_CHUNKS * CHUNK) % ATTN_T == 0

    xp = x_prompt.reshape(mp, d)
    xs = x_sample.reshape(ms, d)

    n_c = bs + bp
    pad = -n_c % BF16_SUBLANES
    c_all = jnp.concatenate([c_sample, c_prompt, jnp.zeros((pad, d), F32)], axis=0)
    mod_a, mod_b = _adaln_tables(c_all, w_mod_a[0], b_mod_a[0], w_mod_b[0], b_mod_b[0])
    shift_a, scale_a, gate_a = _split_mod(mod_a, d)
    shift_b, scale_b, gate_b = _split_mod(mod_b, d)

    h_a = _prenorm(xp, xs, g_norm_a[0], shift_a, scale_a, seq, t_new)
    w_in = w_in_a[0]
    q_scale = 1.0 / math.sqrt(HEAD_DIM)
    q_a, = _mm(h_a, w_in, 0, d, (BF16,), scale=q_scale)
    k_p, k_p16 = _mm(h_a, w_in, d, d, (F32, BF16), row0=0, rows=mp)
    k_s, = _mm(h_a, w_in, d, d, (F32,), row0=mp, rows=ms)
    v_p, v_p16 = _mm(h_a, w_in, 2 * d, d, (F32, BF16), row0=0, rows=mp)
    v_s, = _mm(h_a, w_in, 2 * d, d, (F32,), row0=mp, rows=ms)
    sg_a, = _mm(h_a, w_in, 3 * d, d, (BF16,), silu=True)

    u_p = _stick_prompt(q_a, k_p16, v_p16, sg_a, bp, seq)
    u_s = _stick_sample(q_a, cache_a_k[0], cache_a_v[0], k_s, v_s, sg_a, mp)
    x1_p = _mm_resid(u_p, w_out_a[0], xp, gate_a, seq, bs)
    x1_s = _mm_resid(u_s, w_out_a[0], xs, gate_a, t_new, 0)

    h_kv, h_b = _dualnorm(x1_p, x1_s, g_kv, g_norm_b[0], shift_b, scale_b, seq, t_new)
    kb, kb16 = _mm(h_kv, w_kv, 0, d, (F32, BF16))
    vb, vb16 = _mm(h_kv, w_kv, d, d, (F32, BF16))
    q_b, = _mm(h_b, w_in_b[0], 0, d, (BF16,), scale=q_scale)
    sg_b, = _mm(h_b, w_in_b[0], d, d, (BF16,), silu=True)

    ub_p = _band_prompt(q_b, kb16, vb16, sg_b, rel_bias_b[0], bp, seq)
    ub_s = _band_sample(q_b, cache_b_k, cache_b_v, kb16, vb16, sg_b, rel_bias_b[0], mp, past)
    x2_p = _mm_resid(ub_p, w_out_b[0], x1_p, gate_b, seq, bs)
    x2_s = _mm_resid(ub_s, w_out_b[0], x1_s, gate_b, t_new, 0)

    y_p = _finalnorm(x2_p, g_final).reshape(bp, seq, d)
    y_s = _finalnorm(x2_s, g_final).reshape(bs, t_new, d)

    keep_b = min(LEFT_CHUNKS * CHUNK, seq)
    heads = lambda a, b, t: a.reshape(b, t, n_heads, HEAD_DIM)
    tail = lambda a: jnp.stack([a[(b + 1) * seq - keep_b:(b + 1) * seq] for b in range(bp)])
    return (y_p, y_s,
            heads(k_p, bp, seq)[None], heads(v_p, bp, seq)[None],
            heads(k_s, bs, t_new)[None], heads(v_s, bs, t_new)[None],
            heads(tail(kb), bp, keep_b), heads(tail(vb), bp, keep_b),
            heads(kb[mp:], bs, t_new), heads(vb[mp:], bs, t_new))
```

```python
import functools
import math

import jax
import jax.numpy as jnp
from jax import lax
from jax.experimental import pallas as pl
from jax.experimental.pallas import tpu as pltpu

F32 = jnp.float32
BF16 = jnp.bfloat16

HEAD_DIM = 128
CHUNK = 64
LEFT_CHUNKS = 8
REL_CLIP = 128
EPS = 1e-6
NEG_INF = -1e30

V7X_VMEM_LIMIT_BYTES = 54 * 1024 * 1024
LANES = 128
BF16_SUBLANES = 16

MM_TM = 512
MM_TM_NARROW_OUT = 1024
RESID_TM = 512
MM_TN = 1024
NORM_TM = 256
MOD_TN = 512
ATTN_T = 256
STICK_TQ = 256
STICK_HEADS = 8
BAND_HEADS = 4

F32_EXP_UNDERFLOW_LOG = -104.0
SAMPLE_HEADS = 8
CACHE_SLOTS = 4


def _params(n_grid):
    return pltpu.CompilerParams(
        dimension_semantics=("arbitrary",) * n_grid,
        vmem_limit_bytes=V7X_VMEM_LIMIT_BYTES)


def _silu(x):
    return x / (1.0 + jnp.exp(-x))


def _dot(a, b):
    return jnp.dot(a, b, preferred_element_type=F32)


def _dot_nt(a, b):
    return lax.dot_general(a, b, (((1,), (1,)), ((), ())), preferred_element_type=F32)


def _mod_kernel(c_ref, wa_ref, ba_ref, wb_ref, bb_ref, oa_ref, ob_ref):
    a = _silu(c_ref[...]).astype(BF16)
    oa_ref[...] = _dot(a, wa_ref[...].astype(BF16)) + ba_ref[...]
    ob_ref[...] = _dot(a, wb_ref[...].astype(BF16)) + bb_ref[...]


def _adaln_tables(c_all, w_a, b_a, w_b, b_b):
    rows, d = c_all.shape
    n = w_a.shape[1]
    w_spec = pl.BlockSpec((d, MOD_TN), lambda j: (0, j))
    b_spec = pl.BlockSpec((1, MOD_TN), lambda j: (0, j))
    o_spec = pl.BlockSpec((rows, MOD_TN), lambda j: (0, j))
    return pl.pallas_call(
        _mod_kernel,
        out_shape=(jax.ShapeDtypeStruct((rows, n), F32),) * 2,
        grid=(n // MOD_TN,),
        in_specs=[pl.BlockSpec((rows, d), lambda j: (0, 0)), w_spec, b_spec, w_spec, b_spec],
        out_specs=(o_spec, o_spec),
        compiler_params=_params(1),
        name="adaln_mod",
    )(c_all, w_a, b_a.reshape(1, n), w_b, b_b.reshape(1, n))


def _split_mod(mod, d):
    rows = mod.shape[0]
    return tuple(mod[:, k * d:(k + 1) * d].reshape(rows, 1, d) for k in range(3))


def _rms_scale(x):
    return lax.rsqrt(jnp.mean(x * x, axis=-1, keepdims=True) + EPS)


def _modulate(y, shift_ref, scale_ref):
    nb = shift_ref.shape[0]
    tm, d = y.shape
    y3 = y.reshape(nb, tm // nb, d)
    return (y3 * (1.0 + scale_ref[...]) + shift_ref[...]).reshape(tm, d)


def _prenorm_kernel(xp_ref, xs_ref, g_ref, shp_ref, scp_ref, shs_ref, scs_ref, o_ref, *, n_prompt_tiles):
    i = pl.program_id(0)

    def run(x_ref, shift_ref, scale_ref):
        x = x_ref[...]
        y = x * _rms_scale(x) * g_ref[...]
        o_ref[...] = _modulate(y, shift_ref, scale_ref).astype(o_ref.dtype)

    @pl.when(i < n_prompt_tiles)
    def _():
        run(xp_ref, shp_ref, scp_ref)

    @pl.when(i >= n_prompt_tiles)
    def _():
        run(xs_ref, shs_ref, scs_ref)


def _group_specs(tm, d, n_prompt_tiles, prompt_rpb, sample_rpb, n_sample_batches):
    tiles_per_prompt_batch = prompt_rpb // tm
    nb_s = tm // sample_rpb
    rows_p = pl.BlockSpec((tm, d), lambda i: (jnp.minimum(i, n_prompt_tiles - 1), 0))
    rows_s = pl.BlockSpec((tm, d), lambda i: (jnp.maximum(i - n_prompt_tiles, 0), 0))
    tab_p = pl.BlockSpec(
        (1, 1, d),
        lambda i: (n_sample_batches + jnp.minimum(i, n_prompt_tiles - 1) // tiles_per_prompt_batch, 0, 0))
    tab_s = pl.BlockSpec((nb_s, 1, d), lambda i: (jnp.maximum(i - n_prompt_tiles, 0), 0, 0))
    return rows_p, rows_s, tab_p, tab_s


def _prenorm(x_p, x_s, g, shift, scale, prompt_rpb, sample_rpb):
    mp, d = x_p.shape
    ms = x_s.shape[0]
    tm = NORM_TM
    npt = mp // tm
    rows_p, rows_s, tab_p, tab_s = _group_specs(tm, d, npt, prompt_rpb, sample_rpb, ms // sample_rpb)
    return pl.pallas_call(
        functools.partial(_prenorm_kernel, n_prompt_tiles=npt),
        out_shape=jax.ShapeDtypeStruct((mp + ms, d), BF16),
        grid=((mp + ms) // tm,),
        in_specs=[rows_p, rows_s, pl.BlockSpec((1, d), lambda i: (0, 0)), tab_p, tab_p, tab_s, tab_s],
        out_specs=pl.BlockSpec((tm, d), lambda i: (i, 0)),
        compiler_params=_params(1),
        name="prenorm",
    )(x_p, x_s, g.reshape(1, d), shift, scale, shift, scale)


def _dualnorm_kernel(xp_ref, xs_ref, gkv_ref, gb_ref, shp_ref, scp_ref, shs_ref, scs_ref,
                     okv_ref, ob_ref, *, n_prompt_tiles):
    i = pl.program_id(0)

    def run(x_ref, shift_ref, scale_ref):
        x = x_ref[...]
        y = x * _rms_scale(x)
        okv_ref[...] = (y * gkv_ref[...]).astype(okv_ref.dtype)
        ob_ref[...] = _modulate(y * gb_ref[...], shift_ref, scale_ref).astype(ob_ref.dtype)

    @pl.when(i < n_prompt_tiles)
    def _():
        run(xp_ref, shp_ref, scp_ref)

    @pl.when(i >= n_prompt_tiles)
    def _():
        run(xs_ref, shs_ref, scs_ref)


def _dualnorm(x_p, x_s, g_kv, g_b, shift, scale, prompt_rpb, sample_rpb):
    mp, d = x_p.shape
    ms = x_s.shape[0]
    tm = NORM_TM
    npt = mp // tm
    rows_p, rows_s, tab_p, tab_s = _group_specs(tm, d, npt, prompt_rpb, sample_rpb, ms // sample_rpb)
    g_spec = pl.BlockSpec((1, d), lambda i: (0, 0))
    o_spec = pl.BlockSpec((tm, d), lambda i: (i, 0))
    return pl.pallas_call(
        functools.partial(_dualnorm_kernel, n_prompt_tiles=npt),
        out_shape=(jax.ShapeDtypeStruct((mp + ms, d), BF16),) * 2,
        grid=((mp + ms) // tm,),
        in_specs=[rows_p, rows_s, g_spec, g_spec, tab_p, tab_p, tab_s, tab_s],
        out_specs=(o_spec, o_spec),
        compiler_params=_params(1),
        name="dualnorm",
    )(x_p, x_s, g_kv.reshape(1, d), g_b.reshape(1, d), shift, scale, shift, scale)


def _finalnorm_kernel(x_ref, g_ref, o_ref):
    x = x_ref[...]
    o_ref[...] = x * _rms_scale(x) * g_ref[...]


def _finalnorm(x, g):
    m, d = x.shape
    tm = NORM_TM
    return pl.pallas_call(
        _finalnorm_kernel,
        out_shape=jax.ShapeDtypeStruct((m, d), F32),
        grid=(m // tm,),
        in_specs=[pl.BlockSpec((tm, d), lambda i: (i, 0)), pl.BlockSpec((1, d), lambda i: (0, 0))],
        out_specs=pl.BlockSpec((tm, d), lambda i: (i, 0)),
        compiler_params=_params(1),
        name="finalnorm",
    )(x, g.reshape(1, d))


def _stage_weight_tile(w_hbm, stage_ref, wbf_ref, sem, col0):
    j, nj = pl.program_id(0), pl.num_programs(0)
    tn = wbf_ref.shape[1]

    def tile_copy(jj):
        return pltpu.make_async_copy(w_hbm.at[:, pl.ds(col0 + jj * tn, tn)], stage_ref, sem)

    @pl.when(pl.program_id(1) == 0)
    def _():
        @pl.when(j == 0)
        def _():
            tile_copy(0).start()

        tile_copy(j).wait()
        wbf_ref[...] = stage_ref[...].astype(BF16)

        @pl.when(j + 1 < nj)
        def _():
            tile_copy(j + 1).start()


def _weight_scratch(k, tn):
    return [pltpu.VMEM((k, tn), F32), pltpu.VMEM((k, tn), BF16), pltpu.SemaphoreType.DMA(())]


def _mm_kernel(h_ref, w_hbm, *refs, silu, scale, col0):
    *o_refs, stage_ref, wbf_ref, sem = refs
    _stage_weight_tile(w_hbm, stage_ref, wbf_ref, sem, col0)
    acc = _dot(h_ref[...], wbf_ref[...])
    if silu:
        acc = _silu(acc)
    if scale is not None:
        acc = acc * scale
    for o_ref in o_refs:
        o_ref[...] = acc.astype(o_ref.dtype)


def _mm(h, w, col0, n_out, out_dtypes, *, silu=False, scale=None, row0=0, rows=None):
    k = h.shape[1]
    rows = h.shape[0] if rows is None else rows
    tm = MM_TM if F32 in out_dtypes else MM_TM_NARROW_OUT
    tn = MM_TN
    i0 = row0 // tm
    o_spec = pl.BlockSpec((tm, tn), lambda j, i: (i, j))
    return pl.pallas_call(
        functools.partial(_mm_kernel, silu=silu, scale=scale, col0=col0),
        out_shape=tuple(jax.ShapeDtypeStruct((rows, n_out), dt) for dt in out_dtypes),
        grid=(n_out // tn, rows // tm),
        in_specs=[pl.BlockSpec((tm, k), lambda j, i: (i0 + i, 0)),
                  pl.BlockSpec(memory_space=pl.ANY)],
        out_specs=tuple(o_spec for _ in out_dtypes),
        scratch_shapes=_weight_scratch(k, tn),
        compiler_params=_params(2),
        name="proj",
    )(h, w)


def _mm_resid_kernel(u_ref, w_hbm, x_ref, gate_ref, o_ref, stage_ref, wbf_ref, sem):
    _stage_weight_tile(w_hbm, stage_ref, wbf_ref, sem, 0)
    acc = _dot(u_ref[...], wbf_ref[...])
    nb = gate_ref.shape[0]
    tm, tn = acc.shape
    y = acc.reshape(nb, tm // nb, tn) * gate_ref[...]
    o_ref[...] = x_ref[...] + y.reshape(tm, tn)


def _mm_resid(u, w, x, gate, rows_per_batch, table_row0):
    m, k = u.shape
    n = w.shape[1]
    tm, tn = RESID_TM, MM_TN
    if rows_per_batch >= tm:
        nb = 1
        tiles_per_batch = rows_per_batch // tm
        gate_map = lambda j, i: (table_row0 + i // tiles_per_batch, 0, j)
    else:
        nb = tm // rows_per_batch
        blk0 = table_row0 // nb
        gate_map = lambda j, i: (blk0 + i, 0, j)
    return pl.pallas_call(
        _mm_resid_kernel,
        out_shape=jax.ShapeDtypeStruct((m, n), F32),
        grid=(n // tn, m // tm),
        in_specs=[pl.BlockSpec((tm, k), lambda j, i: (i, 0)),
                  pl.BlockSpec(memory_space=pl.ANY),
                  pl.BlockSpec((tm, tn), lambda j, i: (i, j)),
                  pl.BlockSpec((nb, 1, tn), gate_map)],
        out_specs=pl.BlockSpec((tm, tn), lambda j, i: (i, j)),
        scratch_shapes=_weight_scratch(k, tn),
        compiler_params=_params(2),
        name="out_proj",
    )(u, w, x, gate)


def _softplus(z):
    return jnp.maximum(z, 0.0) + jnp.log(1.0 + jnp.exp(-jnp.abs(z)))


def _suffix_sum(x, tri):
    hi = x.astype(BF16)
    lo = (x - hi.astype(F32)).astype(BF16)
    return _dot(hi, tri) + _dot(lo, tri)


def _stick_blocks(qs, kblks, vblks, tri, carries, accs, causal):
    zs = [_dot_nt(q, kblk) for q, kblk in zip(qs, kblks)]
    sps = [_softplus(z) if causal is None else jnp.where(causal, _softplus(z), 0.0) for z in zs]
    incls = [_suffix_sum(sp, tri) for sp in sps]
    ws = [jnp.exp(z - incl + carry) for z, incl, carry in zip(zs, incls, carries)]
    if causal is not None:
        ws = [jnp.where(causal, w, 0.0) for w in ws]
    accs = [acc + _dot(w.astype(BF16), vblk) for acc, w, vblk in zip(accs, ws, vblks)]
    carries = [carry - incl[:, :1] for carry, incl in zip(carries, incls)]
    return carries, accs


def _strict_lower(tq, tk):
    return lax.broadcasted_iota(jnp.int32, (tq, tk), 1) < lax.broadcasted_iota(jnp.int32, (tq, tk), 0)


def _all_weights_vanish(carry):
    return jnp.max(carry) <= F32_EXP_UNDERFLOW_LOG


def _stick_prompt_kernel(q_ref, k_ref, v_ref, sg_ref, tri_ref, o_ref, *, tk):
    qi = pl.program_id(2)
    tq = q_ref.shape[1]
    n_sub = tq // tk
    heads = range(q_ref.shape[2] // HEAD_DIM)
    lanes = [slice(hh * HEAD_DIM, (hh + 1) * HEAD_DIM) for hh in heads]
    qs = [q_ref[0, :, lanes[hh]] for hh in heads]
    tri = tri_ref[...]

    def kv(k0):
        k0 = pl.multiple_of(k0, tk)
        return ([k_ref[0, pl.ds(k0, tk), lanes[hh]] for hh in heads],
                [v_ref[0, pl.ds(k0, tk), lanes[hh]] for hh in heads])

    carries = [jnp.zeros((tq, 1), F32) for _ in heads]
    accs = [jnp.zeros((tq, HEAD_DIM), F32) for _ in heads]
    for sb in reversed(range(n_sub)):
        r0 = sb * tk
        kblks, vblks = kv(qi * tq + r0)
        c_sub, a_sub = _stick_blocks([q[r0:] for q in qs], kblks, vblks, tri, [c[r0:] for c in carries],
                                     [a[r0:] for a in accs], _strict_lower(tq - r0, tk))
        if r0:
            c_sub = [jnp.concatenate([c[:r0], cs], axis=0) for c, cs in zip(carries, c_sub)]
            a_sub = [jnp.concatenate([a[:r0], as_], axis=0) for a, as_ in zip(accs, a_sub)]
        carries, accs = c_sub, a_sub

    def cond(state):
        it, carries, _ = state
        return jnp.logical_and(
            it < qi * n_sub, jnp.logical_not(_all_weights_vanish(functools.reduce(jnp.maximum, carries))))

    def body(state):
        it, carries, accs = state
        kblks, vblks = kv((qi * n_sub - 1 - it) * tk)
        carries, accs = _stick_blocks(qs, kblks, vblks, tri, carries, accs, None)
        return it + 1, tuple(carries), tuple(accs)

    _, carries, accs = lax.while_loop(cond, body, (jnp.int32(0), tuple(carries), tuple(accs)))
    for hh in heads:
        o_ref[0, :, lanes[hh]] = (accs[hh] * sg_ref[0, :, lanes[hh]].astype(F32)).astype(o_ref.dtype)


def _tri_incl(t):
    return (jnp.arange(t)[:, None] >= jnp.arange(t)[None, :]).astype(BF16)


def _stick_prompt(q, k, v, sg, batch, seq):
    d = k.shape[1]
    tq, tk = STICK_TQ, ATTN_T
    tiles = seq // tq
    hw = STICK_HEADS * HEAD_DIM
    flat = lambda a: a.reshape(1, a.shape[0], d)
    blk_in = pl.BlockSpec((1, tq, hw), lambda b, h, i: (0, b * tiles + i, h))
    kv = pl.BlockSpec((1, seq, hw), lambda b, h, i: (0, b, h))
    out = pl.pallas_call(
        functools.partial(_stick_prompt_kernel, tk=tk),
        out_shape=jax.ShapeDtypeStruct((batch, seq, d), BF16),
        grid=(batch, d // hw, tiles),
        in_specs=[blk_in, kv, kv, blk_in, pl.BlockSpec((tk, tk), lambda b, h, i: (0, 0))],
        out_specs=pl.BlockSpec((1, tq, hw), lambda b, h, i: (b, i, h)),
        compiler_params=_params(3),
        name="stick_prompt",
    )(flat(q), flat(k), flat(v), flat(sg), _tri_incl(tk))
    return out.reshape(batch * seq, d)


def _head_rows(cache_ref, hh, t0, n):
    nh, hd = cache_ref.shape[-2:]
    flat = cache_ref.reshape(math.prod(cache_ref.shape[:-1]), hd)
    return flat[pl.ds(t0 * nh + hh, n, stride=nh), :]


def _stick_sample_kernel(q_ref, ck_hbm, cv_hbm, k_ref, v_ref, sg_ref, tri_ref, o_ref, kbuf, vbuf, sem):
    bi, g = pl.program_id(0), pl.program_id(1)
    ng = pl.num_programs(1)
    step = bi * ng + g
    parity = step % 2
    t_new = q_ref.shape[1]
    past = ck_hbm.shape[1]
    _, kblock, nh, _ = kbuf.shape
    n_blk = past // kblock
    heads = range(nh)
    lanes = [slice(hh * HEAD_DIM, (hh + 1) * HEAD_DIM) for hh in heads]
    tri = tri_ref[...]
    qs = [q_ref[0, :, lanes[hh]] for hh in heads]

    def block_copies(batch, group, blk, slot):
        rows = pl.ds(past - (blk + 1) * kblock, kblock)
        cols = pl.ds(group * nh, nh)
        return (pltpu.make_async_copy(ck_hbm.at[batch, rows, cols, :], kbuf.at[slot], sem.at[0, slot]),
                pltpu.make_async_copy(cv_hbm.at[batch, rows, cols, :], vbuf.at[slot], sem.at[1, slot]))

    def newest_copies(of_step):
        return block_copies(of_step // ng, of_step % ng, 0, of_step % 2)

    def older_copies(blk):
        return block_copies(bi, g, blk, 2 + blk % 2)

    def attend(slot, carries, accs):
        t0 = slot * kblock
        return _stick_blocks(
            qs, [_head_rows(kbuf, hh, t0, kblock).astype(BF16) for hh in heads],
            [_head_rows(vbuf, hh, t0, kblock).astype(BF16) for hh in heads], tri, carries, accs, None)

    def alive(carries):
        return jnp.logical_not(_all_weights_vanish(functools.reduce(jnp.maximum, carries)))

    @pl.when(step == 0)
    def _():
        for c in newest_copies(step):
            c.start()

    @pl.when(step + 1 < pl.num_programs(0) * ng)
    def _():
        for c in newest_copies(step + 1):
            c.start()

    carries, accs = _stick_blocks(
        qs, [k_ref[0, :, lanes[hh]].astype(BF16) for hh in heads],
        [v_ref[0, :, lanes[hh]].astype(BF16) for hh in heads], tri_ref[:t_new, :t_new],
        [jnp.zeros((t_new, 1), F32) for _ in heads], [jnp.zeros((t_new, HEAD_DIM), F32) for _ in heads],
        _strict_lower(t_new, t_new))

    for c in newest_copies(step):
        c.wait()
    carries, accs = attend(parity, carries, accs)

    more = jnp.logical_and(n_blk > 1, alive(carries))

    @pl.when(more)
    def _():
        for c in older_copies(1):
            c.start()

    def cond(state):
        it, carries, _ = state
        return jnp.logical_and(it < n_blk, alive(carries))

    def body(state):
        it, carries, accs = state
        for c in older_copies(it):
            c.wait()

        @pl.when(it + 1 < n_blk)
        def _():
            for c in older_copies(it + 1):
                c.start()

        carries, accs = attend(2 + it % 2, carries, accs)
        return it + 1, tuple(carries), tuple(accs)

    done, carries, accs = lax.while_loop(cond, body, (jnp.int32(1), tuple(carries), tuple(accs)))

    @pl.when(jnp.logical_and(more, done < n_blk))
    def _():
        for c in older_copies(done):
            c.wait()

    for hh in heads:
        o_ref[0, :, lanes[hh]] = (accs[hh] * sg_ref[0, :, lanes[hh]].astype(F32)).astype(o_ref.dtype)


def _stick_sample(q, cache_k, cache_v, k, v, sg, row0):
    b, past, n_heads, _ = cache_k.shape
    d = n_heads * HEAD_DIM
    t_new = k.shape[0] // b
    hw = SAMPLE_HEADS * HEAD_DIM
    t = ATTN_T
    blk0 = row0 // t_new
    q3 = q.reshape(q.shape[0] // t_new, t_new, d)
    sg3 = sg.reshape(q3.shape)
    merged = pl.BlockSpec((1, t_new, hw), lambda bi, g: (blk0 + bi, 0, g))
    new = pl.BlockSpec((1, t_new, hw), lambda bi, g: (bi, 0, g))
    cache = pl.BlockSpec(memory_space=pl.ANY)
    block_buf = pltpu.VMEM((CACHE_SLOTS, t, SAMPLE_HEADS, HEAD_DIM), F32)
    out = pl.pallas_call(
        _stick_sample_kernel,
        out_shape=jax.ShapeDtypeStruct((b, t_new, d), BF16),
        grid=(b, d // hw),
        in_specs=[merged, cache, cache, new, new, merged, pl.BlockSpec((t, t), lambda bi, g: (0, 0))],
        out_specs=new,
        scratch_shapes=[block_buf, block_buf, pltpu.SemaphoreType.DMA((2, CACHE_SLOTS))],
        compiler_params=_params(2),
        name="stick_sample",
    )(q3, cache_k, cache_v, k.reshape(b, t_new, d), v.reshape(b, t_new, d), sg3, _tri_incl(t))
    return out.reshape(b * t_new, d)


def _band_prompt_kernel(q_ref, sg_ref, bias_ref, *refs):
    qi = pl.program_id(2)
    t = q_ref.shape[1]
    n_kb = bias_ref.shape[2] // t
    k_refs, v_refs, o_ref = refs[:n_kb], refs[n_kb:2 * n_kb], refs[2 * n_kb]
    heads = range(q_ref.shape[2] // HEAD_DIM)
    lanes = [slice(hh * HEAD_DIM, (hh + 1) * HEAD_DIM) for hh in heads]
    exists = [qi - (n_kb - 1) + dd >= 0 for dd in range(n_kb)]
    scores = [[jnp.where(exists[dd], _dot_nt(q_ref[0, :, lanes[hh]], k_refs[dd][0, :, lanes[hh]])
                         + bias_ref[hh, :, dd * t:(dd + 1) * t], NEG_INF)
               for dd in range(n_kb)] for hh in heads]
    maxes = [functools.reduce(jnp.maximum, [s.max(axis=-1, keepdims=True) for s in scores[hh]]) for hh in heads]
    probs = [[jnp.exp(s - maxes[hh]) for s in scores[hh]] for hh in heads]
    accs = [sum(_dot(p.astype(BF16), v_refs[dd][0, :, lanes[hh]]) for dd, p in enumerate(probs[hh]))
            for hh in heads]
    for hh in heads:
        denom = sum(p.sum(axis=-1, keepdims=True) for p in probs[hh])
        o_ref[0, :, lanes[hh]] = (accs[hh] / denom * sg_ref[0, :, lanes[hh]].astype(F32)).astype(o_ref.dtype)


def _toeplitz(values_at, rows, cols):
    period = rows + cols
    j = jnp.arange(period)
    v = values_at(jnp.where(j < cols, j, j - period))
    flat = jnp.tile(v, (1,) * (v.ndim - 1) + (rows,))[..., :rows * (period - 1)]
    return flat.reshape(v.shape[:-1] + (rows, period - 1))[..., :cols]


def _band_bias_prompt(rel_bias, t):
    n_kb = (LEFT_CHUNKS * CHUNK) // t + 1
    back = (n_kb - 1) * t
    rb = rel_bias.astype(F32)
    bias = _toeplitz(lambda m: rb[:, jnp.clip(back - m, -REL_CLIP, REL_CLIP) + REL_CLIP], t, n_kb * t)
    r = jnp.arange(t)[:, None]
    c = jnp.arange(n_kb * t)[None, :]
    chunk_diff = (back + r) // CHUNK - c // CHUNK
    visible = (chunk_diff >= 0) & (chunk_diff <= LEFT_CHUNKS)
    return jnp.where(visible[None], bias, NEG_INF)


def _band_prompt(q, k, v, sg, rel_bias, batch, seq):
    d = k.shape[1]
    t = ATTN_T
    tiles = seq // t
    hw = BAND_HEADS * HEAD_DIM
    bias = _band_bias_prompt(rel_bias, t)
    n_kb = bias.shape[2] // t
    blk_in = pl.BlockSpec((1, t, hw), lambda b, h, i: (0, b * tiles + i, h))
    kv = [pl.BlockSpec((1, t, hw), functools.partial(
        lambda b, h, i, back: (0, b * tiles + jnp.maximum(i - back, 0), h), back=n_kb - 1 - dd))
        for dd in range(n_kb)]
    flat = lambda a: a.reshape(1, a.shape[0], d)
    out = pl.pallas_call(
        _band_prompt_kernel,
        out_shape=jax.ShapeDtypeStruct((batch, seq, d), BF16),
        grid=(batch, d // hw, tiles),
        in_specs=[blk_in, blk_in, pl.BlockSpec((BAND_HEADS,) + bias.shape[1:], lambda b, h, i: (h, 0, 0))]
        + kv + kv,
        out_specs=pl.BlockSpec((1, t, hw), lambda b, h, i: (b, i, h)),
        compiler_params=_params(3),
        name="band_prompt",
    )(flat(q), flat(sg), bias, *([flat(k)] * n_kb), *([flat(v)] * n_kb))
    return out.reshape(batch * seq, d)


def _band_sample_kernel(q_ref, k_ref, v_ref, sg_ref, bias_ref, ck_ref, cv_ref, o_ref):
    past, nh = ck_ref.shape[1:3]
    heads = range(nh)
    lanes = [slice(hh * HEAD_DIM, (hh + 1) * HEAD_DIM) for hh in heads]

    def keys(hh):
        return [(_head_rows(ck_ref, hh, 0, past).astype(BF16), slice(0, past)),
                (k_ref[0, :, lanes[hh]].astype(BF16), slice(past, None))]

    def values(hh):
        return [_head_rows(cv_ref, hh, 0, past).astype(BF16), v_ref[0, :, lanes[hh]].astype(BF16)]

    scores = [[_dot_nt(q_ref[0, :, lanes[hh]], kblk) + bias_ref[hh, :, cols] for kblk, cols in keys(hh)]
              for hh in heads]
    maxes = [functools.reduce(jnp.maximum, [s.max(axis=-1, keepdims=True) for s in scores[hh]]) for hh in heads]
    probs = [[jnp.exp(s - maxes[hh]) for s in scores[hh]] for hh in heads]
    accs = [sum(_dot(p.astype(BF16), vblk) for p, vblk in zip(probs[hh], values(hh))) for hh in heads]
    for hh in heads:
        denom = sum(p.sum(axis=-1, keepdims=True) for p in probs[hh])
        o_ref[0, :, lanes[hh]] = (accs[hh] / denom * sg_ref[0, :, lanes[hh]].astype(F32)).astype(o_ref.dtype)


def _band_bias_sample(rel_bias, past_total, past_b, t_new):
    rb = rel_bias.astype(F32)
    bias = _toeplitz(lambda m: rb[:, jnp.clip(past_b - m, -REL_CLIP, REL_CLIP) + REL_CLIP],
                     t_new, past_b + t_new)
    q_pos = past_total + jnp.arange(t_new)
    k_pos = past_total - past_b + jnp.arange(past_b + t_new)
    q_chunk = q_pos // CHUNK
    k_chunk = k_pos // CHUNK
    visible = ((k_pos[None, :] >= 0) & (k_chunk[None, :] <= q_chunk[:, None])
               & (k_chunk[None, :] >= q_chunk[:, None] - LEFT_CHUNKS))
    return jnp.where(visible[None], bias, NEG_INF)


def _band_sample(q, cache_k, cache_v, k, v, sg, rel_bias, row0, past_total):
    b, past_b, n_heads, _ = cache_k.shape
    d = n_heads * HEAD_DIM
    t_new = (q.shape[0] - row0) // b
    hw = SAMPLE_HEADS * HEAD_DIM
    blk0 = row0 // t_new
    bias = _band_bias_sample(rel_bias, past_total, past_b, t_new)
    r3 = lambda a: a.reshape(a.shape[0] // t_new, t_new, d)
    merged = pl.BlockSpec((1, t_new, hw), lambda g, bi: (blk0 + bi, 0, g))
    new = pl.BlockSpec((1, t_new, hw), lambda g, bi: (bi, 0, g))
    cache = pl.BlockSpec((1, past_b, SAMPLE_HEADS, HEAD_DIM), lambda g, bi: (bi, 0, g, 0))
    out = pl.pallas_call(
        _band_sample_kernel,
        out_shape=jax.ShapeDtypeStruct((b, t_new, d), BF16),
        grid=(d // hw, b),
        in_specs=[merged, new, new, merged,
                  pl.BlockSpec((SAMPLE_HEADS,) + bias.shape[1:], lambda g, bi: (g, 0, 0)), cache, cache],
        out_specs=new,
        compiler_params=_params(2),
        name="band_sample",
    )(r3(q), r3(k), r3(v), r3(sg), bias, cache_k, cache_v)
    return out.reshape(b * t_new, d)


def kernel(x_prompt, x_sample, c_prompt, c_sample, cache_a_k, cache_a_v, cache_b_k, cache_b_v, w_mod_a, b_mod_a, g_norm_a, w_in_a, w_out_a, g_kv, w_kv, w_mod_b, b_mod_b, g_norm_b, w_in_b, rel_bias_b, w_out_b, g_final):
    bp, seq, d = x_prompt.shape
    bs, t_new, _ = x_sample.shape
    n_heads = d // HEAD_DIM
    past = cache_a_k.shape[2]
    past_b = cache_b_k.shape[1]
    mp, ms = bp * seq, bs * t_new
    assert w_mod_a.shape[0] == 1 and w_mod_b.shape[0] == 1, "one layer of each mixer"
    assert seq % MM_TM == 0 and mp % MM_TM == 0 and ms % MM_TM == 0 and MM_TM % t_new == 0
    assert (mp + ms) % MM_TM_NARROW_OUT == 0
    assert seq % RESID_TM == 0 and ms % RESID_TM == 0 and RESID_TM % t_new == 0 and d % MM_TN == 0
    assert bs % (RESID_TM // t_new) == 0 and bs % (NORM_TM // t_new) == 0
    assert seq % STICK_TQ == 0 and STICK_TQ % ATTN_T == 0
    assert seq % ATTN_T == 0 and past % ATTN_T == 0 and (LEFT_CHUNKS * CHUNK) % ATTN_T == 0

    xp = x_prompt.reshape(mp, d)
    xs = x_sample.reshape(ms, d)

    n_c = bs + bp
    pad = -n_c % BF16_SUBLANES
    c_all = jnp.concatenate([c_sample, c_prompt, jnp.zeros((pad, d), F32)], axis=0)
    mod_a, mod_b = _adaln_tables(c_all, w_mod_a[0], b_mod_a[0], w_mod_b[0], b_mod_b[0])
    shift_a, scale_a, gate_a = _split_mod(mod_a, d)
    shift_b, scale_b, gate_b = _split_mod(mod_b, d)

    h_a = _prenorm(xp, xs, g_norm_a[0], shift_a, scale_a, seq, t_new)
    w_in = w_in_a[0]
    q_scale = 1.0 / math.sqrt(HEAD_DIM)
    q_a, = _mm(h_a, w_in, 0, d, (BF16,), scale=q_scale)
    k_p, k_p16 = _mm(h_a, w_in, d, d, (F32, BF16), row0=0, rows=mp)
    k_s, = _mm(h_a, w_in, d, d, (F32,), row0=mp, rows=ms)
    v_p, v_p16 = _mm(h_a, w_in, 2 * d, d, (F32, BF16), row0=0, rows=mp)
    v_s, = _mm(h_a, w_in, 2 * d, d, (F32,), row0=mp, rows=ms)
    sg_a, = _mm(h_a, w_in, 3 * d, d, (BF16,), silu=True)

    u_p = _stick_prompt(q_a, k_p16, v_p16, sg_a, bp, seq)
    u_s = _stick_sample(q_a, cache_a_k[0], cache_a_v[0], k_s, v_s, sg_a, mp)
    x1_p = _mm_resid(u_p, w_out_a[0], xp, gate_a, seq, bs)
    x1_s = _mm_resid(u_s, w_out_a[0], xs, gate_a, t_new, 0)

    h_kv, h_b = _dualnorm(x1_p, x1_s, g_kv, g_norm_b[0], shift_b, scale_b, seq, t_new)
    kb_p, kb_p16 = _mm(h_kv, w_kv, 0, d, (F32, BF16), row0=0, rows=mp)
    kb_s, kb_s16 = _mm(h_kv, w_kv, 0, d, (F32, BF16), row0=mp, rows=ms)
    vb_p, vb_p16 = _mm(h_kv, w_kv, d, d, (F32, BF16), row0=0, rows=mp)
    vb_s, vb_s16 = _mm(h_kv, w_kv, d, d, (F32, BF16), row0=mp, rows=ms)
    q_b, = _mm(h_b, w_in_b[0], 0, d, (BF16,), scale=q_scale)
    sg_b, = _mm(h_b, w_in_b[0], d, d, (BF16,), silu=True)

    ub_p = _band_prompt(q_b, kb_p16, vb_p16, sg_b, rel_bias_b[0], bp, seq)
    ub_s = _band_sample(q_b, cache_b_k, cache_b_v, kb_s16, vb_s16, sg_b, rel_bias_b[0], mp, past)
    x2_p = _mm_resid(ub_p, w_out_b[0], x1_p, gate_b, seq, bs)
    x2_s = _mm_resid(ub_s, w_out_b[0], x1_s, gate_b, t_new, 0)

    y_p = _finalnorm(x2_p, g_final).reshape(bp, seq, d)
    y_s = _finalnorm(x2_s, g_final).reshape(bs, t_new, d)

    keep_b = min(LEFT_CHUNKS * CHUNK, seq)
    heads = lambda a, b, t: a.reshape(b, t, n_heads, HEAD_DIM)
    tail = lambda a: jnp.stack([a[(b + 1) * seq - keep_b:(b + 1) * seq] for b in range(bp)])
    return (y_p, y_s,
            heads(k_p, bp, seq)[None], heads(v_p, bp, seq)[None],
            heads(k_s, bs, t_new)[None], heads(v_s, bs, t_new)[None],
            heads(tail(kb_p), bp, keep_b), heads(tail(vb_p), bp, keep_b),
            heads(kb_s, bs, t_new), heads(vb_s, bs, t_new))
```

```python
import functools
import math

import jax
import jax.numpy as jnp
from jax import lax
from jax.experimental import pallas as pl
from jax.experimental.pallas import tpu as pltpu

F32 = jnp.float32
BF16 = jnp.bfloat16

HEAD_DIM = 128
CHUNK = 64
LEFT_CHUNKS = 8
REL_CLIP = 128
EPS = 1e-6
NEG_INF = -1e30

V7X_VMEM_LIMIT_BYTES = 54 * 1024 * 1024
LANES = 128
BF16_SUBLANES = 16

MM_TM = 512
MM_TM_NARROW_OUT = 1024
RESID_TM = 512
MM_TN = 1024
NORM_TM = 256
MOD_TN = 512
ATTN_T = 256
STICK_TQ = 256
STICK_HEADS = 8
BAND_HEADS = 4

F32_EXP_UNDERFLOW_LOG = -104.0
SAMPLE_HEADS = 8
CACHE_SLOTS = 4


def _params(n_grid):
    return pltpu.CompilerParams(
        dimension_semantics=("arbitrary",) * n_grid,
        vmem_limit_bytes=V7X_VMEM_LIMIT_BYTES)


def _silu(x):
    return x / (1.0 + jnp.exp(-x))


def _dot(a, b):
    return jnp.dot(a, b, preferred_element_type=F32)


def _dot_nt(a, b):
    return lax.dot_general(a, b, (((1,), (1,)), ((), ())), preferred_element_type=F32)


def _mod_kernel(c_ref, wa_ref, ba_ref, wb_ref, bb_ref, oa_ref, ob_ref):
    a = _silu(c_ref[...]).astype(BF16)
    oa_ref[...] = _dot(a, wa_ref[...].astype(BF16)) + ba_ref[...]
    ob_ref[...] = _dot(a, wb_ref[...].astype(BF16)) + bb_ref[...]


def _adaln_tables(c_all, w_a, b_a, w_b, b_b):
    rows, d = c_all.shape
    n = w_a.shape[1]
    w_spec = pl.BlockSpec((d, MOD_TN), lambda j: (0, j))
    b_spec = pl.BlockSpec((1, MOD_TN), lambda j: (0, j))
    o_spec = pl.BlockSpec((rows, MOD_TN), lambda j: (0, j))
    return pl.pallas_call(
        _mod_kernel,
        out_shape=(jax.ShapeDtypeStruct((rows, n), F32),) * 2,
        grid=(n // MOD_TN,),
        in_specs=[pl.BlockSpec((rows, d), lambda j: (0, 0)), w_spec, b_spec, w_spec, b_spec],
        out_specs=(o_spec, o_spec),
        compiler_params=_params(1),
        name="adaln_mod",
    )(c_all, w_a, b_a.reshape(1, n), w_b, b_b.reshape(1, n))


def _split_mod(mod, d):
    rows = mod.shape[0]
    return tuple(mod[:, k * d:(k + 1) * d].reshape(rows, 1, d) for k in range(3))


def _rms_scale(x):
    return lax.rsqrt(jnp.mean(x * x, axis=-1, keepdims=True) + EPS)


def _modulate(y, shift_ref, scale_ref):
    nb = shift_ref.shape[0]
    tm, d = y.shape
    y3 = y.reshape(nb, tm // nb, d)
    return (y3 * (1.0 + scale_ref[...]) + shift_ref[...]).reshape(tm, d)


def _prenorm_kernel(xp_ref, xs_ref, g_ref, shp_ref, scp_ref, shs_ref, scs_ref, o_ref, *, n_prompt_tiles):
    i = pl.program_id(0)

    def run(x_ref, shift_ref, scale_ref):
        x = x_ref[...]
        y = x * _rms_scale(x) * g_ref[...]
        o_ref[...] = _modulate(y, shift_ref, scale_ref).astype(o_ref.dtype)

    @pl.when(i < n_prompt_tiles)
    def _():
        run(xp_ref, shp_ref, scp_ref)

    @pl.when(i >= n_prompt_tiles)
    def _():
        run(xs_ref, shs_ref, scs_ref)


def _group_specs(tm, d, n_prompt_tiles, prompt_rpb, sample_rpb, n_sample_batches):
    tiles_per_prompt_batch = prompt_rpb // tm
    nb_s = tm // sample_rpb
    rows_p = pl.BlockSpec((tm, d), lambda i: (jnp.minimum(i, n_prompt_tiles - 1), 0))
    rows_s = pl.BlockSpec((tm, d), lambda i: (jnp.maximum(i - n_prompt_tiles, 0), 0))
    tab_p = pl.BlockSpec(
        (1, 1, d),
        lambda i: (n_sample_batches + jnp.minimum(i, n_prompt_tiles - 1) // tiles_per_prompt_batch, 0, 0))
    tab_s = pl.BlockSpec((nb_s, 1, d), lambda i: (jnp.maximum(i - n_prompt_tiles, 0), 0, 0))
    return rows_p, rows_s, tab_p, tab_s


def _prenorm(x_p, x_s, g, shift, scale, prompt_rpb, sample_rpb):
    mp, d = x_p.shape
    ms = x_s.shape[0]
    tm = NORM_TM
    npt = mp // tm
    rows_p, rows_s, tab_p, tab_s = _group_specs(tm, d, npt, prompt_rpb, sample_rpb, ms // sample_rpb)
    return pl.pallas_call(
        functools.partial(_prenorm_kernel, n_prompt_tiles=npt),
        out_shape=jax.ShapeDtypeStruct((mp + ms, d), BF16),
        grid=((mp + ms) // tm,),
        in_specs=[rows_p, rows_s, pl.BlockSpec((1, d), lambda i: (0, 0)), tab_p, tab_p, tab_s, tab_s],
        out_specs=pl.BlockSpec((tm, d), lambda i: (i, 0)),
        compiler_params=_params(1),
        name="prenorm",
    )(x_p, x_s, g.reshape(1, d), shift, scale, shift, scale)


def _dualnorm_kernel(xp_ref, xs_ref, gkv_ref, gb_ref, shp_ref, scp_ref, shs_ref, scs_ref,
                     okv_ref, ob_ref, *, n_prompt_tiles):
    i = pl.program_id(0)

    def run(x_ref, shift_ref, scale_ref):
        x = x_ref[...]
        y = x * _rms_scale(x)
        okv_ref[...] = (y * gkv_ref[...]).astype(okv_ref.dtype)
        ob_ref[...] = _modulate(y * gb_ref[...], shift_ref, scale_ref).astype(ob_ref.dtype)

    @pl.when(i < n_prompt_tiles)
    def _():
        run(xp_ref, shp_ref, scp_ref)

    @pl.when(i >= n_prompt_tiles)
    def _():
        run(xs_ref, shs_ref, scs_ref)


def _dualnorm(x_p, x_s, g_kv, g_b, shift, scale, prompt_rpb, sample_rpb):
    mp, d = x_p.shape
    ms = x_s.shape[0]
    tm = NORM_TM
    npt = mp // tm
    rows_p, rows_s, tab_p, tab_s = _group_specs(tm, d, npt, prompt_rpb, sample_rpb, ms // sample_rpb)
    g_spec = pl.BlockSpec((1, d), lambda i: (0, 0))
    o_spec = pl.BlockSpec((tm, d), lambda i: (i, 0))
    return pl.pallas_call(
        functools.partial(_dualnorm_kernel, n_prompt_tiles=npt),
        out_shape=(jax.ShapeDtypeStruct((mp + ms, d), BF16),) * 2,
        grid=((mp + ms) // tm,),
        in_specs=[rows_p, rows_s, g_spec, g_spec, tab_p, tab_p, tab_s, tab_s],
        out_specs=(o_spec, o_spec),
        compiler_params=_params(1),
        name="dualnorm",
    )(x_p, x_s, g_kv.reshape(1, d), g_b.reshape(1, d), shift, scale, shift, scale)


def _finalnorm_kernel(x_ref, g_ref, o_ref):
    x = x_ref[...]
    o_ref[...] = x * _rms_scale(x) * g_ref[...]


def _finalnorm(x, g):
    m, d = x.shape
    tm = NORM_TM
    return pl.pallas_call(
        _finalnorm_kernel,
        out_shape=jax.ShapeDtypeStruct((m, d), F32),
        grid=(m // tm,),
        in_specs=[pl.BlockSpec((tm, d), lambda i: (i, 0)), pl.BlockSpec((1, d), lambda i: (0, 0))],
        out_specs=pl.BlockSpec((tm, d), lambda i: (i, 0)),
        compiler_params=_params(1),
        name="finalnorm",
    )(x, g.reshape(1, d))


def _stage_weight_tile(w_hbm, stage_ref, wbf_ref, sem, col0):
    j, nj = pl.program_id(0), pl.num_programs(0)
    tn = wbf_ref.shape[1]

    def tile_copy(jj):
        return pltpu.make_async_copy(w_hbm.at[:, pl.ds(col0 + jj * tn, tn)], stage_ref, sem)

    @pl.when(pl.program_id(1) == 0)
    def _():
        @pl.when(j == 0)
        def _():
            tile_copy(0).start()

        tile_copy(j).wait()
        wbf_ref[...] = stage_ref[...].astype(BF16)

        @pl.when(j + 1 < nj)
        def _():
            tile_copy(j + 1).start()


def _weight_scratch(k, tn):
    return [pltpu.VMEM((k, tn), F32), pltpu.VMEM((k, tn), BF16), pltpu.SemaphoreType.DMA(())]


def _mm_kernel(h_ref, w_hbm, *refs, silu, scale, col0):
    *o_refs, stage_ref, wbf_ref, sem = refs
    _stage_weight_tile(w_hbm, stage_ref, wbf_ref, sem, col0)
    acc = _dot(h_ref[...], wbf_ref[...])
    if silu:
        acc = _silu(acc)
    if scale is not None:
        acc = acc * scale
    for o_ref in o_refs:
        o_ref[...] = acc.astype(o_ref.dtype)


def _mm(h, w, col0, n_out, out_dtypes, *, silu=False, scale=None, row0=0, rows=None):
    k = h.shape[1]
    rows = h.shape[0] if rows is None else rows
    tm = MM_TM if F32 in out_dtypes else MM_TM_NARROW_OUT
    tn = MM_TN
    i0 = row0 // tm
    o_spec = pl.BlockSpec((tm, tn), lambda j, i: (i, j))
    return pl.pallas_call(
        functools.partial(_mm_kernel, silu=silu, scale=scale, col0=col0),
        out_shape=tuple(jax.ShapeDtypeStruct((rows, n_out), dt) for dt in out_dtypes),
        grid=(n_out // tn, rows // tm),
        in_specs=[pl.BlockSpec((tm, k), lambda j, i: (i0 + i, 0)),
                  pl.BlockSpec(memory_space=pl.ANY)],
        out_specs=tuple(o_spec for _ in out_dtypes),
        scratch_shapes=_weight_scratch(k, tn),
        compiler_params=_params(2),
        name="proj",
    )(h, w)


def _mm_resid_kernel(u_ref, w_hbm, x_ref, gate_ref, o_ref, stage_ref, wbf_ref, sem):
    _stage_weight_tile(w_hbm, stage_ref, wbf_ref, sem, 0)
    acc = _dot(u_ref[...], wbf_ref[...])
    nb = gate_ref.shape[0]
    tm, tn = acc.shape
    y = acc.reshape(nb, tm // nb, tn) * gate_ref[...]
    o_ref[...] = x_ref[...] + y.reshape(tm, tn)


def _mm_resid(u, w, x, gate, rows_per_batch, table_row0):
    m, k = u.shape
    n = w.shape[1]
    tm, tn = RESID_TM, MM_TN
    if rows_per_batch >= tm:
        nb = 1
        tiles_per_batch = rows_per_batch // tm
        gate_map = lambda j, i: (table_row0 + i // tiles_per_batch, 0, j)
    else:
        nb = tm // rows_per_batch
        blk0 = table_row0 // nb
        gate_map = lambda j, i: (blk0 + i, 0, j)
    return pl.pallas_call(
        _mm_resid_kernel,
        out_shape=jax.ShapeDtypeStruct((m, n), F32),
        grid=(n // tn, m // tm),
        in_specs=[pl.BlockSpec((tm, k), lambda j, i: (i, 0)),
                  pl.BlockSpec(memory_space=pl.ANY),
                  pl.BlockSpec((tm, tn), lambda j, i: (i, j)),
                  pl.BlockSpec((nb, 1, tn), gate_map)],
        out_specs=pl.BlockSpec((tm, tn), lambda j, i: (i, j)),
        scratch_shapes=_weight_scratch(k, tn),
        compiler_params=_params(2),
        name="out_proj",
    )(u, w, x, gate)


def _softplus(z):
    return jnp.maximum(z, 0.0) + jnp.log(1.0 + jnp.exp(-jnp.abs(z)))


def _suffix_sum(x, tri):
    hi = x.astype(BF16)
    lo = (x - hi.astype(F32)).astype(BF16)
    return _dot(hi, tri) + _dot(lo, tri)


def _stick_blocks(qs, kblks, vblks, tri, carries, accs, causal):
    zs = [_dot_nt(q, kblk) for q, kblk in zip(qs, kblks)]
    sps = [_softplus(z) if causal is None else jnp.where(causal, _softplus(z), 0.0) for z in zs]
    incls = [_suffix_sum(sp, tri) for sp in sps]
    ws = [jnp.exp(z - incl + carry) for z, incl, carry in zip(zs, incls, carries)]
    if causal is not None:
        ws = [jnp.where(causal, w, 0.0) for w in ws]
    accs = [acc + _dot(w.astype(BF16), vblk) for acc, w, vblk in zip(accs, ws, vblks)]
    carries = [carry - incl[:, :1] for carry, incl in zip(carries, incls)]
    return carries, accs


def _strict_lower(tq, tk):
    return lax.broadcasted_iota(jnp.int32, (tq, tk), 1) < lax.broadcasted_iota(jnp.int32, (tq, tk), 0)


def _all_weights_vanish(carry):
    return jnp.max(carry) <= F32_EXP_UNDERFLOW_LOG


def _stick_prompt_kernel(q_ref, k_ref, v_ref, sg_ref, tri_ref, o_ref, *, tk):
    qi = pl.program_id(2)
    tq = q_ref.shape[1]
    n_sub = tq // tk
    heads = range(q_ref.shape[2] // HEAD_DIM)
    lanes = [slice(hh * HEAD_DIM, (hh + 1) * HEAD_DIM) for hh in heads]
    qs = [q_ref[0, :, lanes[hh]] for hh in heads]
    tri = tri_ref[...]

    def kv(k0):
        k0 = pl.multiple_of(k0, tk)
        return ([k_ref[0, pl.ds(k0, tk), lanes[hh]] for hh in heads],
                [v_ref[0, pl.ds(k0, tk), lanes[hh]] for hh in heads])

    carries = [jnp.zeros((tq, 1), F32) for _ in heads]
    accs = [jnp.zeros((tq, HEAD_DIM), F32) for _ in heads]
    for sb in reversed(range(n_sub)):
        r0 = sb * tk
        kblks, vblks = kv(qi * tq + r0)
        c_sub, a_sub = _stick_blocks([q[r0:] for q in qs], kblks, vblks, tri, [c[r0:] for c in carries],
                                     [a[r0:] for a in accs], _strict_lower(tq - r0, tk))
        if r0:
            c_sub = [jnp.concatenate([c[:r0], cs], axis=0) for c, cs in zip(carries, c_sub)]
            a_sub = [jnp.concatenate([a[:r0], as_], axis=0) for a, as_ in zip(accs, a_sub)]
        carries, accs = c_sub, a_sub

    def cond(state):
        it, carries, _ = state
        return jnp.logical_and(
            it < qi * n_sub, jnp.logical_not(_all_weights_vanish(functools.reduce(jnp.maximum, carries))))

    def body(state):
        it, carries, accs = state
        kblks, vblks = kv((qi * n_sub - 1 - it) * tk)
        carries, accs = _stick_blocks(qs, kblks, vblks, tri, carries, accs, None)
        return it + 1, tuple(carries), tuple(accs)

    _, carries, accs = lax.while_loop(cond, body, (jnp.int32(0), tuple(carries), tuple(accs)))
    for hh in heads:
        o_ref[0, :, lanes[hh]] = (accs[hh] * sg_ref[0, :, lanes[hh]].astype(F32)).astype(o_ref.dtype)


def _tri_incl(t):
    return (jnp.arange(t)[:, None] >= jnp.arange(t)[None, :]).astype(BF16)


def _stick_prompt(q, k, v, sg, batch, seq):
    d = k.shape[1]
    tq, tk = STICK_TQ, ATTN_T
    tiles = seq // tq
    hw = STICK_HEADS * HEAD_DIM
    flat = lambda a: a.reshape(1, a.shape[0], d)
    blk_in = pl.BlockSpec((1, tq, hw), lambda b, h, i: (0, b * tiles + i, h))
    kv = pl.BlockSpec((1, seq, hw), lambda b, h, i: (0, b, h))
    out = pl.pallas_call(
        functools.partial(_stick_prompt_kernel, tk=tk),
        out_shape=jax.ShapeDtypeStruct((batch, seq, d), BF16),
        grid=(batch, d // hw, tiles),
        in_specs=[blk_in, kv, kv, blk_in, pl.BlockSpec((tk, tk), lambda b, h, i: (0, 0))],
        out_specs=pl.BlockSpec((1, tq, hw), lambda b, h, i: (b, i, h)),
        compiler_params=_params(3),
        name="stick_prompt",
    )(flat(q), flat(k), flat(v), flat(sg), _tri_incl(tk))
    return out.reshape(batch * seq, d)


def _head_rows(cache_ref, hh, t0, n):
    nh, hd = cache_ref.shape[-2:]
    flat = cache_ref.reshape(math.prod(cache_ref.shape[:-1]), hd)
    return flat[pl.ds(t0 * nh + hh, n, stride=nh), :]


def _stick_sample_kernel(q_ref, ck_hbm, cv_hbm, k_ref, v_ref, sg_ref, tri_ref, o_ref, kbuf, vbuf, sem):
    bi, g = pl.program_id(0), pl.program_id(1)
    ng = pl.num_programs(1)
    step = bi * ng + g
    parity = step % 2
    t_new = q_ref.shape[1]
    past = ck_hbm.shape[1]
    _, kblock, nh, _ = kbuf.shape
    n_blk = past // kblock
    heads = range(nh)
    lanes = [slice(hh * HEAD_DIM, (hh + 1) * HEAD_DIM) for hh in heads]
    tri = tri_ref[...]
    qs = [q_ref[0, :, lanes[hh]] for hh in heads]

    def block_copies(batch, group, blk, slot):
        rows = pl.ds(past - (blk + 1) * kblock, kblock)
        cols = pl.ds(group * nh, nh)
        return (pltpu.make_async_copy(ck_hbm.at[batch, rows, cols, :], kbuf.at[slot], sem.at[0, slot]),
                pltpu.make_async_copy(cv_hbm.at[batch, rows, cols, :], vbuf.at[slot], sem.at[1, slot]))

    def newest_copies(of_step):
        return block_copies(of_step // ng, of_step % ng, 0, of_step % 2)

    def older_copies(blk):
        return block_copies(bi, g, blk, 2 + blk % 2)

    def attend(slot, carries, accs):
        t0 = slot * kblock
        return _stick_blocks(
            qs, [_head_rows(kbuf, hh, t0, kblock).astype(BF16) for hh in heads],
            [_head_rows(vbuf, hh, t0, kblock).astype(BF16) for hh in heads], tri, carries, accs, None)

    def alive(carries):
        return jnp.logical_not(_all_weights_vanish(functools.reduce(jnp.maximum, carries)))

    @pl.when(step == 0)
    def _():
        for c in newest_copies(step):
            c.start()

    @pl.when(step + 1 < pl.num_programs(0) * ng)
    def _():
        for c in newest_copies(step + 1):
            c.start()

    carries, accs = _stick_blocks(
        qs, [k_ref[0, :, lanes[hh]].astype(BF16) for hh in heads],
        [v_ref[0, :, lanes[hh]].astype(BF16) for hh in heads], tri_ref[:t_new, :t_new],
        [jnp.zeros((t_new, 1), F32) for _ in heads], [jnp.zeros((t_new, HEAD_DIM), F32) for _ in heads],
        _strict_lower(t_new, t_new))

    for c in newest_copies(step):
        c.wait()
    carries, accs = attend(parity, carries, accs)

    more = jnp.logical_and(n_blk > 1, alive(carries))

    @pl.when(more)
    def _():
        for c in older_copies(1):
            c.start()

    def cond(state):
        it, carries, _ = state
        return jnp.logical_and(it < n_blk, alive(carries))

    def body(state):
        it, carries, accs = state
        for c in older_copies(it):
            c.wait()

        @pl.when(it + 1 < n_blk)
        def _():
            for c in older_copies(it + 1):
                c.start()

        carries, accs = attend(2 + it % 2, carries, accs)
        return it + 1, tuple(carries), tuple(accs)

    done, carries, accs = lax.while_loop(cond, body, (jnp.int32(1), tuple(carries), tuple(accs)))

    @pl.when(jnp.logical_and(more, done < n_blk))
    def _():
        for c in older_copies(done):
            c.wait()

    for hh in heads:
        o_ref[0, :, lanes[hh]] = (accs[hh] * sg_ref[0, :, lanes[hh]].astype(F32)).astype(o_ref.dtype)


def _stick_sample(q, cache_k, cache_v, k, v, sg, row0):
    b, past, n_heads, _ = cache_k.shape
    d = n_heads * HEAD_DIM
    t_new = k.shape[0] // b
    hw = SAMPLE_HEADS * HEAD_DIM
    t = ATTN_T
    blk0 = row0 // t_new
    q3 = q.reshape(q.shape[0] // t_new, t_new, d)
    sg3 = sg.reshape(q3.shape)
    merged = pl.BlockSpec((1, t_new, hw), lambda bi, g: (blk0 + bi, 0, g))
    new = pl.BlockSpec((1, t_new, hw), lambda bi, g: (bi, 0, g))
    cache = pl.BlockSpec(memory_space=pl.ANY)
    block_buf = pltpu.VMEM((CACHE_SLOTS, t, SAMPLE_HEADS, HEAD_DIM), F32)
    out = pl.pallas_call(
        _stick_sample_kernel,
        out_shape=jax.ShapeDtypeStruct((b, t_new, d), BF16),
        grid=(b, d // hw),
        in_specs=[merged, cache, cache, new, new, merged, pl.BlockSpec((t, t), lambda bi, g: (0, 0))],
        out_specs=new,
        scratch_shapes=[block_buf, block_buf, pltpu.SemaphoreType.DMA((2, CACHE_SLOTS))],
        compiler_params=_params(2),
        name="stick_sample",
    )(q3, cache_k, cache_v, k.reshape(b, t_new, d), v.reshape(b, t_new, d), sg3, _tri_incl(t))
    return out.reshape(b * t_new, d)


def _band_prompt_kernel(q_ref, sg_ref, rel_ref, *refs):
    qi = pl.program_id(2)
    t = q_ref.shape[1]
    n_kb = rel_ref.shape[2] // t - 1
    k_refs, v_refs, o_ref, bias_ref = refs[:n_kb], refs[n_kb:2 * n_kb], refs[2 * n_kb], refs[2 * n_kb + 1]
    heads = range(q_ref.shape[2] // HEAD_DIM)
    lanes = [slice(hh * HEAD_DIM, (hh + 1) * HEAD_DIM) for hh in heads]

    @pl.when(qi == 0)
    def _():
        r = lax.broadcasted_iota(jnp.int32, (t, n_kb * t), 0)
        c = lax.broadcasted_iota(jnp.int32, (t, n_kb * t), 1)
        chunk_diff = lax.div((n_kb - 1) * t + r, CHUNK) - lax.div(c, CHUNK)
        visible = jnp.logical_and(chunk_diff >= 0, chunk_diff <= LEFT_CHUNKS)
        for hh in heads:
            table = jnp.broadcast_to(rel_ref[hh], (t, (n_kb + 1) * t))
            rolled = pltpu.roll(table, 0, 1, stride=1, stride_axis=0)
            bias_ref[hh] = jnp.where(visible, rolled[:, t:], NEG_INF)

    exists = [qi - (n_kb - 1) + dd >= 0 for dd in range(n_kb)]
    scores = [[jnp.where(exists[dd], _dot_nt(q_ref[0, :, lanes[hh]], k_refs[dd][0, :, lanes[hh]])
                         + bias_ref[hh, :, dd * t:(dd + 1) * t], NEG_INF)
               for dd in range(n_kb)] for hh in heads]
    maxes = [functools.reduce(jnp.maximum, [s.max(axis=-1, keepdims=True) for s in scores[hh]]) for hh in heads]
    probs = [[jnp.exp(s - maxes[hh]) for s in scores[hh]] for hh in heads]
    accs = [sum(_dot(p.astype(BF16), v_refs[dd][0, :, lanes[hh]]) for dd, p in enumerate(probs[hh]))
            for hh in heads]
    for hh in heads:
        denom = sum(p.sum(axis=-1, keepdims=True) for p in probs[hh])
        o_ref[0, :, lanes[hh]] = (accs[hh] / denom * sg_ref[0, :, lanes[hh]].astype(F32)).astype(o_ref.dtype)


def _toeplitz(values_at, rows, cols):
    period = rows + cols
    j = jnp.arange(period)
    v = values_at(jnp.where(j < cols, j, j - period))
    flat = jnp.tile(v, (1,) * (v.ndim - 1) + (rows,))[..., :rows * (period - 1)]
    return flat.reshape(v.shape[:-1] + (rows, period - 1))[..., :cols]


def _band_rel_table(rel_bias, t, n_kb):
    back = (n_kb - 1) * t
    offset = jnp.arange((n_kb + 1) * t) - t
    idx = jnp.clip(back - offset, -REL_CLIP, REL_CLIP) + REL_CLIP
    return rel_bias.astype(F32)[:, None, idx]


def _band_prompt(q, k, v, sg, rel_bias, batch, seq):
    d = k.shape[1]
    t = ATTN_T
    tiles = seq // t
    hw = BAND_HEADS * HEAD_DIM
    n_kb = (LEFT_CHUNKS * CHUNK) // t + 1
    rel = _band_rel_table(rel_bias, t, n_kb)
    blk_in = pl.BlockSpec((1, t, hw), lambda b, h, i: (0, b * tiles + i, h))
    kv = [pl.BlockSpec((1, t, hw), functools.partial(
        lambda b, h, i, back: (0, b * tiles + jnp.maximum(i - back, 0), h), back=n_kb - 1 - dd))
        for dd in range(n_kb)]
    flat = lambda a: a.reshape(1, a.shape[0], d)
    out = pl.pallas_call(
        _band_prompt_kernel,
        out_shape=jax.ShapeDtypeStruct((batch, seq, d), BF16),
        grid=(batch, d // hw, tiles),
        in_specs=[blk_in, blk_in, pl.BlockSpec((BAND_HEADS,) + rel.shape[1:], lambda b, h, i: (h, 0, 0))]
        + kv + kv,
        out_specs=pl.BlockSpec((1, t, hw), lambda b, h, i: (b, i, h)),
        scratch_shapes=[pltpu.VMEM((BAND_HEADS, t, n_kb * t), F32)],
        compiler_params=_params(3),
        name="band_prompt",
    )(flat(q), flat(sg), rel, *([flat(k)] * n_kb), *([flat(v)] * n_kb))
    return out.reshape(batch * seq, d)


def _band_sample_kernel(q_ref, k_ref, v_ref, sg_ref, bias_ref, ck_ref, cv_ref, o_ref):
    past, nh = ck_ref.shape[1:3]
    heads = range(nh)
    lanes = [slice(hh * HEAD_DIM, (hh + 1) * HEAD_DIM) for hh in heads]

    def keys(hh):
        return [(_head_rows(ck_ref, hh, 0, past).astype(BF16), slice(0, past)),
                (k_ref[0, :, lanes[hh]].astype(BF16), slice(past, None))]

    def values(hh):
        return [_head_rows(cv_ref, hh, 0, past).astype(BF16), v_ref[0, :, lanes[hh]].astype(BF16)]

    scores = [[_dot_nt(q_ref[0, :, lanes[hh]], kblk) + bias_ref[hh, :, cols] for kblk, cols in keys(hh)]
              for hh in heads]
    maxes = [functools.reduce(jnp.maximum, [s.max(axis=-1, keepdims=True) for s in scores[hh]]) for hh in heads]
    probs = [[jnp.exp(s - maxes[hh]) for s in scores[hh]] for hh in heads]
    accs = [sum(_dot(p.astype(BF16), vblk) for p, vblk in zip(probs[hh], values(hh))) for hh in heads]
    for hh in heads:
        denom = sum(p.sum(axis=-1, keepdims=True) for p in probs[hh])
        o_ref[0, :, lanes[hh]] = (accs[hh] / denom * sg_ref[0, :, lanes[hh]].astype(F32)).astype(o_ref.dtype)


def _band_bias_sample(rel_bias, past_total, past_b, t_new):
    rb = rel_bias.astype(F32)
    bias = _toeplitz(lambda m: rb[:, jnp.clip(past_b - m, -REL_CLIP, REL_CLIP) + REL_CLIP],
                     t_new, past_b + t_new)
    q_pos = past_total + jnp.arange(t_new)
    k_pos = past_total - past_b + jnp.arange(past_b + t_new)
    q_chunk = q_pos // CHUNK
    k_chunk = k_pos // CHUNK
    visible = ((k_pos[None, :] >= 0) & (k_chunk[None, :] <= q_chunk[:, None])
               & (k_chunk[None, :] >= q_chunk[:, None] - LEFT_CHUNKS))
    return jnp.where(visible[None], bias, NEG_INF)


def _band_sample(q, cache_k, cache_v, k, v, sg, rel_bias, row0, past_total):
    b, past_b, n_heads, _ = cache_k.shape
    d = n_heads * HEAD_DIM
    t_new = (q.shape[0] - row0) // b
    hw = SAMPLE_HEADS * HEAD_DIM
    blk0 = row0 // t_new
    bias = _band_bias_sample(rel_bias, past_total, past_b, t_new)
    r3 = lambda a: a.reshape(a.shape[0] // t_new, t_new, d)
    merged = pl.BlockSpec((1, t_new, hw), lambda g, bi: (blk0 + bi, 0, g))
    new = pl.BlockSpec((1, t_new, hw), lambda g, bi: (bi, 0, g))
    cache = pl.BlockSpec((1, past_b, SAMPLE_HEADS, HEAD_DIM), lambda g, bi: (bi, 0, g, 0))
    out = pl.pallas_call(
        _band_sample_kernel,
        out_shape=jax.ShapeDtypeStruct((b, t_new, d), BF16),
        grid=(d // hw, b),
        in_specs=[merged, new, new, merged,
                  pl.BlockSpec((SAMPLE_HEADS,) + bias.shape[1:], lambda g, bi: (g, 0, 0)), cache, cache],
        out_specs=new,
        compiler_params=_params(2),
        name="band_sample",
    )(r3(q), r3(k), r3(v), r3(sg), bias, cache_k, cache_v)
    return out.reshape(b * t_new, d)


def kernel(x_prompt, x_sample, c_prompt, c_sample, cache_a_k, cache_a_v, cache_b_k, cache_b_v, w_mod_a, b_mod_a, g_norm_a, w_in_a, w_out_a, g_kv, w_kv, w_mod_b, b_mod_b, g_norm_b, w_in_b, rel_bias_b, w_out_b, g_final):
    bp, seq, d = x_prompt.shape
    bs, t_new, _ = x_sample.shape
    n_heads = d // HEAD_DIM
    past = cache_a_k.shape[2]
    past_b = cache_b_k.shape[1]
    mp, ms = bp * seq, bs * t_new
    assert w_mod_a.shape[0] == 1 and w_mod_b.shape[0] == 1, "one layer of each mixer"
    assert seq % MM_TM == 0 and mp % MM_TM == 0 and ms % MM_TM == 0 and MM_TM % t_new == 0
    assert (mp + ms) % MM_TM_NARROW_OUT == 0
    assert seq % RESID_TM == 0 and ms % RESID_TM == 0 and RESID_TM % t_new == 0 and d % MM_TN == 0
    assert bs % (RESID_TM // t_new) == 0 and bs % (NORM_TM // t_new) == 0
    assert seq % STICK_TQ == 0 and STICK_TQ % ATTN_T == 0
    assert seq % ATTN_T == 0 and past % ATTN_T == 0 and (LEFT_CHUNKS * CHUNK) % ATTN_T == 0

    xp = x_prompt.reshape(mp, d)
    xs = x_sample.reshape(ms, d)

    n_c = bs + bp
    pad = -n_c % BF16_SUBLANES
    c_all = jnp.concatenate([c_sample, c_prompt, jnp.zeros((pad, d), F32)], axis=0)
    mod_a, mod_b = _adaln_tables(c_all, w_mod_a[0], b_mod_a[0], w_mod_b[0], b_mod_b[0])
    shift_a, scale_a, gate_a = _split_mod(mod_a, d)
    shift_b, scale_b, gate_b = _split_mod(mod_b, d)

    h_a = _prenorm(xp, xs, g_norm_a[0], shift_a, scale_a, seq, t_new)
    w_in = w_in_a[0]
    q_scale = 1.0 / math.sqrt(HEAD_DIM)
    q_a, = _mm(h_a, w_in, 0, d, (BF16,), scale=q_scale)
    k_p, k_p16 = _mm(h_a, w_in, d, d, (F32, BF16), row0=0, rows=mp)
    k_s, = _mm(h_a, w_in, d, d, (F32,), row0=mp, rows=ms)
    v_p, v_p16 = _mm(h_a, w_in, 2 * d, d, (F32, BF16), row0=0, rows=mp)
    v_s, = _mm(h_a, w_in, 2 * d, d, (F32,), row0=mp, rows=ms)
    sg_a, = _mm(h_a, w_in, 3 * d, d, (BF16,), silu=True)

    u_p = _stick_prompt(q_a, k_p16, v_p16, sg_a, bp, seq)
    u_s = _stick_sample(q_a, cache_a_k[0], cache_a_v[0], k_s, v_s, sg_a, mp)
    x1_p = _mm_resid(u_p, w_out_a[0], xp, gate_a, seq, bs)
    x1_s = _mm_resid(u_s, w_out_a[0], xs, gate_a, t_new, 0)

    h_kv, h_b = _dualnorm(x1_p, x1_s, g_kv, g_norm_b[0], shift_b, scale_b, seq, t_new)
    kb_p, kb_p16 = _mm(h_kv, w_kv, 0, d, (F32, BF16), row0=0, rows=mp)
    kb_s, kb_s16 = _mm(h_kv, w_kv, 0, d, (F32, BF16), row0=mp, rows=ms)
    vb_p, vb_p16 = _mm(h_kv, w_kv, d, d, (F32, BF16), row0=0, rows=mp)
    vb_s, vb_s16 = _mm(h_kv, w_kv, d, d, (F32, BF16), row0=mp, rows=ms)
    q_b, = _mm(h_b, w_in_b[0], 0, d, (BF16,), scale=q_scale)
    sg_b, = _mm(h_b, w_in_b[0], d, d, (BF16,), silu=True)

    ub_p = _band_prompt(q_b, kb_p16, vb_p16, sg_b, rel_bias_b[0], bp, seq)
    ub_s = _band_sample(q_b, cache_b_k, cache_b_v, kb_s16, vb_s16, sg_b, rel_bias_b[0], mp, past)
    x2_p = _mm_resid(ub_p, w_out_b[0], x1_p, gate_b, seq, bs)
    x2_s = _mm_resid(ub_s, w_out_b[0], x1_s, gate_b, t_new, 0)

    y_p = _finalnorm(x2_p, g_final).reshape(bp, seq, d)
    y_s = _finalnorm(x2_s, g_final).reshape(bs, t_new, d)

    keep_b = min(LEFT_CHUNKS * CHUNK, seq)
    heads = lambda a, b, t: a.reshape(b, t, n_heads, HEAD_DIM)
    tail = lambda a: jnp.stack([a[(b + 1) * seq - keep_b:(b + 1) * seq] for b in range(bp)])
    return (y_p, y_s,
            heads(k_p, bp, seq)[None], heads(v_p, bp, seq)[None],
            heads(k_s, bs, t_new)[None], heads(v_s, bs, t_new)[None],
            heads(tail(kb_p), bp, keep_b), heads(tail(vb_p), bp, keep_b),
            heads(kb_s, bs, t_new), heads(vb_s, bs, t_new))
```

```python
import functools
import math

import jax
import jax.numpy as jnp
from jax import lax
from jax.experimental import pallas as pl
from jax.experimental.pallas import tpu as pltpu

F32 = jnp.float32
BF16 = jnp.bfloat16

HEAD_DIM = 128
CHUNK = 64
LEFT_CHUNKS = 8
REL_CLIP = 128
EPS = 1e-6
NEG_INF = -1e30

V7X_VMEM_LIMIT_BYTES = 54 * 1024 * 1024
LANES = 128
BF16_SUBLANES = 16

MM_TM = 512
MM_TM_NARROW_OUT = 1024
RESID_TM = 512
MM_TN = 1024
NORM_TM = 256
MOD_TN = 512
ATTN_T = 256
STICK_TQ = 256
STICK_HEADS = 8
BAND_HEADS = 4

F32_EXP_UNDERFLOW_LOG = -104.0
SAMPLE_HEADS = 8
CACHE_SLOTS = 4


def _params(n_grid):
    return pltpu.CompilerParams(
        dimension_semantics=("arbitrary",) * n_grid,
        vmem_limit_bytes=V7X_VMEM_LIMIT_BYTES)


def _silu(x):
    return x / (1.0 + jnp.exp(-x))


def _dot(a, b):
    return jnp.dot(a, b, preferred_element_type=F32)


def _dot_nt(a, b):
    return lax.dot_general(a, b, (((1,), (1,)), ((), ())), preferred_element_type=F32)


def _mod_kernel(c_ref, wa_ref, ba_ref, wb_ref, bb_ref, oa_ref, ob_ref):
    a = _silu(c_ref[...]).astype(BF16)
    oa_ref[...] = _dot(a, wa_ref[...].astype(BF16)) + ba_ref[...]
    ob_ref[...] = _dot(a, wb_ref[...].astype(BF16)) + bb_ref[...]


def _adaln_tables(c_all, w_a, b_a, w_b, b_b):
    rows, d = c_all.shape
    n = w_a.shape[1]
    w_spec = pl.BlockSpec((d, MOD_TN), lambda j: (0, j))
    b_spec = pl.BlockSpec((1, MOD_TN), lambda j: (0, j))
    o_spec = pl.BlockSpec((rows, MOD_TN), lambda j: (0, j))
    return pl.pallas_call(
        _mod_kernel,
        out_shape=(jax.ShapeDtypeStruct((rows, n), F32),) * 2,
        grid=(n // MOD_TN,),
        in_specs=[pl.BlockSpec((rows, d), lambda j: (0, 0)), w_spec, b_spec, w_spec, b_spec],
        out_specs=(o_spec, o_spec),
        compiler_params=_params(1),
        name="adaln_mod",
    )(c_all, w_a, b_a.reshape(1, n), w_b, b_b.reshape(1, n))


def _split_mod(mod, d):
    rows = mod.shape[0]
    return tuple(mod[:, k * d:(k + 1) * d].reshape(rows, 1, d) for k in range(3))


def _rms_scale(x):
    return lax.rsqrt(jnp.mean(x * x, axis=-1, keepdims=True) + EPS)


def _modulate(y, shift_ref, scale_ref):
    nb = shift_ref.shape[0]
    tm, d = y.shape
    y3 = y.reshape(nb, tm // nb, d)
    return (y3 * (1.0 + scale_ref[...]) + shift_ref[...]).reshape(tm, d)


def _prenorm_kernel(xp_ref, xs_ref, g_ref, shp_ref, scp_ref, shs_ref, scs_ref, o_ref, *, n_prompt_tiles):
    i = pl.program_id(0)

    def run(x_ref, shift_ref, scale_ref):
        x = x_ref[...]
        y = x * _rms_scale(x) * g_ref[...]
        o_ref[...] = _modulate(y, shift_ref, scale_ref).astype(o_ref.dtype)

    @pl.when(i < n_prompt_tiles)
    def _():
        run(xp_ref, shp_ref, scp_ref)

    @pl.when(i >= n_prompt_tiles)
    def _():
        run(xs_ref, shs_ref, scs_ref)


def _group_specs(tm, d, n_prompt_tiles, prompt_rpb, sample_rpb, n_sample_batches):
    tiles_per_prompt_batch = prompt_rpb // tm
    nb_s = tm // sample_rpb
    rows_p = pl.BlockSpec((tm, d), lambda i: (jnp.minimum(i, n_prompt_tiles - 1), 0))
    rows_s = pl.BlockSpec((tm, d), lambda i: (jnp.maximum(i - n_prompt_tiles, 0), 0))
    tab_p = pl.BlockSpec(
        (1, 1, d),
        lambda i: (n_sample_batches + jnp.minimum(i, n_prompt_tiles - 1) // tiles_per_prompt_batch, 0, 0))
    tab_s = pl.BlockSpec((nb_s, 1, d), lambda i: (jnp.maximum(i - n_prompt_tiles, 0), 0, 0))
    return rows_p, rows_s, tab_p, tab_s


def _prenorm(x_p, x_s, g, shift, scale, prompt_rpb, sample_rpb):
    mp, d = x_p.shape
    ms = x_s.shape[0]
    tm = NORM_TM
    npt = mp // tm
    rows_p, rows_s, tab_p, tab_s = _group_specs(tm, d, npt, prompt_rpb, sample_rpb, ms // sample_rpb)
    return pl.pallas_call(
        functools.partial(_prenorm_kernel, n_prompt_tiles=npt),
        out_shape=jax.ShapeDtypeStruct((mp + ms, d), BF16),
        grid=((mp + ms) // tm,),
        in_specs=[rows_p, rows_s, pl.BlockSpec((1, d), lambda i: (0, 0)), tab_p, tab_p, tab_s, tab_s],
        out_specs=pl.BlockSpec((tm, d), lambda i: (i, 0)),
        compiler_params=_params(1),
        name="prenorm",
    )(x_p, x_s, g.reshape(1, d), shift, scale, shift, scale)


def _dualnorm_kernel(xp_ref, xs_ref, gkv_ref, gb_ref, shp_ref, scp_ref, shs_ref, scs_ref,
                     okv_ref, ob_ref, *, n_prompt_tiles):
    i = pl.program_id(0)

    def run(x_ref, shift_ref, scale_ref):
        x = x_ref[...]
        y = x * _rms_scale(x)
        okv_ref[...] = (y * gkv_ref[...]).astype(okv_ref.dtype)
        ob_ref[...] = _modulate(y * gb_ref[...], shift_ref, scale_ref).astype(ob_ref.dtype)

    @pl.when(i < n_prompt_tiles)
    def _():
        run(xp_ref, shp_ref, scp_ref)

    @pl.when(i >= n_prompt_tiles)
    def _():
        run(xs_ref, shs_ref, scs_ref)


def _dualnorm(x_p, x_s, g_kv, g_b, shift, scale, prompt_rpb, sample_rpb):
    mp, d = x_p.shape
    ms = x_s.shape[0]
    tm = NORM_TM
    npt = mp // tm
    rows_p, rows_s, tab_p, tab_s = _group_specs(tm, d, npt, prompt_rpb, sample_rpb, ms // sample_rpb)
    g_spec = pl.BlockSpec((1, d), lambda i: (0, 0))
    o_spec = pl.BlockSpec((tm, d), lambda i: (i, 0))
    return pl.pallas_call(
        functools.partial(_dualnorm_kernel, n_prompt_tiles=npt),
        out_shape=(jax.ShapeDtypeStruct((mp + ms, d), BF16),) * 2,
        grid=((mp + ms) // tm,),
        in_specs=[rows_p, rows_s, g_spec, g_spec, tab_p, tab_p, tab_s, tab_s],
        out_specs=(o_spec, o_spec),
        compiler_params=_params(1),
        name="dualnorm",
    )(x_p, x_s, g_kv.reshape(1, d), g_b.reshape(1, d), shift, scale, shift, scale)


def _finalnorm_kernel(x_ref, g_ref, o_ref):
    x = x_ref[...]
    o_ref[...] = x * _rms_scale(x) * g_ref[...]


def _finalnorm(x, g):
    m, d = x.shape
    tm = NORM_TM
    return pl.pallas_call(
        _finalnorm_kernel,
        out_shape=jax.ShapeDtypeStruct((m, d), F32),
        grid=(m // tm,),
        in_specs=[pl.BlockSpec((tm, d), lambda i: (i, 0)), pl.BlockSpec((1, d), lambda i: (0, 0))],
        out_specs=pl.BlockSpec((tm, d), lambda i: (i, 0)),
        compiler_params=_params(1),
        name="finalnorm",
    )(x, g.reshape(1, d))


def _stage_weight_tile(w_hbm, stage_ref, wbf_ref, sem, col0):
    j, nj = pl.program_id(0), pl.num_programs(0)
    tn = wbf_ref.shape[1]

    def tile_copy(jj):
        return pltpu.make_async_copy(w_hbm.at[:, pl.ds(col0 + jj * tn, tn)], stage_ref, sem)

    @pl.when(pl.program_id(1) == 0)
    def _():
        @pl.when(j == 0)
        def _():
            tile_copy(0).start()

        tile_copy(j).wait()
        wbf_ref[...] = stage_ref[...].astype(BF16)

        @pl.when(j + 1 < nj)
        def _():
            tile_copy(j + 1).start()


def _weight_scratch(k, tn):
    return [pltpu.VMEM((k, tn), F32), pltpu.VMEM((k, tn), BF16), pltpu.SemaphoreType.DMA(())]


def _mm_kernel(h_ref, w_hbm, *refs, silu, scale, col0):
    *o_refs, stage_ref, wbf_ref, sem = refs
    _stage_weight_tile(w_hbm, stage_ref, wbf_ref, sem, col0)
    acc = _dot(h_ref[...], wbf_ref[...])
    if silu:
        acc = _silu(acc)
    if scale is not None:
        acc = acc * scale
    for o_ref in o_refs:
        o_ref[...] = acc.astype(o_ref.dtype)


def _mm(h, w, col0, n_out, out_dtypes, *, silu=False, scale=None, row0=0, rows=None):
    k = h.shape[1]
    rows = h.shape[0] if rows is None else rows
    tm = MM_TM if F32 in out_dtypes else MM_TM_NARROW_OUT
    tn = MM_TN
    i0 = row0 // tm
    o_spec = pl.BlockSpec((tm, tn), lambda j, i: (i, j))
    return pl.pallas_call(
        functools.partial(_mm_kernel, silu=silu, scale=scale, col0=col0),
        out_shape=tuple(jax.ShapeDtypeStruct((rows, n_out), dt) for dt in out_dtypes),
        grid=(n_out // tn, rows // tm),
        in_specs=[pl.BlockSpec((tm, k), lambda j, i: (i0 + i, 0)),
                  pl.BlockSpec(memory_space=pl.ANY)],
        out_specs=tuple(o_spec for _ in out_dtypes),
        scratch_shapes=_weight_scratch(k, tn),
        compiler_params=_params(2),
        name="proj",
    )(h, w)


def _mm_resid_kernel(u_ref, w_hbm, x_ref, gate_ref, o_ref, stage_ref, wbf_ref, sem):
    _stage_weight_tile(w_hbm, stage_ref, wbf_ref, sem, 0)
    acc = _dot(u_ref[...], wbf_ref[...])
    nb = gate_ref.shape[0]
    tm, tn = acc.shape
    y = acc.reshape(nb, tm // nb, tn) * gate_ref[...]
    o_ref[...] = x_ref[...] + y.reshape(tm, tn)


def _mm_resid(u, w, x, gate, rows_per_batch, table_row0):
    m, k = u.shape
    n = w.shape[1]
    tm, tn = RESID_TM, MM_TN
    if rows_per_batch >= tm:
        nb = 1
        tiles_per_batch = rows_per_batch // tm
        gate_map = lambda j, i: (table_row0 + i // tiles_per_batch, 0, j)
    else:
        nb = tm // rows_per_batch
        blk0 = table_row0 // nb
        gate_map = lambda j, i: (blk0 + i, 0, j)
    return pl.pallas_call(
        _mm_resid_kernel,
        out_shape=jax.ShapeDtypeStruct((m, n), F32),
        grid=(n // tn, m // tm),
        in_specs=[pl.BlockSpec((tm, k), lambda j, i: (i, 0)),
                  pl.BlockSpec(memory_space=pl.ANY),
                  pl.BlockSpec((tm, tn), lambda j, i: (i, j)),
                  pl.BlockSpec((nb, 1, tn), gate_map)],
        out_specs=pl.BlockSpec((tm, tn), lambda j, i: (i, j)),
        scratch_shapes=_weight_scratch(k, tn),
        compiler_params=_params(2),
        name="out_proj",
    )(u, w, x, gate)


def _softplus(z):
    return jnp.maximum(z, 0.0) + jnp.log(1.0 + jnp.exp(-jnp.abs(z)))


def _suffix_sum(x, tri):
    hi = x.astype(BF16)
    lo = (x - hi.astype(F32)).astype(BF16)
    return _dot(hi, tri) + _dot(lo, tri)


def _stick_blocks(qs, kblks, vblks, tri, carries, accs, causal):
    zs = [_dot_nt(q, kblk) for q, kblk in zip(qs, kblks)]
    sps = [_softplus(z) if causal is None else jnp.where(causal, _softplus(z), 0.0) for z in zs]
    incls = [_suffix_sum(sp, tri) for sp in sps]
    ws = [jnp.exp(z - incl + carry) for z, incl, carry in zip(zs, incls, carries)]
    if causal is not None:
        ws = [jnp.where(causal, w, 0.0) for w in ws]
    accs = [acc + _dot(w.astype(BF16), vblk) for acc, w, vblk in zip(accs, ws, vblks)]
    carries = [carry - incl[:, :1] for carry, incl in zip(carries, incls)]
    return carries, accs


def _strict_lower(tq, tk):
    return lax.broadcasted_iota(jnp.int32, (tq, tk), 1) < lax.broadcasted_iota(jnp.int32, (tq, tk), 0)


def _all_weights_vanish(carry):
    return jnp.max(carry) <= F32_EXP_UNDERFLOW_LOG


def _stick_prompt_kernel(q_ref, k_ref, v_ref, sg_ref, tri_ref, o_ref, *, tk):
    qi = pl.program_id(2)
    tq = q_ref.shape[1]
    n_sub = tq // tk
    heads = range(q_ref.shape[2] // HEAD_DIM)
    lanes = [slice(hh * HEAD_DIM, (hh + 1) * HEAD_DIM) for hh in heads]
    qs = [q_ref[0, :, lanes[hh]] for hh in heads]
    tri = tri_ref[...]

    def kv(k0):
        k0 = pl.multiple_of(k0, tk)
        return ([k_ref[0, pl.ds(k0, tk), lanes[hh]] for hh in heads],
                [v_ref[0, pl.ds(k0, tk), lanes[hh]] for hh in heads])

    carries = [jnp.zeros((tq, 1), F32) for _ in heads]
    accs = [jnp.zeros((tq, HEAD_DIM), F32) for _ in heads]
    for sb in reversed(range(n_sub)):
        r0 = sb * tk
        kblks, vblks = kv(qi * tq + r0)
        c_sub, a_sub = _stick_blocks([q[r0:] for q in qs], kblks, vblks, tri, [c[r0:] for c in carries],
                                     [a[r0:] for a in accs], _strict_lower(tq - r0, tk))
        if r0:
            c_sub = [jnp.concatenate([c[:r0], cs], axis=0) for c, cs in zip(carries, c_sub)]
            a_sub = [jnp.concatenate([a[:r0], as_], axis=0) for a, as_ in zip(accs, a_sub)]
        carries, accs = c_sub, a_sub

    def cond(state):
        it, carries, _ = state
        return jnp.logical_and(
            it < qi * n_sub, jnp.logical_not(_all_weights_vanish(functools.reduce(jnp.maximum, carries))))

    def body(state):
        it, carries, accs = state
        kblks, vblks = kv((qi * n_sub - 1 - it) * tk)
        carries, accs = _stick_blocks(qs, kblks, vblks, tri, carries, accs, None)
        return it + 1, tuple(carries), tuple(accs)

    _, carries, accs = lax.while_loop(cond, body, (jnp.int32(0), tuple(carries), tuple(accs)))
    for hh in heads:
        o_ref[0, :, lanes[hh]] = (accs[hh] * sg_ref[0, :, lanes[hh]].astype(F32)).astype(o_ref.dtype)


def _tri_incl(t):
    return (jnp.arange(t)[:, None] >= jnp.arange(t)[None, :]).astype(BF16)


def _stick_prompt(q, k, v, sg, batch, seq):
    d = k.shape[1]
    tq, tk = STICK_TQ, ATTN_T
    tiles = seq // tq
    hw = STICK_HEADS * HEAD_DIM
    flat = lambda a: a.reshape(1, a.shape[0], d)
    blk_in = pl.BlockSpec((1, tq, hw), lambda b, h, i: (0, b * tiles + i, h))
    kv = pl.BlockSpec((1, seq, hw), lambda b, h, i: (0, b, h))
    out = pl.pallas_call(
        functools.partial(_stick_prompt_kernel, tk=tk),
        out_shape=jax.ShapeDtypeStruct((batch, seq, d), BF16),
        grid=(batch, d // hw, tiles),
        in_specs=[blk_in, kv, kv, blk_in, pl.BlockSpec((tk, tk), lambda b, h, i: (0, 0))],
        out_specs=pl.BlockSpec((1, tq, hw), lambda b, h, i: (b, i, h)),
        compiler_params=_params(3),
        name="stick_prompt",
    )(flat(q), flat(k), flat(v), flat(sg), _tri_incl(tk))
    return out.reshape(batch * seq, d)


def _head_rows(cache_ref, hh, t0, n):
    nh, hd = cache_ref.shape[-2:]
    flat = cache_ref.reshape(math.prod(cache_ref.shape[:-1]), hd)
    return flat[pl.ds(t0 * nh + hh, n, stride=nh), :]


def _stick_sample_kernel(q_ref, ck_hbm, cv_hbm, k_ref, v_ref, sg_ref, tri_ref, o_ref, kbuf, vbuf, sem):
    bi, g = pl.program_id(0), pl.program_id(1)
    ng = pl.num_programs(1)
    step = bi * ng + g
    parity = step % 2
    t_new = q_ref.shape[1]
    past = ck_hbm.shape[1]
    _, kblock, nh, _ = kbuf.shape
    n_blk = past // kblock
    heads = range(nh)
    lanes = [slice(hh * HEAD_DIM, (hh + 1) * HEAD_DIM) for hh in heads]
    tri = tri_ref[...]
    qs = [q_ref[0, :, lanes[hh]] for hh in heads]

    def block_copies(batch, group, blk, slot):
        rows = pl.ds(past - (blk + 1) * kblock, kblock)
        cols = pl.ds(group * nh, nh)
        return (pltpu.make_async_copy(ck_hbm.at[batch, rows, cols, :], kbuf.at[slot], sem.at[0, slot]),
                pltpu.make_async_copy(cv_hbm.at[batch, rows, cols, :], vbuf.at[slot], sem.at[1, slot]))

    def newest_copies(of_step):
        return block_copies(of_step // ng, of_step % ng, 0, of_step % 2)

    def older_copies(blk):
        return block_copies(bi, g, blk, 2 + blk % 2)

    def attend(slot, carries, accs):
        t0 = slot * kblock
        return _stick_blocks(
            qs, [_head_rows(kbuf, hh, t0, kblock).astype(BF16) for hh in heads],
            [_head_rows(vbuf, hh, t0, kblock).astype(BF16) for hh in heads], tri, carries, accs, None)

    def alive(carries):
        return jnp.logical_not(_all_weights_vanish(functools.reduce(jnp.maximum, carries)))

    @pl.when(step == 0)
    def _():
        for c in newest_copies(step):
            c.start()

    @pl.when(step + 1 < pl.num_programs(0) * ng)
    def _():
        for c in newest_copies(step + 1):
            c.start()

    carries, accs = _stick_blocks(
        qs, [k_ref[0, :, lanes[hh]].astype(BF16) for hh in heads],
        [v_ref[0, :, lanes[hh]].astype(BF16) for hh in heads], tri_ref[:t_new, :t_new],
        [jnp.zeros((t_new, 1), F32) for _ in heads], [jnp.zeros((t_new, HEAD_DIM), F32) for _ in heads],
        _strict_lower(t_new, t_new))

    for c in newest_copies(step):
        c.wait()
    carries, accs = attend(parity, carries, accs)

    more = jnp.logical_and(n_blk > 1, alive(carries))

    @pl.when(more)
    def _():
        for c in older_copies(1):
            c.start()

    def cond(state):
        it, carries, _ = state
        return jnp.logical_and(it < n_blk, alive(carries))

    def body(state):
        it, carries, accs = state
        for c in older_copies(it):
            c.wait()

        @pl.when(it + 1 < n_blk)
        def _():
            for c in older_copies(it + 1):
                c.start()

        carries, accs = attend(2 + it % 2, carries, accs)
        return it + 1, tuple(carries), tuple(accs)

    done, carries, accs = lax.while_loop(cond, body, (jnp.int32(1), tuple(carries), tuple(accs)))

    @pl.when(jnp.logical_and(more, done < n_blk))
    def _():
        for c in older_copies(done):
            c.wait()

    for hh in heads:
        o_ref[0, :, lanes[hh]] = (accs[hh] * sg_ref[0, :, lanes[hh]].astype(F32)).astype(o_ref.dtype)


def _stick_sample(q, cache_k, cache_v, k, v, sg, row0):
    b, past, n_heads, _ = cache_k.shape
    d = n_heads * HEAD_DIM
    t_new = k.shape[0] // b
    hw = SAMPLE_HEADS * HEAD_DIM
    t = ATTN_T
    blk0 = row0 // t_new
    q3 = q.reshape(q.shape[0] // t_new, t_new, d)
    sg3 = sg.reshape(q3.shape)
    merged = pl.BlockSpec((1, t_new, hw), lambda bi, g: (blk0 + bi, 0, g))
    new = pl.BlockSpec((1, t_new, hw), lambda bi, g: (bi, 0, g))
    cache = pl.BlockSpec(memory_space=pl.ANY)
    block_buf = pltpu.VMEM((CACHE_SLOTS, t, SAMPLE_HEADS, HEAD_DIM), F32)
    out = pl.pallas_call(
        _stick_sample_kernel,
        out_shape=jax.ShapeDtypeStruct((b, t_new, d), BF16),
        grid=(b, d // hw),
        in_specs=[merged, cache, cache, new, new, merged, pl.BlockSpec((t, t), lambda bi, g: (0, 0))],
        out_specs=new,
        scratch_shapes=[block_buf, block_buf, pltpu.SemaphoreType.DMA((2, CACHE_SLOTS))],
        compiler_params=_params(2),
        name="stick_sample",
    )(q3, cache_k, cache_v, k.reshape(b, t_new, d), v.reshape(b, t_new, d), sg3, _tri_incl(t))
    return out.reshape(b * t_new, d)


def _band_prompt_kernel(q_ref, sg_ref, rel_ref, *refs):
    qi = pl.program_id(2)
    t = q_ref.shape[1]
    n_kb = rel_ref.shape[2] // t - 1
    k_refs, v_refs, o_ref, bias_ref = refs[:n_kb], refs[n_kb:2 * n_kb], refs[2 * n_kb], refs[2 * n_kb + 1]
    heads = range(q_ref.shape[2] // HEAD_DIM)
    lanes = [slice(hh * HEAD_DIM, (hh + 1) * HEAD_DIM) for hh in heads]

    @pl.when(qi == 0)
    def _():
        r = lax.broadcasted_iota(jnp.int32, (t, n_kb * t), 0)
        c = lax.broadcasted_iota(jnp.int32, (t, n_kb * t), 1)
        chunk_diff = lax.div((n_kb - 1) * t + r, CHUNK) - lax.div(c, CHUNK)
        visible = jnp.logical_and(chunk_diff >= 0, chunk_diff <= LEFT_CHUNKS)
        for hh in heads:
            table = jnp.broadcast_to(rel_ref[hh], (t, (n_kb + 1) * t))
            rolled = pltpu.roll(table, 0, 1, stride=1, stride_axis=0)
            bias_ref[hh] = jnp.where(visible, rolled[:, t:], NEG_INF)

    exists = [qi - (n_kb - 1) + dd >= 0 for dd in range(n_kb)]
    scores = [[jnp.where(exists[dd], _dot_nt(q_ref[0, :, lanes[hh]], k_refs[dd][0, :, lanes[hh]])
                         + bias_ref[hh, :, dd * t:(dd + 1) * t], NEG_INF)
               for dd in range(n_kb)] for hh in heads]
    maxes = [functools.reduce(jnp.maximum, [s.max(axis=-1, keepdims=True) for s in scores[hh]]) for hh in heads]
    probs = [[jnp.exp(s - maxes[hh]) for s in scores[hh]] for hh in heads]
    accs = [sum(_dot(p.astype(BF16), v_refs[dd][0, :, lanes[hh]]) for dd, p in enumerate(probs[hh]))
            for hh in heads]
    for hh in heads:
        denom = sum(p.sum(axis=-1, keepdims=True) for p in probs[hh])
        o_ref[0, :, lanes[hh]] = (accs[hh] / denom * sg_ref[0, :, lanes[hh]].astype(F32)).astype(o_ref.dtype)


def _band_rel_table(rel_bias, t, n_kb):
    back = (n_kb - 1) * t
    offset = jnp.arange((n_kb + 1) * t) - t
    idx = jnp.clip(back - offset, -REL_CLIP, REL_CLIP) + REL_CLIP
    return rel_bias.astype(F32)[:, None, idx]


def _band_prompt(q, k, v, sg, rel_bias, batch, seq):
    d = k.shape[1]
    t = ATTN_T
    tiles = seq // t
    hw = BAND_HEADS * HEAD_DIM
    n_kb = (LEFT_CHUNKS * CHUNK) // t + 1
    rel = _band_rel_table(rel_bias, t, n_kb)
    blk_in = pl.BlockSpec((1, t, hw), lambda b, h, i: (0, b * tiles + i, h))
    kv = [pl.BlockSpec((1, t, hw), functools.partial(
        lambda b, h, i, back: (0, b * tiles + jnp.maximum(i - back, 0), h), back=n_kb - 1 - dd))
        for dd in range(n_kb)]
    flat = lambda a: a.reshape(1, a.shape[0], d)
    out = pl.pallas_call(
        _band_prompt_kernel,
        out_shape=jax.ShapeDtypeStruct((batch, seq, d), BF16),
        grid=(batch, d // hw, tiles),
        in_specs=[blk_in, blk_in, pl.BlockSpec((BAND_HEADS,) + rel.shape[1:], lambda b, h, i: (h, 0, 0))]
        + kv + kv,
        out_specs=pl.BlockSpec((1, t, hw), lambda b, h, i: (b, i, h)),
        scratch_shapes=[pltpu.VMEM((BAND_HEADS, t, n_kb * t), F32)],
        compiler_params=_params(3),
        name="band_prompt",
    )(flat(q), flat(sg), rel, *([flat(k)] * n_kb), *([flat(v)] * n_kb))
    return out.reshape(batch * seq, d)


def _band_sample_kernel(q_ref, k_ref, v_ref, sg_ref, rel_ref, ck_ref, cv_ref, o_ref, bias_ref, *, past_total):
    past, nh = ck_ref.shape[1:3]
    t_new = q_ref.shape[1]
    n_keys = past + t_new
    heads = range(nh)
    lanes = [slice(hh * HEAD_DIM, (hh + 1) * HEAD_DIM) for hh in heads]

    @pl.when(pl.program_id(1) == 0)
    def _():
        r = lax.broadcasted_iota(jnp.int32, (t_new, n_keys), 0)
        c = lax.broadcasted_iota(jnp.int32, (t_new, n_keys), 1)
        q_chunk = lax.div(past_total + r, CHUNK)
        k_chunk = lax.div(past_total - past + c, CHUNK)
        visible = jnp.logical_and(k_chunk <= q_chunk, k_chunk >= q_chunk - LEFT_CHUNKS)
        for hh in heads:
            table = jnp.broadcast_to(rel_ref[hh], (t_new, rel_ref.shape[2]))
            rolled = pltpu.roll(table, 0, 1, stride=1, stride_axis=0)
            bias_ref[hh] = jnp.where(visible, rolled[:, LANES:LANES + n_keys], NEG_INF)

    def keys(hh):
        return [(_head_rows(ck_ref, hh, 0, past).astype(BF16), slice(0, past)),
                (k_ref[0, :, lanes[hh]].astype(BF16), slice(past, None))]

    def values(hh):
        return [_head_rows(cv_ref, hh, 0, past).astype(BF16), v_ref[0, :, lanes[hh]].astype(BF16)]

    scores = [[_dot_nt(q_ref[0, :, lanes[hh]], kblk) + bias_ref[hh, :, cols] for kblk, cols in keys(hh)]
              for hh in heads]
    maxes = [functools.reduce(jnp.maximum, [s.max(axis=-1, keepdims=True) for s in scores[hh]]) for hh in heads]
    probs = [[jnp.exp(s - maxes[hh]) for s in scores[hh]] for hh in heads]
    accs = [sum(_dot(p.astype(BF16), vblk) for p, vblk in zip(probs[hh], values(hh))) for hh in heads]
    for hh in heads:
        denom = sum(p.sum(axis=-1, keepdims=True) for p in probs[hh])
        o_ref[0, :, lanes[hh]] = (accs[hh] / denom * sg_ref[0, :, lanes[hh]].astype(F32)).astype(o_ref.dtype)


def _band_sample(q, cache_k, cache_v, k, v, sg, rel_bias, row0, past_total):
    b, past_b, n_heads, _ = cache_k.shape
    assert past_total >= past_b
    d = n_heads * HEAD_DIM
    t_new = (q.shape[0] - row0) // b
    hw = SAMPLE_HEADS * HEAD_DIM
    blk0 = row0 // t_new
    width = pl.next_power_of_2(LANES + past_b + t_new)
    offset = jnp.arange(width) - LANES
    rel = rel_bias.astype(F32)[:, None, jnp.clip(past_b - offset, -REL_CLIP, REL_CLIP) + REL_CLIP]
    r3 = lambda a: a.reshape(a.shape[0] // t_new, t_new, d)
    merged = pl.BlockSpec((1, t_new, hw), lambda g, bi: (blk0 + bi, 0, g))
    new = pl.BlockSpec((1, t_new, hw), lambda g, bi: (bi, 0, g))
    cache = pl.BlockSpec((1, past_b, SAMPLE_HEADS, HEAD_DIM), lambda g, bi: (bi, 0, g, 0))
    out = pl.pallas_call(
        functools.partial(_band_sample_kernel, past_total=past_total),
        out_shape=jax.ShapeDtypeStruct((b, t_new, d), BF16),
        grid=(d // hw, b),
        in_specs=[merged, new, new, merged,
                  pl.BlockSpec((SAMPLE_HEADS, 1, width), lambda g, bi: (g, 0, 0)), cache, cache],
        out_specs=new,
        scratch_shapes=[pltpu.VMEM((SAMPLE_HEADS, t_new, past_b + t_new), F32)],
        compiler_params=_params(2),
        name="band_sample",
    )(r3(q), r3(k), r3(v), r3(sg), rel, cache_k, cache_v)
    return out.reshape(b * t_new, d)


def kernel(x_prompt, x_sample, c_prompt, c_sample, cache_a_k, cache_a_v, cache_b_k, cache_b_v, w_mod_a, b_mod_a, g_norm_a, w_in_a, w_out_a, g_kv, w_kv, w_mod_b, b_mod_b, g_norm_b, w_in_b, rel_bias_b, w_out_b, g_final):
    bp, seq, d = x_prompt.shape
    bs, t_new, _ = x_sample.shape
    n_heads = d // HEAD_DIM
    past = cache_a_k.shape[2]
    past_b = cache_b_k.shape[1]
    mp, ms = bp * seq, bs * t_new
    assert w_mod_a.shape[0] == 1 and w_mod_b.shape[0] == 1, "one layer of each mixer"
    assert seq % MM_TM == 0 and mp % MM_TM == 0 and ms % MM_TM == 0 and MM_TM % t_new == 0
    assert (mp + ms) % MM_TM_NARROW_OUT == 0
    assert seq % RESID_TM == 0 and ms % RESID_TM == 0 and RESID_TM % t_new == 0 and d % MM_TN == 0
    assert bs % (RESID_TM // t_new) == 0 and bs % (NORM_TM // t_new) == 0
    assert seq % STICK_TQ == 0 and STICK_TQ % ATTN_T == 0
    assert seq % ATTN_T == 0 and past % ATTN_T == 0 and (LEFT_CHUNKS * CHUNK) % ATTN_T == 0

    xp = x_prompt.reshape(mp, d)
    xs = x_sample.reshape(ms, d)

    n_c = bs + bp
    pad = -n_c % BF16_SUBLANES
    c_all = jnp.concatenate([c_sample, c_prompt, jnp.zeros((pad, d), F32)], axis=0)
    mod_a, mod_b = _adaln_tables(c_all, w_mod_a[0], b_mod_a[0], w_mod_b[0], b_mod_b[0])
    shift_a, scale_a, gate_a = _split_mod(mod_a, d)
    shift_b, scale_b, gate_b = _split_mod(mod_b, d)

    h_a = _prenorm(xp, xs, g_norm_a[0], shift_a, scale_a, seq, t_new)
    w_in = w_in_a[0]
    q_scale = 1.0 / math.sqrt(HEAD_DIM)
    q_a, = _mm(h_a, w_in, 0, d, (BF16,), scale=q_scale)
    k_p, k_p16 = _mm(h_a, w_in, d, d, (F32, BF16), row0=0, rows=mp)
    k_s, = _mm(h_a, w_in, d, d, (F32,), row0=mp, rows=ms)
    v_p, v_p16 = _mm(h_a, w_in, 2 * d, d, (F32, BF16), row0=0, rows=mp)
    v_s, = _mm(h_a, w_in, 2 * d, d, (F32,), row0=mp, rows=ms)
    sg_a, = _mm(h_a, w_in, 3 * d, d, (BF16,), silu=True)

    u_p = _stick_prompt(q_a, k_p16, v_p16, sg_a, bp, seq)
    u_s = _stick_sample(q_a, cache_a_k[0], cache_a_v[0], k_s, v_s, sg_a, mp)
    x1_p = _mm_resid(u_p, w_out_a[0], xp, gate_a, seq, bs)
    x1_s = _mm_resid(u_s, w_out_a[0], xs, gate_a, t_new, 0)

    h_kv, h_b = _dualnorm(x1_p, x1_s, g_kv, g_norm_b[0], shift_b, scale_b, seq, t_new)
    kb_p, kb_p16 = _mm(h_kv, w_kv, 0, d, (F32, BF16), row0=0, rows=mp)
    kb_s, kb_s16 = _mm(h_kv, w_kv, 0, d, (F32, BF16), row0=mp, rows=ms)
    vb_p, vb_p16 = _mm(h_kv, w_kv, d, d, (F32, BF16), row0=0, rows=mp)
    vb_s, vb_s16 = _mm(h_kv, w_kv, d, d, (F32, BF16), row0=mp, rows=ms)
    q_b, = _mm(h_b, w_in_b[0], 0, d, (BF16,), scale=q_scale)
    sg_b, = _mm(h_b, w_in_b[0], d, d, (BF16,), silu=True)

    ub_p = _band_prompt(q_b, kb_p16, vb_p16, sg_b, rel_bias_b[0], bp, seq)
    ub_s = _band_sample(q_b, cache_b_k, cache_b_v, kb_s16, vb_s16, sg_b, rel_bias_b[0], mp, past)
    x2_p = _mm_resid(ub_p, w_out_b[0], x1_p, gate_b, seq, bs)
    x2_s = _mm_resid(ub_s, w_out_b[0], x1_s, gate_b, t_new, 0)

    y_p = _finalnorm(x2_p, g_final).reshape(bp, seq, d)
    y_s = _finalnorm(x2_s, g_final).reshape(bs, t_new, d)

    keep_b = min(LEFT_CHUNKS * CHUNK, seq)
    heads = lambda a, b, t: a.reshape(b, t, n_heads, HEAD_DIM)
    tail = lambda a: jnp.stack([a[(b + 1) * seq - keep_b:(b + 1) * seq] for b in range(bp)])
    return (y_p, y_s,
            heads(k_p, bp, seq)[None], heads(v_p, bp, seq)[None],
            heads(k_s, bs, t_new)[None], heads(v_s, bs, t_new)[None],
            heads(tail(kb_p), bp, keep_b), heads(tail(vb_p), bp, keep_b),
            heads(kb_s, bs, t_new), heads(vb_s, bs, t_new))
```

```python
import functools
import math

import jax
import jax.numpy as jnp
from jax import lax
from jax.experimental import pallas as pl
from jax.experimental.pallas import tpu as pltpu

F32 = jnp.float32
BF16 = jnp.bfloat16

HEAD_DIM = 128
CHUNK = 64
LEFT_CHUNKS = 8
REL_CLIP = 128
EPS = 1e-6
NEG_INF = -1e30

V7X_VMEM_LIMIT_BYTES = 54 * 1024 * 1024
LANES = 128
BF16_SUBLANES = 16

MM_TM = 512
MM_TM_NARROW_OUT = 1024
RESID_TM = 512
MM_TN = 1024
NORM_TM = 256
MOD_TN = 512
ATTN_T = 256
STICK_TQ = 256
STICK_HEADS = 8
BAND_HEADS = 4

F32_EXP_UNDERFLOW_LOG = -104.0
SAMPLE_HEADS = 8
CACHE_SLOTS = 4


def _params(n_grid):
    return pltpu.CompilerParams(
        dimension_semantics=("arbitrary",) * n_grid,
        vmem_limit_bytes=V7X_VMEM_LIMIT_BYTES)


def _silu(x):
    return x / (1.0 + jnp.exp(-x))


def _dot(a, b):
    return jnp.dot(a, b, preferred_element_type=F32)


def _dot_nt(a, b):
    return lax.dot_general(a, b, (((1,), (1,)), ((), ())), preferred_element_type=F32)


def _mod_kernel(c_ref, wa_ref, ba_ref, wb_ref, bb_ref, oa_ref, ob_ref):
    a = _silu(c_ref[...]).astype(BF16)
    oa_ref[...] = _dot(a, wa_ref[...].astype(BF16)) + ba_ref[...]
    ob_ref[...] = _dot(a, wb_ref[...].astype(BF16)) + bb_ref[...]


def _adaln_tables(c_all, w_a, b_a, w_b, b_b):
    rows, d = c_all.shape
    n = w_a.shape[1]
    w_spec = pl.BlockSpec((d, MOD_TN), lambda j: (0, j))
    b_spec = pl.BlockSpec((1, MOD_TN), lambda j: (0, j))
    o_spec = pl.BlockSpec((rows, MOD_TN), lambda j: (0, j))
    return pl.pallas_call(
        _mod_kernel,
        out_shape=(jax.ShapeDtypeStruct((rows, n), F32),) * 2,
        grid=(n // MOD_TN,),
        in_specs=[pl.BlockSpec((rows, d), lambda j: (0, 0)), w_spec, b_spec, w_spec, b_spec],
        out_specs=(o_spec, o_spec),
        compiler_params=_params(1),
        name="adaln_mod",
    )(c_all, w_a, b_a.reshape(1, n), w_b, b_b.reshape(1, n))


def _split_mod(mod, d):
    rows = mod.shape[0]
    return tuple(mod[:, k * d:(k + 1) * d].reshape(rows, 1, d) for k in range(3))


def _rms_scale(x):
    return lax.rsqrt(jnp.mean(x * x, axis=-1, keepdims=True) + EPS)


def _modulate(y, shift_ref, scale_ref):
    nb = shift_ref.shape[0]
    tm, d = y.shape
    y3 = y.reshape(nb, tm // nb, d)
    return (y3 * (1.0 + scale_ref[...]) + shift_ref[...]).reshape(tm, d)


def _prenorm_kernel(xp_ref, xs_ref, g_ref, shp_ref, scp_ref, shs_ref, scs_ref, o_ref, *, n_prompt_tiles):
    i = pl.program_id(0)

    def run(x_ref, shift_ref, scale_ref):
        x = x_ref[...]
        y = x * _rms_scale(x) * g_ref[...]
        o_ref[...] = _modulate(y, shift_ref, scale_ref).astype(o_ref.dtype)

    @pl.when(i < n_prompt_tiles)
    def _():
        run(xp_ref, shp_ref, scp_ref)

    @pl.when(i >= n_prompt_tiles)
    def _():
        run(xs_ref, shs_ref, scs_ref)


def _group_specs(tm, d, n_prompt_tiles, prompt_rpb, sample_rpb, n_sample_batches):
    tiles_per_prompt_batch = prompt_rpb // tm
    nb_s = tm // sample_rpb
    rows_p = pl.BlockSpec((tm, d), lambda i: (jnp.minimum(i, n_prompt_tiles - 1), 0))
    rows_s = pl.BlockSpec((tm, d), lambda i: (jnp.maximum(i - n_prompt_tiles, 0), 0))
    tab_p = pl.BlockSpec(
        (1, 1, d),
        lambda i: (n_sample_batches + jnp.minimum(i, n_prompt_tiles - 1) // tiles_per_prompt_batch, 0, 0))
    tab_s = pl.BlockSpec((nb_s, 1, d), lambda i: (jnp.maximum(i - n_prompt_tiles, 0), 0, 0))
    return rows_p, rows_s, tab_p, tab_s


def _prenorm(x_p, x_s, g, shift, scale, prompt_rpb, sample_rpb):
    mp, d = x_p.shape
    ms = x_s.shape[0]
    tm = NORM_TM
    npt = mp // tm
    rows_p, rows_s, tab_p, tab_s = _group_specs(tm, d, npt, prompt_rpb, sample_rpb, ms // sample_rpb)
    return pl.pallas_call(
        functools.partial(_prenorm_kernel, n_prompt_tiles=npt),
        out_shape=jax.ShapeDtypeStruct((mp + ms, d), BF16),
        grid=((mp + ms) // tm,),
        in_specs=[rows_p, rows_s, pl.BlockSpec((1, d), lambda i: (0, 0)), tab_p, tab_p, tab_s, tab_s],
        out_specs=pl.BlockSpec((tm, d), lambda i: (i, 0)),
        compiler_params=_params(1),
        name="prenorm",
    )(x_p, x_s, g.reshape(1, d), shift, scale, shift, scale)


def _dualnorm_kernel(xp_ref, xs_ref, gkv_ref, gb_ref, shp_ref, scp_ref, shs_ref, scs_ref,
                     okv_ref, ob_ref, *, n_prompt_tiles):
    i = pl.program_id(0)

    def run(x_ref, shift_ref, scale_ref):
        x = x_ref[...]
        y = x * _rms_scale(x)
        okv_ref[...] = (y * gkv_ref[...]).astype(okv_ref.dtype)
        ob_ref[...] = _modulate(y * gb_ref[...], shift_ref, scale_ref).astype(ob_ref.dtype)

    @pl.when(i < n_prompt_tiles)
    def _():
        run(xp_ref, shp_ref, scp_ref)

    @pl.when(i >= n_prompt_tiles)
    def _():
        run(xs_ref, shs_ref, scs_ref)


def _dualnorm(x_p, x_s, g_kv, g_b, shift, scale, prompt_rpb, sample_rpb):
    mp, d = x_p.shape
    ms = x_s.shape[0]
    tm = NORM_TM
    npt = mp // tm
    rows_p, rows_s, tab_p, tab_s = _group_specs(tm, d, npt, prompt_rpb, sample_rpb, ms // sample_rpb)
    g_spec = pl.BlockSpec((1, d), lambda i: (0, 0))
    o_spec = pl.BlockSpec((tm, d), lambda i: (i, 0))
    return pl.pallas_call(
        functools.partial(_dualnorm_kernel, n_prompt_tiles=npt),
        out_shape=(jax.ShapeDtypeStruct((mp + ms, d), BF16),) * 2,
        grid=((mp + ms) // tm,),
        in_specs=[rows_p, rows_s, g_spec, g_spec, tab_p, tab_p, tab_s, tab_s],
        out_specs=(o_spec, o_spec),
        compiler_params=_params(1),
        name="dualnorm",
    )(x_p, x_s, g_kv.reshape(1, d), g_b.reshape(1, d), shift, scale, shift, scale)


def _finalnorm_kernel(x_ref, g_ref, o_ref):
    x = x_ref[...]
    o_ref[...] = x * _rms_scale(x) * g_ref[...]


def _finalnorm(x, g):
    m, d = x.shape
    tm = NORM_TM
    return pl.pallas_call(
        _finalnorm_kernel,
        out_shape=jax.ShapeDtypeStruct((m, d), F32),
        grid=(m // tm,),
        in_specs=[pl.BlockSpec((tm, d), lambda i: (i, 0)), pl.BlockSpec((1, d), lambda i: (0, 0))],
        out_specs=pl.BlockSpec((tm, d), lambda i: (i, 0)),
        compiler_params=_params(1),
        name="finalnorm",
    )(x, g.reshape(1, d))


def _stage_weight_tile(w_hbm, stage_ref, wbf_ref, sem, col0):
    j, nj = pl.program_id(0), pl.num_programs(0)
    tn = wbf_ref.shape[1]

    def tile_copy(jj):
        return pltpu.make_async_copy(w_hbm.at[:, pl.ds(col0 + jj * tn, tn)], stage_ref, sem)

    @pl.when(pl.program_id(1) == 0)
    def _():
        @pl.when(j == 0)
        def _():
            tile_copy(0).start()

        tile_copy(j).wait()
        wbf_ref[...] = stage_ref[...].astype(BF16)

        @pl.when(j + 1 < nj)
        def _():
            tile_copy(j + 1).start()


def _weight_scratch(k, tn):
    return [pltpu.VMEM((k, tn), F32), pltpu.VMEM((k, tn), BF16), pltpu.SemaphoreType.DMA(())]


def _mm_kernel(h_ref, w_hbm, *refs, silu, scale, col0):
    *o_refs, stage_ref, wbf_ref, sem = refs
    _stage_weight_tile(w_hbm, stage_ref, wbf_ref, sem, col0)
    acc = _dot(h_ref[...], wbf_ref[...])
    if silu:
        acc = _silu(acc)
    if scale is not None:
        acc = acc * scale
    for o_ref in o_refs:
        o_ref[...] = acc.astype(o_ref.dtype)


def _mm(h, w, col0, n_out, out_dtypes, *, silu=False, scale=None, row0=0, rows=None):
    k = h.shape[1]
    rows = h.shape[0] if rows is None else rows
    tm = MM_TM if F32 in out_dtypes else MM_TM_NARROW_OUT
    tn = MM_TN
    i0 = row0 // tm
    o_spec = pl.BlockSpec((tm, tn), lambda j, i: (i, j))
    return pl.pallas_call(
        functools.partial(_mm_kernel, silu=silu, scale=scale, col0=col0),
        out_shape=tuple(jax.ShapeDtypeStruct((rows, n_out), dt) for dt in out_dtypes),
        grid=(n_out // tn, rows // tm),
        in_specs=[pl.BlockSpec((tm, k), lambda j, i: (i0 + i, 0)),
                  pl.BlockSpec(memory_space=pl.ANY)],
        out_specs=tuple(o_spec for _ in out_dtypes),
        scratch_shapes=_weight_scratch(k, tn),
        compiler_params=_params(2),
        name="proj",
    )(h, w)


def _mm_resid_kernel(u_ref, w_hbm, x_ref, gate_ref, o_ref, stage_ref, wbf_ref, sem):
    _stage_weight_tile(w_hbm, stage_ref, wbf_ref, sem, 0)
    acc = _dot(u_ref[...], wbf_ref[...])
    nb = gate_ref.shape[0]
    tm, tn = acc.shape
    y = acc.reshape(nb, tm // nb, tn) * gate_ref[...]
    o_ref[...] = x_ref[...] + y.reshape(tm, tn)


def _mm_resid(u, w, x, gate, rows_per_batch, table_row0):
    m, k = u.shape
    n = w.shape[1]
    tm, tn = RESID_TM, MM_TN
    if rows_per_batch >= tm:
        nb = 1
        tiles_per_batch = rows_per_batch // tm
        gate_map = lambda j, i: (table_row0 + i // tiles_per_batch, 0, j)
    else:
        nb = tm // rows_per_batch
        blk0 = table_row0 // nb
        gate_map = lambda j, i: (blk0 + i, 0, j)
    return pl.pallas_call(
        _mm_resid_kernel,
        out_shape=jax.ShapeDtypeStruct((m, n), F32),
        grid=(n // tn, m // tm),
        in_specs=[pl.BlockSpec((tm, k), lambda j, i: (i, 0)),
                  pl.BlockSpec(memory_space=pl.ANY),
                  pl.BlockSpec((tm, tn), lambda j, i: (i, j)),
                  pl.BlockSpec((nb, 1, tn), gate_map)],
        out_specs=pl.BlockSpec((tm, tn), lambda j, i: (i, j)),
        scratch_shapes=_weight_scratch(k, tn),
        compiler_params=_params(2),
        name="out_proj",
    )(u, w, x, gate)


def _softplus(z):
    return jnp.maximum(z, 0.0) + jnp.log(1.0 + jnp.exp(-jnp.abs(z)))


def _suffix_sum(x, tri):
    hi = x.astype(BF16)
    lo = (x - hi.astype(F32)).astype(BF16)
    return _dot(hi, tri) + _dot(lo, tri)


def _stick_blocks(qs, kblks, vblks, tri, carries, accs, causal):
    zs = [_dot_nt(q, kblk) for q, kblk in zip(qs, kblks)]
    sps = [_softplus(z) if causal is None else jnp.where(causal, _softplus(z), 0.0) for z in zs]
    incls = [_suffix_sum(sp, tri) for sp in sps]
    ws = [jnp.exp(z - incl + carry) for z, incl, carry in zip(zs, incls, carries)]
    if causal is not None:
        ws = [jnp.where(causal, w, 0.0) for w in ws]
    accs = [acc + _dot(w.astype(BF16), vblk) for acc, w, vblk in zip(accs, ws, vblks)]
    carries = [carry - incl[:, :1] for carry, incl in zip(carries, incls)]
    return carries, accs


def _strict_lower(tq, tk):
    return lax.broadcasted_iota(jnp.int32, (tq, tk), 1) < lax.broadcasted_iota(jnp.int32, (tq, tk), 0)


def _all_weights_vanish(carry):
    return jnp.max(carry) <= F32_EXP_UNDERFLOW_LOG


def _stick_prompt_kernel(q_ref, k_ref, v_ref, sg_ref, tri_ref, o_ref, *, tk):
    qi = pl.program_id(2)
    tq = q_ref.shape[1]
    n_sub = tq // tk
    heads = range(q_ref.shape[2] // HEAD_DIM)
    lanes = [slice(hh * HEAD_DIM, (hh + 1) * HEAD_DIM) for hh in heads]
    qs = [q_ref[0, :, lanes[hh]] for hh in heads]
    tri = tri_ref[...]

    def kv(k0):
        k0 = pl.multiple_of(k0, tk)
        return ([k_ref[0, pl.ds(k0, tk), lanes[hh]] for hh in heads],
                [v_ref[0, pl.ds(k0, tk), lanes[hh]] for hh in heads])

    carries = [jnp.zeros((tq, 1), F32) for _ in heads]
    accs = [jnp.zeros((tq, HEAD_DIM), F32) for _ in heads]
    for sb in reversed(range(n_sub)):
        r0 = sb * tk
        kblks, vblks = kv(qi * tq + r0)
        c_sub, a_sub = _stick_blocks([q[r0:] for q in qs], kblks, vblks, tri, [c[r0:] for c in carries],
                                     [a[r0:] for a in accs], _strict_lower(tq - r0, tk))
        if r0:
            c_sub = [jnp.concatenate([c[:r0], cs], axis=0) for c, cs in zip(carries, c_sub)]
            a_sub = [jnp.concatenate([a[:r0], as_], axis=0) for a, as_ in zip(accs, a_sub)]
        carries, accs = c_sub, a_sub

    def cond(state):
        it, carries, _ = state
        return jnp.logical_and(
            it < qi * n_sub, jnp.logical_not(_all_weights_vanish(functools.reduce(jnp.maximum, carries))))

    def body(state):
        it, carries, accs = state
        kblks, vblks = kv((qi * n_sub - 1 - it) * tk)
        carries, accs = _stick_blocks(qs, kblks, vblks, tri, carries, accs, None)
        return it + 1, tuple(carries), tuple(accs)

    _, carries, accs = lax.while_loop(cond, body, (jnp.int32(0), tuple(carries), tuple(accs)))
    for hh in heads:
        o_ref[0, :, lanes[hh]] = (accs[hh] * sg_ref[0, :, lanes[hh]].astype(F32)).astype(o_ref.dtype)


def _tri_incl(t):
    return (jnp.arange(t)[:, None] >= jnp.arange(t)[None, :]).astype(BF16)


def _stick_prompt(q, k, v, sg, batch, seq):
    d = k.shape[1]
    tq, tk = STICK_TQ, ATTN_T
    tiles = seq // tq
    hw = STICK_HEADS * HEAD_DIM
    flat = lambda a: a.reshape(1, a.shape[0], d)
    blk_in = pl.BlockSpec((1, tq, hw), lambda b, h, i: (0, b * tiles + i, h))
    kv = pl.BlockSpec((1, seq, hw), lambda b, h, i: (0, b, h))
    out = pl.pallas_call(
        functools.partial(_stick_prompt_kernel, tk=tk),
        out_shape=jax.ShapeDtypeStruct((batch, seq, d), BF16),
        grid=(batch, d // hw, tiles),
        in_specs=[blk_in, kv, kv, blk_in, pl.BlockSpec((tk, tk), lambda b, h, i: (0, 0))],
        out_specs=pl.BlockSpec((1, tq, hw), lambda b, h, i: (b, i, h)),
        compiler_params=_params(3),
        name="stick_prompt",
    )(flat(q), flat(k), flat(v), flat(sg), _tri_incl(tk))
    return out.reshape(batch * seq, d)


def _head_rows(cache_ref, hh, t0, n):
    nh, hd = cache_ref.shape[-2:]
    flat = cache_ref.reshape(math.prod(cache_ref.shape[:-1]), hd)
    return flat[pl.ds(t0 * nh + hh, n, stride=nh), :]


def _stick_sample_kernel(q_ref, ck_hbm, cv_hbm, k_ref, v_ref, sg_ref, tri_ref, o_ref, kbuf, vbuf, sem):
    bi, g = pl.program_id(0), pl.program_id(1)
    ng = pl.num_programs(1)
    step = bi * ng + g
    parity = step % 2
    t_new = q_ref.shape[1]
    past = ck_hbm.shape[1]
    _, kblock, nh, _ = kbuf.shape
    n_blk = past // kblock
    heads = range(nh)
    lanes = [slice(hh * HEAD_DIM, (hh + 1) * HEAD_DIM) for hh in heads]
    tri = tri_ref[...]
    qs = [q_ref[0, :, lanes[hh]] for hh in heads]

    def block_copies(batch, group, blk, slot):
        rows = pl.ds(past - (blk + 1) * kblock, kblock)
        cols = pl.ds(group * nh, nh)
        return (pltpu.make_async_copy(ck_hbm.at[batch, rows, cols, :], kbuf.at[slot], sem.at[0, slot]),
                pltpu.make_async_copy(cv_hbm.at[batch, rows, cols, :], vbuf.at[slot], sem.at[1, slot]))

    def newest_copies(of_step):
        return block_copies(of_step // ng, of_step % ng, 0, of_step % 2)

    def older_copies(blk):
        return block_copies(bi, g, blk, 2 + blk % 2)

    def attend(slot, carries, accs):
        t0 = slot * kblock
        return _stick_blocks(
            qs, [_head_rows(kbuf, hh, t0, kblock).astype(BF16) for hh in heads],
            [_head_rows(vbuf, hh, t0, kblock).astype(BF16) for hh in heads], tri, carries, accs, None)

    def alive(carries):
        return jnp.logical_not(_all_weights_vanish(functools.reduce(jnp.maximum, carries)))

    @pl.when(step == 0)
    def _():
        for c in newest_copies(step):
            c.start()

    @pl.when(step + 1 < pl.num_programs(0) * ng)
    def _():
        for c in newest_copies(step + 1):
            c.start()

    carries, accs = _stick_blocks(
        qs, [k_ref[0, :, lanes[hh]].astype(BF16) for hh in heads],
        [v_ref[0, :, lanes[hh]].astype(BF16) for hh in heads], tri_ref[:t_new, :t_new],
        [jnp.zeros((t_new, 1), F32) for _ in heads], [jnp.zeros((t_new, HEAD_DIM), F32) for _ in heads],
        _strict_lower(t_new, t_new))

    for c in newest_copies(step):
        c.wait()
    carries, accs = attend(parity, carries, accs)

    more = jnp.logical_and(n_blk > 1, alive(carries))

    @pl.when(more)
    def _():
        for c in older_copies(1):
            c.start()

    def cond(state):
        it, carries, _ = state
        return jnp.logical_and(it < n_blk, alive(carries))

    def body(state):
        it, carries, accs = state
        for c in older_copies(it):
            c.wait()

        @pl.when(it + 1 < n_blk)
        def _():
            for c in older_copies(it + 1):
                c.start()

        carries, accs = attend(2 + it % 2, carries, accs)
        return it + 1, tuple(carries), tuple(accs)

    done, carries, accs = lax.while_loop(cond, body, (jnp.int32(1), tuple(carries), tuple(accs)))

    @pl.when(jnp.logical_and(more, done < n_blk))
    def _():
        for c in older_copies(done):
            c.wait()

    for hh in heads:
        o_ref[0, :, lanes[hh]] = (accs[hh] * sg_ref[0, :, lanes[hh]].astype(F32)).astype(o_ref.dtype)


def _stick_sample(q, cache_k, cache_v, k, v, sg, row0):
    b, past, n_heads, _ = cache_k.shape
    d = n_heads * HEAD_DIM
    t_new = k.shape[0] // b
    hw = SAMPLE_HEADS * HEAD_DIM
    t = ATTN_T
    blk0 = row0 // t_new
    q3 = q.reshape(q.shape[0] // t_new, t_new, d)
    sg3 = sg.reshape(q3.shape)
    merged = pl.BlockSpec((1, t_new, hw), lambda bi, g: (blk0 + bi, 0, g))
    new = pl.BlockSpec((1, t_new, hw), lambda bi, g: (bi, 0, g))
    cache = pl.BlockSpec(memory_space=pl.ANY)
    block_buf = pltpu.VMEM((CACHE_SLOTS, t, SAMPLE_HEADS, HEAD_DIM), F32)
    out = pl.pallas_call(
        _stick_sample_kernel,
        out_shape=jax.ShapeDtypeStruct((b, t_new, d), BF16),
        grid=(b, d // hw),
        in_specs=[merged, cache, cache, new, new, merged, pl.BlockSpec((t, t), lambda bi, g: (0, 0))],
        out_specs=new,
        scratch_shapes=[block_buf, block_buf, pltpu.SemaphoreType.DMA((2, CACHE_SLOTS))],
        compiler_params=_params(2),
        name="stick_sample",
    )(q3, cache_k, cache_v, k.reshape(b, t_new, d), v.reshape(b, t_new, d), sg3, _tri_incl(t))
    return out.reshape(b * t_new, d)


def _band_prompt_kernel(q_ref, sg_ref, rel_ref, *refs):
    qi = pl.program_id(2)
    t = q_ref.shape[1]
    n_kb = rel_ref.shape[2] // t - 1
    k_refs, v_refs, o_ref, bias_ref = refs[:n_kb], refs[n_kb:2 * n_kb], refs[2 * n_kb], refs[2 * n_kb + 1]
    heads = range(q_ref.shape[2] // HEAD_DIM)
    lanes = [slice(hh * HEAD_DIM, (hh + 1) * HEAD_DIM) for hh in heads]

    @pl.when(qi == 0)
    def _():
        r = lax.broadcasted_iota(jnp.int32, (t, n_kb * t), 0)
        c = lax.broadcasted_iota(jnp.int32, (t, n_kb * t), 1)
        chunk_diff = lax.div((n_kb - 1) * t + r, CHUNK) - lax.div(c, CHUNK)
        visible = jnp.logical_and(chunk_diff >= 0, chunk_diff <= LEFT_CHUNKS)
        for hh in heads:
            table = jnp.broadcast_to(rel_ref[hh], (t, (n_kb + 1) * t))
            rolled = pltpu.roll(table, 0, 1, stride=1, stride_axis=0)
            bias_ref[hh] = jnp.where(visible, rolled[:, t:], NEG_INF)

    def attend(window_complete):
        def score(hh, dd):
            s = _dot_nt(q_ref[0, :, lanes[hh]], k_refs[dd][0, :, lanes[hh]]) + bias_ref[hh, :, dd * t:(dd + 1) * t]
            return s if window_complete else jnp.where(qi - (n_kb - 1) + dd >= 0, s, NEG_INF)

        scores = [[score(hh, dd) for dd in range(n_kb)] for hh in heads]
        maxes = [functools.reduce(jnp.maximum, [s.max(axis=-1, keepdims=True) for s in scores[hh]]) for hh in heads]
        probs = [[jnp.exp(s - maxes[hh]) for s in scores[hh]] for hh in heads]
        accs = [sum(_dot(p.astype(BF16), v_refs[dd][0, :, lanes[hh]]) for dd, p in enumerate(probs[hh]))
                for hh in heads]
        for hh in heads:
            denom = sum(p.sum(axis=-1, keepdims=True) for p in probs[hh])
            o_ref[0, :, lanes[hh]] = (accs[hh] / denom * sg_ref[0, :, lanes[hh]].astype(F32)).astype(o_ref.dtype)

    pl.when(qi >= n_kb - 1)(functools.partial(attend, True))
    pl.when(qi < n_kb - 1)(functools.partial(attend, False))


def _band_rel_table(rel_bias, t, n_kb):
    back = (n_kb - 1) * t
    offset = jnp.arange((n_kb + 1) * t) - t
    idx = jnp.clip(back - offset, -REL_CLIP, REL_CLIP) + REL_CLIP
    return rel_bias.astype(F32)[:, None, idx]


def _band_prompt(q, k, v, sg, rel_bias, batch, seq):
    d = k.shape[1]
    t = ATTN_T
    tiles = seq // t
    hw = BAND_HEADS * HEAD_DIM
    n_kb = (LEFT_CHUNKS * CHUNK) // t + 1
    rel = _band_rel_table(rel_bias, t, n_kb)
    blk_in = pl.BlockSpec((1, t, hw), lambda b, h, i: (0, b * tiles + i, h))
    kv = [pl.BlockSpec((1, t, hw), functools.partial(
        lambda b, h, i, back: (0, b * tiles + jnp.maximum(i - back, 0), h), back=n_kb - 1 - dd))
        for dd in range(n_kb)]
    flat = lambda a: a.reshape(1, a.shape[0], d)
    out = pl.pallas_call(
        _band_prompt_kernel,
        out_shape=jax.ShapeDtypeStruct((batch, seq, d), BF16),
        grid=(batch, d // hw, tiles),
        in_specs=[blk_in, blk_in, pl.BlockSpec((BAND_HEADS,) + rel.shape[1:], lambda b, h, i: (h, 0, 0))]
        + kv + kv,
        out_specs=pl.BlockSpec((1, t, hw), lambda b, h, i: (b, i, h)),
        scratch_shapes=[pltpu.VMEM((BAND_HEADS, t, n_kb * t), F32)],
        compiler_params=_params(3),
        name="band_prompt",
    )(flat(q), flat(sg), rel, *([flat(k)] * n_kb), *([flat(v)] * n_kb))
    return out.reshape(batch * seq, d)


def _band_sample_kernel(q_ref, k_ref, v_ref, sg_ref, rel_ref, ck_ref, cv_ref, o_ref, bias_ref, *, past_total):
    past, nh = ck_ref.shape[1:3]
    t_new = q_ref.shape[1]
    n_keys = past + t_new
    heads = range(nh)
    lanes = [slice(hh * HEAD_DIM, (hh + 1) * HEAD_DIM) for hh in heads]

    @pl.when(pl.program_id(1) == 0)
    def _():
        r = lax.broadcasted_iota(jnp.int32, (t_new, n_keys), 0)
        c = lax.broadcasted_iota(jnp.int32, (t_new, n_keys), 1)
        q_chunk = lax.div(past_total + r, CHUNK)
        k_chunk = lax.div(past_total - past + c, CHUNK)
        visible = jnp.logical_and(k_chunk <= q_chunk, k_chunk >= q_chunk - LEFT_CHUNKS)
        for hh in heads:
            table = jnp.broadcast_to(rel_ref[hh], (t_new, rel_ref.shape[2]))
            rolled = pltpu.roll(table, 0, 1, stride=1, stride_axis=0)
            bias_ref[hh] = jnp.where(visible, rolled[:, LANES:LANES + n_keys], NEG_INF)

    def keys(hh):
        return [(_head_rows(ck_ref, hh, 0, past).astype(BF16), slice(0, past)),
                (k_ref[0, :, lanes[hh]].astype(BF16), slice(past, None))]

    def values(hh):
        return [_head_rows(cv_ref, hh, 0, past).astype(BF16), v_ref[0, :, lanes[hh]].astype(BF16)]

    scores = [[_dot_nt(q_ref[0, :, lanes[hh]], kblk) + bias_ref[hh, :, cols] for kblk, cols in keys(hh)]
              for hh in heads]
    maxes = [functools.reduce(jnp.maximum, [s.max(axis=-1, keepdims=True) for s in scores[hh]]) for hh in heads]
    probs = [[jnp.exp(s - maxes[hh]) for s in scores[hh]] for hh in heads]
    accs = [sum(_dot(p.astype(BF16), vblk) for p, vblk in zip(probs[hh], values(hh))) for hh in heads]
    for hh in heads:
        denom = sum(p.sum(axis=-1, keepdims=True) for p in probs[hh])
        o_ref[0, :, lanes[hh]] = (accs[hh] / denom * sg_ref[0, :, lanes[hh]].astype(F32)).astype(o_ref.dtype)


def _band_sample(q, cache_k, cache_v, k, v, sg, rel_bias, row0, past_total):
    b, past_b, n_heads, _ = cache_k.shape
    assert past_total >= past_b
    d = n_heads * HEAD_DIM
    t_new = (q.shape[0] - row0) // b
    hw = SAMPLE_HEADS * HEAD_DIM
    blk0 = row0 // t_new
    width = pl.next_power_of_2(LANES + past_b + t_new)
    offset = jnp.arange(width) - LANES
    rel = rel_bias.astype(F32)[:, None, jnp.clip(past_b - offset, -REL_CLIP, REL_CLIP) + REL_CLIP]
    r3 = lambda a: a.reshape(a.shape[0] // t_new, t_new, d)
    merged = pl.BlockSpec((1, t_new, hw), lambda g, bi: (blk0 + bi, 0, g))
    new = pl.BlockSpec((1, t_new, hw), lambda g, bi: (bi, 0, g))
    cache = pl.BlockSpec((1, past_b, SAMPLE_HEADS, HEAD_DIM), lambda g, bi: (bi, 0, g, 0))
    out = pl.pallas_call(
        functools.partial(_band_sample_kernel, past_total=past_total),
        out_shape=jax.ShapeDtypeStruct((b, t_new, d), BF16),
        grid=(d // hw, b),
        in_specs=[merged, new, new, merged,
                  pl.BlockSpec((SAMPLE_HEADS, 1, width), lambda g, bi: (g, 0, 0)), cache, cache],
        out_specs=new,
        scratch_shapes=[pltpu.VMEM((SAMPLE_HEADS, t_new, past_b + t_new), F32)],
        compiler_params=_params(2),
        name="band_sample",
    )(r3(q), r3(k), r3(v), r3(sg), rel, cache_k, cache_v)
    return out.reshape(b * t_new, d)


def kernel(x_prompt, x_sample, c_prompt, c_sample, cache_a_k, cache_a_v, cache_b_k, cache_b_v, w_mod_a, b_mod_a, g_norm_a, w_in_a, w_out_a, g_kv, w_kv, w_mod_b, b_mod_b, g_norm_b, w_in_b, rel_bias_b, w_out_b, g_final):
    bp, seq, d = x_prompt.shape
    bs, t_new, _ = x_sample.shape
    n_heads = d // HEAD_DIM
    past = cache_a_k.shape[2]
    past_b = cache_b_k.shape[1]
    mp, ms = bp * seq, bs * t_new
    assert w_mod_a.shape[0] == 1 and w_mod_b.shape[0] == 1, "one layer of each mixer"
    assert seq % MM_TM == 0 and mp % MM_TM == 0 and ms % MM_TM == 0 and MM_TM % t_new == 0
    assert (mp + ms) % MM_TM_NARROW_OUT == 0
    assert seq % RESID_TM == 0 and ms % RESID_TM == 0 and RESID_TM % t_new == 0 and d % MM_TN == 0
    assert bs % (RESID_TM // t_new) == 0 and bs % (NORM_TM // t_new) == 0
    assert seq % STICK_TQ == 0 and STICK_TQ % ATTN_T == 0
    assert seq % ATTN_T == 0 and past % ATTN_T == 0 and (LEFT_CHUNKS * CHUNK) % ATTN_T == 0

    xp = x_prompt.reshape(mp, d)
    xs = x_sample.reshape(ms, d)

    n_c = bs + bp
    pad = -n_c % BF16_SUBLANES
    c_all = jnp.concatenate([c_sample, c_prompt, jnp.zeros((pad, d), F32)], axis=0)
    mod_a, mod_b = _adaln_tables(c_all, w_mod_a[0], b_mod_a[0], w_mod_b[0], b_mod_b[0])
    shift_a, scale_a, gate_a = _split_mod(mod_a, d)
    shift_b, scale_b, gate_b = _split_mod(mod_b, d)

    h_a = _prenorm(xp, xs, g_norm_a[0], shift_a, scale_a, seq, t_new)
    w_in = w_in_a[0]
    q_scale = 1.0 / math.sqrt(HEAD_DIM)
    q_a, = _mm(h_a, w_in, 0, d, (BF16,), scale=q_scale)
    k_p, k_p16 = _mm(h_a, w_in, d, d, (F32, BF16), row0=0, rows=mp)
    k_s, = _mm(h_a, w_in, d, d, (F32,), row0=mp, rows=ms)
    v_p, v_p16 = _mm(h_a, w_in, 2 * d, d, (F32, BF16), row0=0, rows=mp)
    v_s, = _mm(h_a, w_in, 2 * d, d, (F32,), row0=mp, rows=ms)
    sg_a, = _mm(h_a, w_in, 3 * d, d, (BF16,), silu=True)

    u_p = _stick_prompt(q_a, k_p16, v_p16, sg_a, bp, seq)
    u_s = _stick_sample(q_a, cache_a_k[0], cache_a_v[0], k_s, v_s, sg_a, mp)
    x1_p = _mm_resid(u_p, w_out_a[0], xp, gate_a, seq, bs)
    x1_s = _mm_resid(u_s, w_out_a[0], xs, gate_a, t_new, 0)

    h_kv, h_b = _dualnorm(x1_p, x1_s, g_kv, g_norm_b[0], shift_b, scale_b, seq, t_new)
    kb_p, kb_p16 = _mm(h_kv, w_kv, 0, d, (F32, BF16), row0=0, rows=mp)
    kb_s, kb_s16 = _mm(h_kv, w_kv, 0, d, (F32, BF16), row0=mp, rows=ms)
    vb_p, vb_p16 = _mm(h_kv, w_kv, d, d, (F32, BF16), row0=0, rows=mp)
    vb_s, vb_s16 = _mm(h_kv, w_kv, d, d, (F32, BF16), row0=mp, rows=ms)
    q_b, = _mm(h_b, w_in_b[0], 0, d, (BF16,), scale=q_scale)
    sg_b, = _mm(h_b, w_in_b[0], d, d, (BF16,), silu=True)

    ub_p = _band_prompt(q_b, kb_p16, vb_p16, sg_b, rel_bias_b[0], bp, seq)
    ub_s = _band_sample(q_b, cache_b_k, cache_b_v, kb_s16, vb_s16, sg_b, rel_bias_b[0], mp, past)
    x2_p = _mm_resid(ub_p, w_out_b[0], x1_p, gate_b, seq, bs)
    x2_s = _mm_resid(ub_s, w_out_b[0], x1_s, gate_b, t_new, 0)

    y_p = _finalnorm(x2_p, g_final).reshape(bp, seq, d)
    y_s = _finalnorm(x2_s, g_final).reshape(bs, t_new, d)

    keep_b = min(LEFT_CHUNKS * CHUNK, seq)
    heads = lambda a, b, t: a.reshape(b, t, n_heads, HEAD_DIM)
    tail = lambda a: jnp.stack([a[(b + 1) * seq - keep_b:(b + 1) * seq] for b in range(bp)])
    return (y_p, y_s,
            heads(k_p, bp, seq)[None], heads(v_p, bp, seq)[None],
            heads(k_s, bs, t_new)[None], heads(v_s, bs, t_new)[None],
            heads(tail(kb_p), bp, keep_b), heads(tail(vb_p), bp, keep_b),
            heads(kb_s, bs, t_new), heads(vb_s, bs, t_new))
```
